```python
import jax, jax.numpy as jnp
from jax import lax
import numpy as np

D_MODEL = 1024
BATCH = 16
SEQ = 2048
DEPTH = 2

N_MIXERS = 2
PLE_DIM = 256
NORM_EPS = 1e-6
A_HEADS = 8
A_QK_DIM = 128
A_V_DIM = 128
A_Q_RANK = 384
A_KV_RANK = 256
IDX_HEADS = 8
IDX_DIM = 64
TOPK_MAX = 256
Q_BLOCK = 128
A_WIDTH = A_HEADS * A_V_DIM
A_SPLITS = [A_Q_RANK, A_Q_RANK + A_KV_RANK, A_Q_RANK + A_KV_RANK + IDX_DIM, A_Q_RANK + A_KV_RANK + IDX_DIM + IDX_HEADS]
A_IN = A_SPLITS[-1] + A_WIDTH
R_HEADS = 4
R_QK_DIM = D_MODEL // R_HEADS
R_V_DIM = 2 * D_MODEL // R_HEADS
R_WIDTH = R_HEADS * R_V_DIM
R_CHUNK = 128
R_SPLITS = [D_MODEL, 2 * D_MODEL, 2 * D_MODEL + R_WIDTH]
R_IN = R_SPLITS[-1] + R_WIDTH
N_A_LAYERS = (DEPTH + 1) // 2
N_B_LAYERS = DEPTH // 2

kernel_name = "dsa_retention_interleaved_hybrid"


def rmsnorm(x, g):
    xf = x.astype(jnp.float32)
    y = xf * lax.rsqrt(jnp.mean(xf * xf, axis=-1, keepdims=True) + NORM_EPS)
    return (y * g.astype(jnp.float32)).astype(x.dtype)


def layernorm(x, g, b):
    xf = x.astype(jnp.float32)
    mu = jnp.mean(xf, axis=-1, keepdims=True)
    var = jnp.mean(jnp.square(xf - mu), axis=-1, keepdims=True)
    y = (xf - mu) * lax.rsqrt(var + NORM_EPS)
    return (y * g.astype(jnp.float32) + b.astype(jnp.float32)).astype(x.dtype)


def dsa_mixer(h, w_in, g_q, g_kv, w_q_up, w_idx_q, g_ik, b_ik, w_uk, w_uv, w_out):
    B, L, _ = h.shape
    z = h @ w_in
    c_q, c_kv, k_idx, w_idx, gate = jnp.split(z, A_SPLITS, axis=-1)
    c_q = rmsnorm(c_q, g_q)
    c_kv = rmsnorm(c_kv, g_kv)
    q = (c_q @ w_q_up).reshape(B, L, A_HEADS, A_QK_DIM)
    q_abs = jnp.einsum('blhd,hcd->blhc', q, w_uk) * (A_QK_DIM ** -0.5)
    q_idx = (c_q @ w_idx_q).reshape(B, L, IDX_HEADS, IDX_DIM)
    k_idx = layernorm(k_idx, g_ik, b_ik)
    w_idx = w_idx * (IDX_HEADS ** -0.5 * IDX_DIM ** -0.5)
    topk = min(TOPK_MAX, L // 4)
    nblk = L // Q_BLOCK
    key_pos = jnp.arange(L)

    def to_blocks(a):
        return a.reshape(B, nblk, Q_BLOCK, *a.shape[2:]).swapaxes(0, 1)

    def block(args):
        blk, qa, qi, wi = args
        t = blk * Q_BLOCK + jnp.arange(Q_BLOCK)
        sc = jnp.einsum('bqhd,bsd->bqhs', qi, k_idx)
        idx_score = jnp.einsum('bqhs,bqh->bqs', jax.nn.relu(sc), wi).astype(jnp.float32)
        causal = key_pos[None, :] <= t[:, None]
        idx_score = jnp.where(causal[None], idx_score, -jnp.inf)
        _, sel = lax.top_k(idx_score, topk)
        valid = sel <= t[None, :, None]
        kv_sel = jax.vmap(lambda c, i: c[i])(c_kv, sel)
        s = jnp.einsum('bqhc,bqkc->bqhk', qa, kv_sel).astype(jnp.float32)
        s = jnp.where(valid[:, :, None, :], s, -jnp.inf)
        pr = jax.nn.softmax(s, axis=-1).astype(kv_sel.dtype)
        return jnp.einsum('bqhk,bqkc->bqhc', pr, kv_sel)

    o = lax.map(block, (jnp.arange(nblk), to_blocks(q_abs), to_blocks(q_idx), to_blocks(w_idx)))
    o = o.swapaxes(0, 1).reshape(B, L, A_HEADS, A_KV_RANK)
    o = jnp.einsum('blhc,hcv->blhv', o, w_uv).reshape(B, L, A_WIDTH)
    return (o * jax.nn.silu(gate)) @ w_out


def rotate_every_two(x):
    x1 = x[..., ::2]
    x2 = x[..., 1::2]
    return jnp.stack((-x2, x1), axis=-1).reshape(x.shape)


def retention_mixer(h, w_in, w_out):
    B, L, _ = h.shape
    z = h @ w_in
    q, k, v, gate = jnp.split(z, R_SPLITS, axis=-1)
    q = q.reshape(B, L, R_HEADS, R_QK_DIM).astype(jnp.float32)
    k = k.reshape(B, L, R_HEADS, R_QK_DIM).astype(jnp.float32) * (R_QK_DIM ** -0.5)
    v = v.reshape(B, L, R_HEADS, R_V_DIM).astype(jnp.float32)
    pos = jnp.arange(L, dtype=jnp.float32)
    angle = 1.0 / (10000.0 ** jnp.linspace(0.0, 1.0, R_QK_DIM // 2, dtype=jnp.float32))
    angle = jnp.repeat(angle, 2)
    theta = pos[:, None] * angle[None, :]
    sin = jnp.sin(theta)[None, :, None, :]
    cos = jnp.cos(theta)[None, :, None, :]
    q = q * cos + rotate_every_two(q) * sin
    k = k * cos + rotate_every_two(k) * sin
    log_gamma = jnp.log(1.0 - 2.0 ** (-5.0 - jnp.arange(R_HEADS, dtype=jnp.float32)))
    ci = jnp.arange(R_CHUNK, dtype=jnp.float32)
    diff = ci[:, None] - ci[None, :]
    dmask = jnp.where(diff[None] >= 0, jnp.exp(diff[None] * log_gamma[:, None, None]), 0.0)
    xi = jnp.exp((ci[None, :] + 1.0) * log_gamma[:, None]).T
    zeta = jnp.exp((R_CHUNK - 1.0 - ci[None, :]) * log_gamma[:, None])
    chunk_decay = jnp.exp(R_CHUNK * log_gamma)
    n = L // R_CHUNK

    def to_chunks(a):
        return a.reshape(B, n, R_CHUNK, *a.shape[2:]).swapaxes(0, 1)

    def step(state, inp):
        qc, kc, vc = inp
        s = jnp.einsum('bihd,bjhd->bhij', qc, kc) * dmask[None]
        inner = jnp.einsum('bhij,bjhv->bihv', s, vc)
        cross = jnp.einsum('bihd,bhdv->bihv', qc, state) * xi[None, :, :, None]
        state = state * chunk_decay[None, :, None, None] + jnp.einsum('bjhd,bjhv,hj->bhdv', kc, vc, zeta)
        return state, inner + cross

    state0 = jnp.zeros((B, R_HEADS, R_QK_DIM, R_V_DIM), jnp.float32)
    _, o = lax.scan(step, state0, (to_chunks(q), to_chunks(k), to_chunks(v)))
    o = o.swapaxes(0, 1).reshape(B, L, R_HEADS, R_V_DIM)
    mu = jnp.mean(o, axis=-1, keepdims=True)
    var = jnp.mean(jnp.square(o - mu), axis=-1, keepdims=True)
    o = ((o - mu) * lax.rsqrt(var + NORM_EPS)).reshape(B, L, R_WIDTH).astype(h.dtype)
    return (o * jax.nn.silu(gate)) @ w_out


def setup_inputs(seed: int = 0) -> dict:
    key = jax.random.key(seed)
    ks = jax.random.split(key, 24)
    f32 = jnp.float32

    def nrm(k, shape, fan_in):
        return jax.random.normal(k, shape, f32) * (fan_in ** -0.5)

    def gain(k, shape):
        return 1.0 + 0.02 * jax.random.normal(k, shape, f32)

    return {
        "x": jax.random.normal(ks[0], (BATCH, SEQ, D_MODEL), f32),
        "p": jax.random.normal(ks[1], (DEPTH, BATCH, SEQ, PLE_DIM), f32),
        "g_pre": gain(ks[2], (DEPTH, D_MODEL)),
        "a_w_in": nrm(ks[3], (N_A_LAYERS, D_MODEL, A_IN), D_MODEL),
        "a_g_q": gain(ks[4], (N_A_LAYERS, A_Q_RANK)),
        "a_g_kv": gain(ks[5], (N_A_LAYERS, A_KV_RANK)),
        "a_w_q_up": nrm(ks[6], (N_A_LAYERS, A_Q_RANK, A_HEADS * A_QK_DIM), A_Q_RANK),
        "a_w_idx_q": nrm(ks[7], (N_A_LAYERS, A_Q_RANK, IDX_HEADS * IDX_DIM), A_Q_RANK),
        "a_g_ik": gain(ks[8], (N_A_LAYERS, IDX_DIM)),
        "a_b_ik": 0.02 * jax.random.normal(ks[9], (N_A_LAYERS, IDX_DIM), f32),
        "a_w_uk": nrm(ks[10], (N_A_LAYERS, A_HEADS, A_KV_RANK, A_QK_DIM), A_KV_RANK),
        "a_w_uv": nrm(ks[11], (N_A_LAYERS, A_HEADS, A_KV_RANK, A_V_DIM), A_KV_RANK),
        "a_w_out": nrm(ks[12], (N_A_LAYERS, A_WIDTH, D_MODEL), A_WIDTH),
        "r_w_in": nrm(ks[13], (N_B_LAYERS, D_MODEL, R_IN), D_MODEL),
        "r_w_out": nrm(ks[14], (N_B_LAYERS, R_WIDTH, D_MODEL), R_WIDTH),
        "w_ple_gate": nrm(ks[15], (DEPTH, D_MODEL, D_MODEL), D_MODEL),
        "g_ple": gain(ks[16], (DEPTH, D_MODEL)),
        "w_ple": nrm(ks[17], (DEPTH, PLE_DIM, D_MODEL), PLE_DIM),
        "g_final": gain(ks[18], (D_MODEL,)),
    }


def reference(x, p, g_pre, a_w_in, a_g_q, a_g_kv, a_w_q_up, a_w_idx_q, a_g_ik, a_b_ik,
              a_w_uk, a_w_uv, a_w_out, r_w_in, r_w_out, w_ple_gate, g_ple, w_ple, g_final):
    h = x
    for i in range(DEPTH):
        hn = rmsnorm(h, g_pre[i])
        j = i // N_MIXERS
        if i % N_MIXERS == 0:
            y = dsa_mixer(hn, a_w_in[j], a_g_q[j], a_g_kv[j], a_w_q_up[j], a_w_idx_q[j],
                          a_g_ik[j], a_b_ik[j], a_w_uk[j], a_w_uv[j], a_w_out[j])
        else:
            y = retention_mixer(hn, r_w_in[j], r_w_out[j])
        h = h + y
        ple_gate = jax.nn.sigmoid(rmsnorm(h, g_ple[i]) @ w_ple_gate[i])
        h = h + (p[i] @ w_ple[i]) * ple_gate
    return rmsnorm(h, g_final)
```

```python
import functools
import math

import jax
import jax.numpy as jnp
from jax import lax
from jax.experimental import pallas as pl
from jax.experimental.pallas import tpu as pltpu

F32 = jnp.float32
BF16 = jnp.bfloat16

NORM_EPS = 1e-6
TOPK_MAX = 256
Q_BLOCK = 128
KEY_CHUNK = 256
R_CHUNK = 128
LANES = 128
MASK_BIAS = -1e30
VALUE_BISECT_MAX = 34
VMEM_LIMIT = 56 * 1024 * 1024


def _rms(x, g):
    return x * lax.rsqrt(jnp.mean(x * x, axis=-1, keepdims=True) + NORM_EPS) * g


def _dot(a, b):
    return jnp.dot(a, b, preferred_element_type=F32)


def _dot_nt(a, b):
    return lax.dot_general(a, b, (((1,), (1,)), ((), ())), preferred_element_type=F32)


def _dot_tn(a, b):
    return lax.dot_general(a, b, (((0,), (0,)), ((), ())), preferred_element_type=F32)


def _const_spec(shape):
    nd = len(shape)
    return pl.BlockSpec(shape, lambda *_: (0,) * nd)


def _proj0_kernel(x_ref, gpre_ref, w0_ref, gq_ref, gkv_ref, gik_ref, bik_ref, wqup_ref, wukt_ref,
                  widxq_ref, qabs_ref, qidx_ref, ka_ref, kb_ref, wi_ref, ckv_ref, sg_ref,
                  *, q_rank, kv_rank, idx_dim, heads, qk_dim, qk_scale, widx_scale):
    hn = _rms(x_ref[...], gpre_ref[...])
    z = _dot(hn.astype(BF16), w0_ref[...])
    o_kv = q_rank
    o_ki = q_rank + kv_rank
    o_g = o_ki + LANES
    cq = _rms(z[:, :o_kv], gq_ref[...]).astype(BF16)
    ckv_ref[...] = _rms(z[:, o_kv:o_ki], gkv_ref[...]).astype(BF16)
    ki = z[:, o_ki:o_g]
    lane = lax.broadcasted_iota(jnp.int32, (1, LANES), 1)
    in_k = lane < idx_dim
    mu = jnp.sum(jnp.where(in_k, ki, 0.0), axis=-1, keepdims=True) * (1.0 / idx_dim)
    d = jnp.where(in_k, ki - mu, 0.0)
    var = jnp.sum(d * d, axis=-1, keepdims=True) * (1.0 / idx_dim)
    kn = d * lax.rsqrt(var + NORM_EPS) * gik_ref[...] + bik_ref[...]
    ka_ref[...] = kn.astype(BF16)
    kb_ref[...] = pltpu.roll(kn, idx_dim, 1).astype(BF16)
    wi_ref[...] = ki * widx_scale
    g = z[:, o_g:]
    sg_ref[...] = (g * jax.nn.sigmoid(g)).astype(BF16)
    q = _dot(cq, wqup_ref[...])
    kvr = wukt_ref.shape[2]
    for h in range(heads):
        qh = q[:, h * qk_dim:(h + 1) * qk_dim].astype(BF16)
        qabs_ref[:, h * kvr:(h + 1) * kvr] = (_dot(qh, wukt_ref[h]) * qk_scale).astype(BF16)
    qidx_ref[...] = _dot(cq, widxq_ref[...]).astype(BF16)


def _attn_kernel(qidx_ref, ka_ref, kb_ref, wi_ref, qabs_ref, ckv_ref, sg_ref, wuv_ref, og_ref,
                 st_ref, bk_ref, s_ref, m_ref, l_ref, acc_ref,
                 *, topk, heads, idx_dim, kv_rank, v_dim, idx_steps):
    qb = pl.program_id(1)
    n_kc = (qb * Q_BLOCK) // KEY_CHUNK + 1
    kf = float(topk)
    pairs = heads // 2

    qi = qidx_ref[...]
    qst = jnp.concatenate([qi[:, j * LANES:(j + 1) * LANES] for j in range(pairs)], axis=0)
    wit = wi_ref[...].T
    w_rows = [wit[idx_dim + h:idx_dim + h + 1, :] for h in range(heads)]
    tq = qb * Q_BLOCK + lax.broadcasted_iota(jnp.int32, (1, Q_BLOCK), 1)

    def fold8(x, op):
        return op(x.reshape(KEY_CHUNK // 8, 8, Q_BLOCK), axis=0)

    def score_body(c, carry):
        mn, mx, cgt, cge = carry
        k0 = pl.multiple_of(c * KEY_CHUNK, KEY_CHUNK)
        sa = _dot_nt(ka_ref[pl.ds(k0, KEY_CHUNK), :], qst)
        sb = _dot_nt(kb_ref[pl.ds(k0, KEY_CHUNK), :], qst)
        acc = jnp.zeros((KEY_CHUNK, Q_BLOCK), F32)
        for j in range(pairs):
            acc = acc + jnp.maximum(sa[:, j * LANES:(j + 1) * LANES], 0.0) * w_rows[2 * j]
            acc = acc + jnp.maximum(sb[:, j * LANES:(j + 1) * LANES], 0.0) * w_rows[2 * j + 1]
        kpos = k0 + lax.broadcasted_iota(jnp.int32, (KEY_CHUNK, Q_BLOCK), 0)
        causal = kpos <= tq
        s = jnp.where(causal, acc, -jnp.inf)
        st_ref[pl.ds(k0, KEY_CHUNK), :] = s
        mn = jnp.minimum(mn, fold8(jnp.where(causal, acc, jnp.inf), jnp.min))
        mx = jnp.maximum(mx, fold8(s, jnp.max))
        cgt = cgt + fold8(jnp.where(s > 0.0, 1.0, 0.0), jnp.sum)
        cge = cge + fold8(jnp.where(s >= 0.0, 1.0, 0.0), jnp.sum)
        return mn, mx, cgt, cge

    z8 = jnp.zeros((8, Q_BLOCK), F32)
    mn, mx, cgt, cge = lax.fori_loop(0, n_kc, score_body, (z8 + jnp.inf, z8 - jnp.inf, z8, z8))
    mn = jnp.min(mn, axis=0, keepdims=True)
    mx = jnp.max(mx, axis=0, keepdims=True)
    cgt0 = jnp.sum(cgt, axis=0, keepdims=True)
    cge0 = jnp.sum(cge, axis=0, keepdims=True)

    def count_gt(ref, thr):
        def body(c, acc):
            k0 = pl.multiple_of(c * KEY_CHUNK, KEY_CHUNK)
            blk = ref[pl.ds(k0, KEY_CHUNK), :]
            return acc + fold8(jnp.where(blk > thr, 1.0, 0.0), jnp.sum)
        return jnp.sum(lax.fori_loop(0, n_kc, body, z8), axis=0, keepdims=True)

    n_causal = (tq + 1).astype(F32)
    small = n_causal <= kf
    pos = cgt0 >= kf
    zero = jnp.logical_and(jnp.logical_not(pos), cge0 >= kf)
    zero = jnp.logical_and(zero, jnp.logical_not(small))
    lo = jnp.where(pos, 0.0, jnp.where(zero, 0.0, mn * 1.0001))
    hi = jnp.where(pos, mx, 0.0)
    c_lo = jnp.where(pos, cgt0, jnp.where(zero, cgt0, n_causal))
    c_hi = jnp.where(pos, 0.0, cgt0)
    lo = jnp.where(small, -jnp.inf, lo)
    hi = jnp.where(small, -jnp.inf, hi)

    def unresolved(c_lo):
        bad = jnp.logical_and(c_lo != kf, jnp.logical_not(jnp.logical_or(small, zero)))
        return jnp.max(jnp.where(bad, 1.0, 0.0)) > 0.0

    def bis_cond(carry):
        it, lo, hi, c_lo, c_hi = carry
        return jnp.logical_and(it < VALUE_BISECT_MAX, unresolved(c_lo))

    def bis_body(carry):
        it, lo, hi, c_lo, c_hi = carry
        mid = 0.5 * (lo + hi)
        c = count_gt(st_ref, mid)
        ge = c >= kf
        return (it + 1, jnp.where(ge, mid, lo), jnp.where(ge, hi, mid),
                jnp.where(ge, c, c_lo), jnp.where(ge, c_hi, c))

    _, lo, hi, c_lo, c_hi = lax.while_loop(bis_cond, bis_body, (jnp.int32(0), lo, hi, c_lo, c_hi))
    done = jnp.logical_or(c_lo == kf, small)
    thr = jnp.where(done, lo, hi)

    @pl.when(jnp.max(jnp.where(done, 0.0, 1.0)) > 0.0)
    def _():
        need = jnp.where(done, 0.0, kf - c_hi)
        tied = jnp.logical_not(jnp.logical_or(done, zero))

        def band_body(c, _):
            k0 = pl.multiple_of(c * KEY_CHUNK, KEY_CHUNK)
            s = st_ref[pl.ds(k0, KEY_CHUNK), :]
            band = jnp.logical_or(jnp.logical_and(zero, s == 0.0),
                                  jnp.logical_and(tied, jnp.logical_and(s > lo, s <= hi)))
            kpos = (k0 + lax.broadcasted_iota(jnp.int32, (KEY_CHUNK, Q_BLOCK), 0)).astype(F32)
            bk_ref[pl.ds(k0, KEY_CHUNK), :] = jnp.where(band, -kpos, -jnp.inf)
            return 0
        lax.fori_loop(0, n_kc, band_body, 0)

        def idx_body(_, carry):
            jlo, jhi = carry
            jm = jnp.floor(0.5 * (jlo + jhi))
            ok = count_gt(bk_ref, -jm - 0.5) >= need
            return jnp.where(ok, jlo, jm), jnp.where(ok, jm, jhi)
        jlo0 = jnp.full((1, Q_BLOCK), -1.0, F32)
        jhi0 = jnp.full((1, Q_BLOCK), 1.0, F32) * (n_kc * KEY_CHUNK - 1).astype(F32)
        _, jhi = lax.fori_loop(0, idx_steps, idx_body, (jlo0, jhi0))
        jsel = jnp.where(done, -1.0, jhi)

        def mark_body(c, _):
            k0 = pl.multiple_of(c * KEY_CHUNK, KEY_CHUNK)
            s = st_ref[pl.ds(k0, KEY_CHUNK), :]
            take = bk_ref[pl.ds(k0, KEY_CHUNK), :] >= -jsel
            st_ref[pl.ds(k0, KEY_CHUNK), :] = jnp.where(take, jnp.inf, s)
            return 0
        lax.fori_loop(0, n_kc, mark_body, 0)

    qa = qabs_ref[...]
    qa_st = jnp.concatenate([qa[:, h * kv_rank:(h + 1) * kv_rank] for h in range(heads)], axis=0)
    eye = (lax.broadcasted_iota(jnp.int32, (Q_BLOCK, Q_BLOCK), 0)
           == lax.broadcasted_iota(jnp.int32, (Q_BLOCK, Q_BLOCK), 1)).astype(BF16)
    m_ref[...] = jnp.full(m_ref.shape, MASK_BIAS, F32)

    def qk_body(c, _):
        k0 = pl.multiple_of(c * KEY_CHUNK, KEY_CHUNK)
        sel_t = jnp.where(st_ref[pl.ds(k0, KEY_CHUNK), :] > thr, 1.0, 0.0).astype(BF16)
        sel = _dot_nt(eye, sel_t)
        bias = (sel - 1.0) * (-MASK_BIAS)
        s = _dot_nt(qa_st, ckv_ref[pl.ds(k0, KEY_CHUNK), :])
        for h in range(heads):
            sh = s[h * Q_BLOCK:(h + 1) * Q_BLOCK, :] + bias
            s_ref[c, h * Q_BLOCK:(h + 1) * Q_BLOCK, :] = sh
            mh = m_ref[h * Q_BLOCK:(h + 1) * Q_BLOCK, :]
            for t in range(KEY_CHUNK // LANES):
                mh = jnp.maximum(mh, sh[:, t * LANES:(t + 1) * LANES])
            m_ref[h * Q_BLOCK:(h + 1) * Q_BLOCK, :] = mh
        return 0
    lax.fori_loop(0, n_kc, qk_body, 0)

    m_ref[...] = jnp.broadcast_to(jnp.max(m_ref[...], axis=-1, keepdims=True), m_ref.shape)
    l_ref[...] = jnp.zeros(l_ref.shape, F32)
    acc_ref[...] = jnp.zeros(acc_ref.shape, F32)

    def pv_body(c, _):
        k0 = pl.multiple_of(c * KEY_CHUNK, KEY_CHUNK)
        m = m_ref[...]
        ps = [jnp.exp(s_ref[c, :, t * LANES:(t + 1) * LANES] - m) for t in range(KEY_CHUNK // LANES)]
        l_ref[...] += sum(ps[1:], ps[0])
        p = jnp.concatenate([q.astype(BF16) for q in ps], axis=1)
        acc_ref[...] += _dot(p, ckv_ref[pl.ds(k0, KEY_CHUNK), :])
        return 0
    lax.fori_loop(0, n_kc, pv_body, 0)

    inv_l = 1.0 / jnp.sum(l_ref[...], axis=-1, keepdims=True)
    for h in range(heads):
        oh = (acc_ref[h * Q_BLOCK:(h + 1) * Q_BLOCK, :] * inv_l[h * Q_BLOCK:(h + 1) * Q_BLOCK]).astype(BF16)
        ov = _dot(oh, wuv_ref[h])
        og_ref[:, h * v_dim:(h + 1) * v_dim] = (
            ov * sg_ref[:, h * v_dim:(h + 1) * v_dim].astype(F32)).astype(BF16)


def _ple(h, p, gple, wpg, wple):
    gate = jax.nn.sigmoid(_dot(_rms(h, gple).astype(BF16), wpg))
    return h + _dot(p.astype(BF16), wple) * gate


def _rotate(x, cos, sin_next, sin_prev):
    nxt = pltpu.roll(x, LANES - 1, 1)
    prv = pltpu.roll(x, 1, 1)
    return x * cos + nxt * sin_next + prv * sin_prev


def _mid_kernel(og_ref, x_ref, p_ref, wout_ref, gple_ref, wpg_ref, wple_ref, gpre_ref,
                wq_ref, wk_ref, wv_ref, wg_ref, cos_ref, sn_ref, sp_ref,
                h_ref, q_ref, k_ref, v_ref, sg_ref, *, k_scale, qk_dim):
    h = x_ref[...] + _dot(og_ref[...], wout_ref[...])
    h = _ple(h, p_ref[...], gple_ref[...], wpg_ref[...], wple_ref[...])
    h_ref[...] = h
    hb = _rms(h, gpre_ref[...]).astype(BF16)
    q = _dot(hb, wq_ref[...])
    k = _dot(hb, wk_ref[...]) * k_scale
    for j in range(q.shape[1] // LANES):
        t = (j * LANES) % qk_dim
        cs = cos_ref[:, t:t + LANES]
        sn = sn_ref[:, t:t + LANES]
        sp = sp_ref[:, t:t + LANES]
        q_ref[:, j * LANES:(j + 1) * LANES] = _rotate(q[:, j * LANES:(j + 1) * LANES], cs, sn, sp).astype(BF16)
        k_ref[:, j * LANES:(j + 1) * LANES] = _rotate(k[:, j * LANES:(j + 1) * LANES], cs, sn, sp).astype(BF16)
    v_ref[...] = _dot(hb, wv_ref[...]).astype(BF16)
    g = _dot(hb, wg_ref[...])
    sg_ref[...] = (g * jax.nn.sigmoid(g)).astype(BF16)


def _ret_kernel(q_ref, k_ref, v_ref, sg_ref, dmask_ref, xi_ref, zeta_ref, decay_ref, og_ref,
                state_ref, *, heads, qk_dim, v_dim):
    @pl.when(pl.program_id(1) == 0)
    def _():
        state_ref[...] = jnp.zeros(state_ref.shape, F32)

    for h in range(heads):
        qh = q_ref[:, h * qk_dim:(h + 1) * qk_dim]
        kh = k_ref[:, h * qk_dim:(h + 1) * qk_dim]
        vh = v_ref[:, h * v_dim:(h + 1) * v_dim]
        st = state_ref[h]
        s = _dot_nt(qh, kh) * dmask_ref[h]
        inner = _dot(s.astype(BF16), vh)
        cross = _dot(qh, st.astype(BF16)) * xi_ref[h]
        vz = (vh.astype(F32) * zeta_ref[h]).astype(BF16)
        state_ref[h] = st * decay_ref[h] + _dot_tn(kh, vz)
        o = inner + cross
        mu = jnp.mean(o, axis=-1, keepdims=True)
        d = o - mu
        var = jnp.mean(d * d, axis=-1, keepdims=True)
        on = d * lax.rsqrt(var + NORM_EPS)
        og_ref[:, h * v_dim:(h + 1) * v_dim] = (
            on * sg_ref[:, h * v_dim:(h + 1) * v_dim].astype(F32)).astype(BF16)


def _out_kernel(og_ref, h_ref, p_ref, wout_ref, gple_ref, wpg_ref, wple_ref, gfin_ref, o_ref):
    h = h_ref[...] + _dot(og_ref[...], wout_ref[...])
    h = _ple(h, p_ref[...], gple_ref[...], wpg_ref[...], wple_ref[...])
    o_ref[...] = _rms(h, gfin_ref[...])


def _params(*sem):
    return pltpu.CompilerParams(dimension_semantics=sem, vmem_limit_bytes=VMEM_LIMIT)


def kernel(x, p, g_pre, a_w_in, a_g_q, a_g_kv, a_w_q_up, a_w_idx_q, a_g_ik, a_b_ik, a_w_uk, a_w_uv,
           a_w_out, r_w_in, r_w_out, w_ple_gate, g_ple, w_ple, g_final):
    B, L, D = x.shape
    N = B * L
    q_rank = a_g_q.shape[-1]
    kv_rank = a_g_kv.shape[-1]
    idx_dim = a_g_ik.shape[-1]
    heads, _, qk_dim = a_w_uk.shape[1:]
    v_dim = a_w_uv.shape[-1]
    idx_heads = a_w_idx_q.shape[-1] // idx_dim
    a_width = heads * v_dim
    ple_dim = p.shape[-1]
    assert idx_heads == heads and 2 * idx_dim == LANES and L % KEY_CHUNK == 0
    assert q_rank % LANES == 0 and kv_rank % LANES == 0 and a_w_in.shape[0] == 1 and r_w_in.shape[0] == 1
    topk = min(TOPK_MAX, L // 4)
    x2 = x.reshape(N, D)
    row = lambda a: a.reshape(1, -1).astype(F32)

    w_in = a_w_in[0]
    o_ki = q_rank + kv_rank
    o_g = o_ki + idx_dim + idx_heads
    w_ki = jnp.pad(w_in[:, o_ki:o_g], ((0, 0), (0, LANES - (o_g - o_ki))))
    w0 = jnp.concatenate([w_in[:, :o_ki], w_ki, w_in[:, o_g:]], axis=1).astype(BF16)
    pad_k = lambda a: jnp.pad(a.reshape(1, -1).astype(F32), ((0, 0), (0, LANES - idx_dim)))
    wukt = jnp.swapaxes(a_w_uk[0], 1, 2).astype(BF16)
    tm0 = 512
    n0 = w0.shape[1]
    outs0 = pl.pallas_call(
        functools.partial(_proj0_kernel, q_rank=q_rank, kv_rank=kv_rank, idx_dim=idx_dim, heads=heads,
                          qk_dim=qk_dim, qk_scale=qk_dim ** -0.5,
                          widx_scale=idx_heads ** -0.5 * idx_dim ** -0.5),
        grid=(N // tm0,),
        in_specs=[pl.BlockSpec((tm0, D), lambda i: (i, 0)), _const_spec((1, D)), _const_spec((D, n0)),
                  _const_spec((1, q_rank)), _const_spec((1, kv_rank)), _const_spec((1, LANES)),
                  _const_spec((1, LANES)), _const_spec((q_rank, heads * qk_dim)),
                  _const_spec((heads, qk_dim, kv_rank)), _const_spec((q_rank, idx_heads * idx_dim))],
        out_specs=[pl.BlockSpec((tm0, heads * kv_rank), lambda i: (i, 0)),
                   pl.BlockSpec((tm0, idx_heads * idx_dim), lambda i: (i, 0)),
                   pl.BlockSpec((tm0, LANES), lambda i: (i, 0)),
                   pl.BlockSpec((tm0, LANES), lambda i: (i, 0)),
                   pl.BlockSpec((tm0, LANES), lambda i: (i, 0)),
                   pl.BlockSpec((tm0, kv_rank), lambda i: (i, 0)),
                   pl.BlockSpec((tm0, a_width), lambda i: (i, 0))],
        out_shape=[jax.ShapeDtypeStruct((N, heads * kv_rank), BF16),
                   jax.ShapeDtypeStruct((N, idx_heads * idx_dim), BF16),
                   jax.ShapeDtypeStruct((N, LANES), BF16),
                   jax.ShapeDtypeStruct((N, LANES), BF16),
                   jax.ShapeDtypeStruct((N, LANES), F32),
                   jax.ShapeDtypeStruct((N, kv_rank), BF16),
                   jax.ShapeDtypeStruct((N, a_width), BF16)],
        compiler_params=_params("parallel"),
        name="dsa_proj",
    )(x2, row(g_pre[0]), w0, row(a_g_q[0]), row(a_g_kv[0]), pad_k(a_g_ik[0]), pad_k(a_b_ik[0]),
      a_w_q_up[0].astype(BF16), wukt, a_w_idx_q[0].astype(BF16))
    qabs, qidx, ka, kb, wi, ckv, sg0 = outs0

    nb = L // Q_BLOCK
    nkc = L // KEY_CHUNK
    b3 = lambda a: a.reshape(B, L, a.shape[-1])
    qblk = lambda w: pl.BlockSpec((None, Q_BLOCK, w), lambda b, i: (b, i, 0))
    seq = lambda w: pl.BlockSpec((None, L, w), lambda b, i: (b, 0, 0))
    og0 = pl.pallas_call(
        functools.partial(_attn_kernel, topk=topk, heads=heads, idx_dim=idx_dim, kv_rank=kv_rank,
                          v_dim=v_dim, idx_steps=max(1, math.ceil(math.log2(L)))),
        grid=(B, nb),
        in_specs=[qblk(idx_heads * idx_dim), seq(LANES), seq(LANES), qblk(LANES), qblk(heads * kv_rank),
                  seq(kv_rank), qblk(a_width), _const_spec((heads, kv_rank, v_dim))],
        out_specs=qblk(a_width),
        out_shape=jax.ShapeDtypeStruct((B, L, a_width), BF16),
        scratch_shapes=[pltpu.VMEM((L, Q_BLOCK), F32), pltpu.VMEM((L, Q_BLOCK), F32),
                        pltpu.VMEM((nkc, heads * Q_BLOCK, KEY_CHUNK), F32),
                        pltpu.VMEM((heads * Q_BLOCK, LANES), F32),
                        pltpu.VMEM((heads * Q_BLOCK, LANES), F32),
                        pltpu.VMEM((heads * Q_BLOCK, kv_rank), F32)],
        compiler_params=_params("parallel", "arbitrary"),
        name="dsa_attn",
    )(b3(qidx), b3(ka), b3(kb), b3(wi), b3(qabs), b3(ckv), b3(sg0), a_w_uv[0].astype(BF16))

    r_heads = 4
    r_qk = D // r_heads
    r_v = 2 * D // r_heads
    r_width = r_heads * r_v
    assert r_w_in.shape[-1] == 2 * D + 2 * r_width and r_qk % LANES == 0 and L % R_CHUNK == 0
    wr = r_w_in[0].astype(BF16)
    wq, wk, wv, wg = wr[:, :D], wr[:, D:2 * D], wr[:, 2 * D:2 * D + r_width], wr[:, 2 * D + r_width:]
    pos = jnp.arange(L, dtype=F32)
    angle = 1.0 / (10000.0 ** jnp.linspace(0.0, 1.0, r_qk // 2, dtype=F32))
    theta = pos[:, None] * jnp.repeat(angle, 2)[None, :]
    even = (jnp.arange(r_qk) % 2 == 0)[None, :]
    cos_t = jnp.cos(theta)
    sin_next = jnp.where(even, -jnp.sin(theta), 0.0)
    sin_prev = jnp.where(even, 0.0, jnp.sin(theta))
    tm1 = 256
    tok = lambda w: pl.BlockSpec((tm1, w), lambda b, i: (b * (L // tm1) + i, 0))
    tab = pl.BlockSpec((tm1, r_qk), lambda b, i: (i, 0))
    h1, rq, rk, rv, sg1 = pl.pallas_call(
        functools.partial(_mid_kernel, k_scale=r_qk ** -0.5, qk_dim=r_qk),
        grid=(B, L // tm1),
        in_specs=[tok(a_width), tok(D), tok(ple_dim), _const_spec((a_width, D)), _const_spec((1, D)),
                  _const_spec((D, D)), _const_spec((ple_dim, D)), _const_spec((1, D)),
                  _const_spec((D, D)), _const_spec((D, D)), _const_spec((D, r_width)),
                  _const_spec((D, r_width)), tab, tab, tab],
        out_specs=[tok(D), tok(D), tok(D), tok(r_width), tok(r_width)],
        out_shape=[jax.ShapeDtypeStruct((N, D), F32), jax.ShapeDtypeStruct((N, D), BF16),
                   jax.ShapeDtypeStruct((N, D), BF16), jax.ShapeDtypeStruct((N, r_width), BF16),
                   jax.ShapeDtypeStruct((N, r_width), BF16)],
        compiler_params=_params("parallel", "arbitrary"),
        name="mid_proj",
    )(og0.reshape(N, a_width), x2, p[0].reshape(N, ple_dim), a_w_out[0].astype(BF16), row(g_ple[0]),
      w_ple_gate[0].astype(BF16), w_ple[0].astype(BF16), row(g_pre[1]), wq, wk, wv, wg,
      cos_t, sin_next, sin_prev)

    log_gamma = jnp.log(1.0 - 2.0 ** (-5.0 - jnp.arange(r_heads, dtype=F32)))
    ci = jnp.arange(R_CHUNK, dtype=F32)
    diff = ci[:, None] - ci[None, :]
    dmask = jnp.where(diff[None] >= 0, jnp.exp(diff[None] * log_gamma[:, None, None]), 0.0)
    xi = jnp.exp((ci[None, :] + 1.0) * log_gamma[:, None])
    zeta = jnp.exp((R_CHUNK - 1.0 - ci[None, :]) * log_gamma[:, None])
    decay = jnp.exp(R_CHUNK * log_gamma)
    xi_b = jnp.broadcast_to(xi[:, :, None], (r_heads, R_CHUNK, r_v))
    zeta_b = jnp.broadcast_to(zeta[:, :, None], (r_heads, R_CHUNK, r_v))
    decay_b = jnp.broadcast_to(decay[:, None, None], (r_heads, 1, r_v))
    nch = L // R_CHUNK
    cblk = lambda w: pl.BlockSpec((None, R_CHUNK, w), lambda b, c: (b, c, 0))
    og1 = pl.pallas_call(
        functools.partial(_ret_kernel, heads=r_heads, qk_dim=r_qk, v_dim=r_v),
        grid=(B, nch),
        in_specs=[cblk(D), cblk(D), cblk(r_width), cblk(r_width),
                  _const_spec((r_heads, R_CHUNK, R_CHUNK)), _const_spec((r_heads, R_CHUNK, r_v)),
                  _const_spec((r_heads, R_CHUNK, r_v)), _const_spec((r_heads, 1, r_v))],
        out_specs=cblk(r_width),
        out_shape=jax.ShapeDtypeStruct((B, L, r_width), BF16),
        scratch_shapes=[pltpu.VMEM((r_heads, r_qk, r_v), F32)],
        compiler_params=_params("parallel", "arbitrary"),
        name="retention",
    )(rq.reshape(B, L, D), rk.reshape(B, L, D), rv.reshape(B, L, r_width), sg1.reshape(B, L, r_width),
      dmask, xi_b, zeta_b, decay_b)

    tm2 = 512
    tk2 = lambda w: pl.BlockSpec((tm2, w), lambda i: (i, 0))
    out = pl.pallas_call(
        _out_kernel,
        grid=(N // tm2,),
        in_specs=[tk2(r_width), tk2(D), tk2(ple_dim), _const_spec((r_width, D)), _const_spec((1, D)),
                  _const_spec((D, D)), _const_spec((ple_dim, D)), _const_spec((1, D))],
        out_specs=tk2(D),
        out_shape=jax.ShapeDtypeStruct((N, D), F32),
        compiler_params=_params("parallel"),
        name="out_proj",
    )(og1.reshape(N, r_width), h1, p[1].reshape(N, ple_dim), r_w_out[0].astype(BF16), row(g_ple[1]),
      w_ple_gate[1].astype(BF16), w_ple[1].astype(BF16), row(g_final))
    return out.reshape(B, L, D)
```

```python
import functools
import math

import jax
import jax.numpy as jnp
from jax import lax
from jax.experimental import pallas as pl
from jax.experimental.pallas import tpu as pltpu

F32 = jnp.float32
BF16 = jnp.bfloat16

NORM_EPS = 1e-6
TOPK_MAX = 256
Q_BLOCK = 128
KEY_CHUNK = 256
R_CHUNK = 128
LANES = 128
MASK_BIAS = -1e30
VALUE_BISECT_MAX = 34
VMEM_LIMIT = 56 * 1024 * 1024


def _rms(x, g):
    return x * lax.rsqrt(jnp.mean(x * x, axis=-1, keepdims=True) + NORM_EPS) * g


def _dot(a, b):
    return jnp.dot(a, b, preferred_element_type=F32)


def _dot_nt(a, b):
    return lax.dot_general(a, b, (((1,), (1,)), ((), ())), preferred_element_type=F32)


def _dot_tn(a, b):
    return lax.dot_general(a, b, (((0,), (0,)), ((), ())), preferred_element_type=F32)


def _const_spec(shape):
    nd = len(shape)
    return pl.BlockSpec(shape, lambda *_: (0,) * nd)


def _proj0_kernel(x_ref, gpre_ref, w0_ref, gq_ref, gkv_ref, gik_ref, bik_ref, wqup_ref, wukt_ref,
                  widxq_ref, qabs_ref, qidx_ref, ka_ref, kb_ref, wi_ref, ckv_ref, sg_ref,
                  *, q_rank, kv_rank, idx_dim, heads, qk_dim, qk_scale, widx_scale):
    hn = _rms(x_ref[...], gpre_ref[...])
    z = _dot(hn.astype(BF16), w0_ref[...])
    o_kv = q_rank
    o_ki = q_rank + kv_rank
    o_g = o_ki + LANES
    cq = _rms(z[:, :o_kv], gq_ref[...]).astype(BF16)
    ckv_ref[...] = _rms(z[:, o_kv:o_ki], gkv_ref[...]).astype(BF16)
    ki = z[:, o_ki:o_g]
    lane = lax.broadcasted_iota(jnp.int32, (1, LANES), 1)
    in_k = lane < idx_dim
    mu = jnp.sum(jnp.where(in_k, ki, 0.0), axis=-1, keepdims=True) * (1.0 / idx_dim)
    d = jnp.where(in_k, ki - mu, 0.0)
    var = jnp.sum(d * d, axis=-1, keepdims=True) * (1.0 / idx_dim)
    kn = d * lax.rsqrt(var + NORM_EPS) * gik_ref[...] + bik_ref[...]
    ka_ref[...] = kn.astype(BF16)
    kb_ref[...] = pltpu.roll(kn, idx_dim, 1).astype(BF16)
    wi_ref[...] = ki * widx_scale
    g = z[:, o_g:]
    sg_ref[...] = (g * jax.nn.sigmoid(g)).astype(BF16)
    q = _dot(cq, wqup_ref[...])
    kvr = wukt_ref.shape[2]
    for h in range(heads):
        qh = q[:, h * qk_dim:(h + 1) * qk_dim].astype(BF16)
        qabs_ref[:, h * kvr:(h + 1) * kvr] = (_dot(qh, wukt_ref[h]) * qk_scale).astype(BF16)
    qidx_ref[...] = _dot(cq, widxq_ref[...]).astype(BF16)


def _attn_kernel(qidx_ref, ka_ref, kb_ref, wi_ref, qabs_ref, ckv_ref, sg_ref, wuv_ref, og_ref,
                 st_ref, bk_ref, s_ref, p_ref, m_ref, acc_ref, l_ref,
                 *, topk, heads, idx_dim, kv_rank, v_dim, idx_steps, max_chunks):
    qb = pl.program_id(1)
    n_kc = (qb * Q_BLOCK) // KEY_CHUNK + 1
    kf = float(topk)
    pairs = heads // 2
    sub = KEY_CHUNK // LANES

    qi = qidx_ref[...]
    qst = jnp.concatenate([qi[:, j * LANES:(j + 1) * LANES] for j in range(pairs)], axis=0)
    wit = wi_ref[...].T
    w_rows = [wit[idx_dim + h:idx_dim + h + 1, :] for h in range(heads)]
    tq = qb * Q_BLOCK + lax.broadcasted_iota(jnp.int32, (1, Q_BLOCK), 1)
    n_causal = (tq + 1).astype(F32)
    small = n_causal <= kf
    qa = qabs_ref[...]
    qa_st = jnp.concatenate([qa[:, h * kv_rank:(h + 1) * kv_rank] for h in range(heads)], axis=0)
    eye = (lax.broadcasted_iota(jnp.int32, (Q_BLOCK, Q_BLOCK), 0)
           == lax.broadcasted_iota(jnp.int32, (Q_BLOCK, Q_BLOCK), 1)).astype(BF16)

    def fold8(x, op):
        return op(x.reshape(x.shape[0] // 8, 8, Q_BLOCK), axis=0)

    def variant(nc):
        chunks = [(c, c * KEY_CHUNK) for c in range(nc)]

        mn = mx = cgt = cge = None
        for c, k0 in chunks:
            sa = _dot_nt(ka_ref[k0:k0 + KEY_CHUNK, :], qst)
            sb = _dot_nt(kb_ref[k0:k0 + KEY_CHUNK, :], qst)
            acc = None
            for j in range(pairs):
                ta = jnp.maximum(sa[:, j * LANES:(j + 1) * LANES], 0.0) * w_rows[2 * j]
                tb = jnp.maximum(sb[:, j * LANES:(j + 1) * LANES], 0.0) * w_rows[2 * j + 1]
                acc = ta + tb if acc is None else acc + ta + tb
            if c == nc - 1:
                kpos = k0 + lax.broadcasted_iota(jnp.int32, (KEY_CHUNK, Q_BLOCK), 0)
                causal = kpos <= tq
                s = jnp.where(causal, acc, -jnp.inf)
                s_min = jnp.where(causal, acc, jnp.inf)
            else:
                s = s_min = acc
            st_ref[k0:k0 + KEY_CHUNK, :] = s
            parts = (fold8(s_min, jnp.min), fold8(s, jnp.max),
                     fold8(jnp.where(s > 0.0, 1.0, 0.0), jnp.sum),
                     fold8(jnp.where(s >= 0.0, 1.0, 0.0), jnp.sum))
            if mn is None:
                mn, mx, cgt, cge = parts
            else:
                mn, mx = jnp.minimum(mn, parts[0]), jnp.maximum(mx, parts[1])
                cgt, cge = cgt + parts[2], cge + parts[3]
        mn = jnp.min(mn, axis=0, keepdims=True)
        mx = jnp.max(mx, axis=0, keepdims=True)
        cgt0 = jnp.sum(cgt, axis=0, keepdims=True)
        cge0 = jnp.sum(cge, axis=0, keepdims=True)

        def count_gt(ref, thr):
            acc = None
            for _, k0 in chunks:
                part = fold8(jnp.where(ref[k0:k0 + KEY_CHUNK, :] > thr, 1.0, 0.0), jnp.sum)
                acc = part if acc is None else acc + part
            return jnp.sum(acc, axis=0, keepdims=True)

        pos = cgt0 >= kf
        zero = jnp.logical_and(jnp.logical_not(pos), cge0 >= kf)
        zero = jnp.logical_and(zero, jnp.logical_not(small))
        lo = jnp.where(pos, 0.0, jnp.where(zero, 0.0, mn * 1.0001))
        hi = jnp.where(pos, mx, 0.0)
        c_lo = jnp.where(pos, cgt0, jnp.where(zero, cgt0, n_causal))
        c_hi = jnp.where(pos, 0.0, cgt0)
        lo = jnp.where(small, -jnp.inf, lo)
        hi = jnp.where(small, -jnp.inf, hi)
        searching = jnp.logical_not(jnp.logical_or(small, zero))

        def unresolved(c_lo):
            bad = jnp.logical_and(c_lo != kf, searching)
            return (jnp.max(jnp.where(bad, 1.0, 0.0)) > 0.0).astype(jnp.int32)

        def bis_cond(carry):
            return jnp.logical_and(carry[0] < VALUE_BISECT_MAX, carry[1] > 0)

        def bis_body(carry):
            it, _, lo, hi, c_lo, c_hi = carry
            flag = unresolved(c_lo)
            mid = 0.5 * (lo + hi)
            c = count_gt(st_ref, mid)
            ge = c >= kf
            return (it + 1, flag, jnp.where(ge, mid, lo), jnp.where(ge, hi, mid),
                    jnp.where(ge, c, c_lo), jnp.where(ge, c_hi, c))

        _, _, lo, hi, c_lo, c_hi = lax.while_loop(
            bis_cond, bis_body, (jnp.int32(0), jnp.int32(1), lo, hi, c_lo, c_hi))
        done = jnp.logical_or(c_lo == kf, small)
        thr = jnp.where(done, lo, hi)

        @pl.when(jnp.max(jnp.where(done, 0.0, 1.0)) > 0.0)
        def _():
            need = jnp.where(done, 0.0, kf - c_hi)
            tied = jnp.logical_not(jnp.logical_or(done, zero))
            for _, k0 in chunks:
                s = st_ref[k0:k0 + KEY_CHUNK, :]
                band = jnp.logical_or(jnp.logical_and(zero, s == 0.0),
                                      jnp.logical_and(tied, jnp.logical_and(s > lo, s <= hi)))
                kpos = (k0 + lax.broadcasted_iota(jnp.int32, (KEY_CHUNK, Q_BLOCK), 0)).astype(F32)
                bk_ref[k0:k0 + KEY_CHUNK, :] = jnp.where(band, -kpos, -jnp.inf)

            def idx_body(_, carry):
                jlo, jhi = carry
                jm = jnp.floor(0.5 * (jlo + jhi))
                ok = count_gt(bk_ref, -jm - 0.5) >= need
                return jnp.where(ok, jlo, jm), jnp.where(ok, jm, jhi)
            jlo0 = jnp.full((1, Q_BLOCK), -1.0, F32)
            jhi0 = jnp.full((1, Q_BLOCK), float(nc * KEY_CHUNK - 1), F32)
            _, jhi = lax.fori_loop(0, idx_steps, idx_body, (jlo0, jhi0))
            jsel = jnp.where(done, -1.0, jhi)
            for _, k0 in chunks:
                take = bk_ref[k0:k0 + KEY_CHUNK, :] >= -jsel
                st_ref[k0:k0 + KEY_CHUNK, :] = jnp.where(take, jnp.inf, st_ref[k0:k0 + KEY_CHUNK, :])

        for c, k0 in chunks:
            sel_t = jnp.where(st_ref[k0:k0 + KEY_CHUNK, :] > thr, 1.0, 0.0).astype(BF16)
            sel = _dot_nt(eye, sel_t)
            bias = (sel - 1.0) * (-MASK_BIAS)
            s = _dot_nt(qa_st, ckv_ref[k0:k0 + KEY_CHUNK, :])
            for h in range(heads):
                rows = slice(h * Q_BLOCK, (h + 1) * Q_BLOCK)
                sh = s[rows, :] + bias
                s_ref[rows, k0:k0 + KEY_CHUNK] = sh
                mh = sh[:, :LANES]
                for t in range(1, sub):
                    mh = jnp.maximum(mh, sh[:, t * LANES:(t + 1) * LANES])
                m_ref[rows, :] = mh if c == 0 else jnp.maximum(m_ref[rows, :], mh)

        for h in range(heads):
            rows = slice(h * Q_BLOCK, (h + 1) * Q_BLOCK)
            m = jnp.broadcast_to(jnp.max(m_ref[rows, :], axis=-1, keepdims=True), (Q_BLOCK, LANES))
            lsum = None
            for t in range(nc * sub):
                pt = jnp.exp(s_ref[rows, t * LANES:(t + 1) * LANES] - m)
                p_ref[rows, t * LANES:(t + 1) * LANES] = pt.astype(BF16)
                lsum = pt if lsum is None else lsum + pt
            l_ref[rows, :] = lsum
        nk = nc * KEY_CHUNK
        acc_ref[...] = _dot(p_ref[:, :nk], ckv_ref[:nk, :])

    for nc in range(1, max_chunks + 1):
        pl.when(n_kc == nc)(functools.partial(variant, nc))

    inv_l = 1.0 / jnp.sum(l_ref[...], axis=-1, keepdims=True)
    for h in range(heads):
        oh = (acc_ref[h * Q_BLOCK:(h + 1) * Q_BLOCK, :] * inv_l[h * Q_BLOCK:(h + 1) * Q_BLOCK]).astype(BF16)
        ov = _dot(oh, wuv_ref[h])
        og_ref[:, h * v_dim:(h + 1) * v_dim] = (
            ov * sg_ref[:, h * v_dim:(h + 1) * v_dim].astype(F32)).astype(BF16)


def _ple(h, p, gple, wpg, wple):
    gate = jax.nn.sigmoid(_dot(_rms(h, gple).astype(BF16), wpg))
    return h + _dot(p.astype(BF16), wple) * gate


def _rotate(x, cos, sin_next, sin_prev):
    nxt = pltpu.roll(x, LANES - 1, 1)
    prv = pltpu.roll(x, 1, 1)
    return x * cos + nxt * sin_next + prv * sin_prev


def _mid_kernel(og_ref, x_ref, p_ref, wout_ref, gple_ref, wpg_ref, wple_ref, gpre_ref,
                wq_ref, wk_ref, wv_ref, wg_ref, cos_ref, sn_ref, sp_ref,
                h_ref, q_ref, k_ref, v_ref, sg_ref, *, k_scale, qk_dim):
    h = x_ref[...] + _dot(og_ref[...], wout_ref[...])
    h = _ple(h, p_ref[...], gple_ref[...], wpg_ref[...], wple_ref[...])
    h_ref[...] = h
    hb = _rms(h, gpre_ref[...]).astype(BF16)
    q = _dot(hb, wq_ref[...])
    k = _dot(hb, wk_ref[...]) * k_scale
    for j in range(q.shape[1] // LANES):
        t = (j * LANES) % qk_dim
        cs = cos_ref[:, t:t + LANES]
        sn = sn_ref[:, t:t + LANES]
        sp = sp_ref[:, t:t + LANES]
        q_ref[:, j * LANES:(j + 1) * LANES] = _rotate(q[:, j * LANES:(j + 1) * LANES], cs, sn, sp).astype(BF16)
        k_ref[:, j * LANES:(j + 1) * LANES] = _rotate(k[:, j * LANES:(j + 1) * LANES], cs, sn, sp).astype(BF16)
    v_ref[...] = _dot(hb, wv_ref[...]).astype(BF16)
    g = _dot(hb, wg_ref[...])
    sg_ref[...] = (g * jax.nn.sigmoid(g)).astype(BF16)


def _ret_kernel(q_ref, k_ref, v_ref, sg_ref, dmask_ref, xi_ref, zeta_ref, decay_ref, og_ref,
                state_ref, *, heads, qk_dim, v_dim):
    @pl.when(pl.program_id(1) == 0)
    def _():
        state_ref[...] = jnp.zeros(state_ref.shape, F32)

    for h in range(heads):
        qh = q_ref[:, h * qk_dim:(h + 1) * qk_dim]
        kh = k_ref[:, h * qk_dim:(h + 1) * qk_dim]
        vh = v_ref[:, h * v_dim:(h + 1) * v_dim]
        st = state_ref[h]
        s = _dot_nt(qh, kh) * dmask_ref[h]
        inner = _dot(s.astype(BF16), vh)
        cross = _dot(qh, st.astype(BF16)) * xi_ref[h]
        vz = (vh.astype(F32) * zeta_ref[h]).astype(BF16)
        state_ref[h] = st * decay_ref[h] + _dot_tn(kh, vz)
        o = inner + cross
        mu = jnp.mean(o, axis=-1, keepdims=True)
        d = o - mu
        var = jnp.mean(d * d, axis=-1, keepdims=True)
        on = d * lax.rsqrt(var + NORM_EPS)
        og_ref[:, h * v_dim:(h + 1) * v_dim] = (
            on * sg_ref[:, h * v_dim:(h + 1) * v_dim].astype(F32)).astype(BF16)


def _out_kernel(og_ref, h_ref, p_ref, wout_ref, gple_ref, wpg_ref, wple_ref, gfin_ref, o_ref):
    h = h_ref[...] + _dot(og_ref[...], wout_ref[...])
    h = _ple(h, p_ref[...], gple_ref[...], wpg_ref[...], wple_ref[...])
    o_ref[...] = _rms(h, gfin_ref[...])


def _params(*sem):
    return pltpu.CompilerParams(dimension_semantics=sem, vmem_limit_bytes=VMEM_LIMIT)


def kernel(x, p, g_pre, a_w_in, a_g_q, a_g_kv, a_w_q_up, a_w_idx_q, a_g_ik, a_b_ik, a_w_uk, a_w_uv,
           a_w_out, r_w_in, r_w_out, w_ple_gate, g_ple, w_ple, g_final):
    B, L, D = x.shape
    N = B * L
    q_rank = a_g_q.shape[-1]
    kv_rank = a_g_kv.shape[-1]
    idx_dim = a_g_ik.shape[-1]
    heads, _, qk_dim = a_w_uk.shape[1:]
    v_dim = a_w_uv.shape[-1]
    idx_heads = a_w_idx_q.shape[-1] // idx_dim
    a_width = heads * v_dim
    ple_dim = p.shape[-1]
    assert idx_heads == heads and 2 * idx_dim == LANES and L % KEY_CHUNK == 0
    assert q_rank % LANES == 0 and kv_rank % LANES == 0 and a_w_in.shape[0] == 1 and r_w_in.shape[0] == 1
    topk = min(TOPK_MAX, L // 4)
    x2 = x.reshape(N, D)
    row = lambda a: a.reshape(1, -1).astype(F32)

    w_in = a_w_in[0]
    o_ki = q_rank + kv_rank
    o_g = o_ki + idx_dim + idx_heads
    w_ki = jnp.pad(w_in[:, o_ki:o_g], ((0, 0), (0, LANES - (o_g - o_ki))))
    w0 = jnp.concatenate([w_in[:, :o_ki], w_ki, w_in[:, o_g:]], axis=1).astype(BF16)
    pad_k = lambda a: jnp.pad(a.reshape(1, -1).astype(F32), ((0, 0), (0, LANES - idx_dim)))
    wukt = jnp.swapaxes(a_w_uk[0], 1, 2).astype(BF16)
    tm0 = 512
    n0 = w0.shape[1]
    outs0 = pl.pallas_call(
        functools.partial(_proj0_kernel, q_rank=q_rank, kv_rank=kv_rank, idx_dim=idx_dim, heads=heads,
                          qk_dim=qk_dim, qk_scale=qk_dim ** -0.5,
                          widx_scale=idx_heads ** -0.5 * idx_dim ** -0.5),
        grid=(N // tm0,),
        in_specs=[pl.BlockSpec((tm0, D), lambda i: (i, 0)), _const_spec((1, D)), _const_spec((D, n0)),
                  _const_spec((1, q_rank)), _const_spec((1, kv_rank)), _const_spec((1, LANES)),
                  _const_spec((1, LANES)), _const_spec((q_rank, heads * qk_dim)),
                  _const_spec((heads, qk_dim, kv_rank)), _const_spec((q_rank, idx_heads * idx_dim))],
        out_specs=[pl.BlockSpec((tm0, heads * kv_rank), lambda i: (i, 0)),
                   pl.BlockSpec((tm0, idx_heads * idx_dim), lambda i: (i, 0)),
                   pl.BlockSpec((tm0, LANES), lambda i: (i, 0)),
                   pl.BlockSpec((tm0, LANES), lambda i: (i, 0)),
                   pl.BlockSpec((tm0, LANES), lambda i: (i, 0)),
                   pl.BlockSpec((tm0, kv_rank), lambda i: (i, 0)),
                   pl.BlockSpec((tm0, a_width), lambda i: (i, 0))],
        out_shape=[jax.ShapeDtypeStruct((N, heads * kv_rank), BF16),
                   jax.ShapeDtypeStruct((N, idx_heads * idx_dim), BF16),
                   jax.ShapeDtypeStruct((N, LANES), BF16),
                   jax.ShapeDtypeStruct((N, LANES), BF16),
                   jax.ShapeDtypeStruct((N, LANES), F32),
                   jax.ShapeDtypeStruct((N, kv_rank), BF16),
                   jax.ShapeDtypeStruct((N, a_width), BF16)],
        compiler_params=_params("parallel"),
        name="dsa_proj",
    )(x2, row(g_pre[0]), w0, row(a_g_q[0]), row(a_g_kv[0]), pad_k(a_g_ik[0]), pad_k(a_b_ik[0]),
      a_w_q_up[0].astype(BF16), wukt, a_w_idx_q[0].astype(BF16))
    qabs, qidx, ka, kb, wi, ckv, sg0 = outs0

    nb = L // Q_BLOCK
    nkc = L // KEY_CHUNK
    b3 = lambda a: a.reshape(B, L, a.shape[-1])
    qblk = lambda w: pl.BlockSpec((None, Q_BLOCK, w), lambda b, i: (b, i, 0))
    seq = lambda w: pl.BlockSpec((None, L, w), lambda b, i: (b, 0, 0))
    og0 = pl.pallas_call(
        functools.partial(_attn_kernel, topk=topk, heads=heads, idx_dim=idx_dim, kv_rank=kv_rank,
                          v_dim=v_dim, idx_steps=max(1, math.ceil(math.log2(L))), max_chunks=nkc),
        grid=(B, nb),
        in_specs=[qblk(idx_heads * idx_dim), seq(LANES), seq(LANES), qblk(LANES), qblk(heads * kv_rank),
                  seq(kv_rank), qblk(a_width), _const_spec((heads, kv_rank, v_dim))],
        out_specs=qblk(a_width),
        out_shape=jax.ShapeDtypeStruct((B, L, a_width), BF16),
        scratch_shapes=[pltpu.VMEM((L, Q_BLOCK), F32), pltpu.VMEM((L, Q_BLOCK), F32),
                        pltpu.VMEM((heads * Q_BLOCK, L), F32),
                        pltpu.VMEM((heads * Q_BLOCK, L), BF16),
                        pltpu.VMEM((heads * Q_BLOCK, LANES), F32),
                        pltpu.VMEM((heads * Q_BLOCK, kv_rank), F32),
                        pltpu.VMEM((heads * Q_BLOCK, LANES), F32)],
        compiler_params=_params("parallel", "arbitrary"),
        name="dsa_attn",
    )(b3(qidx), b3(ka), b3(kb), b3(wi), b3(qabs), b3(ckv), b3(sg0), a_w_uv[0].astype(BF16))

    r_heads = 4
    r_qk = D // r_heads
    r_v = 2 * D // r_heads
    r_width = r_heads * r_v
    assert r_w_in.shape[-1] == 2 * D + 2 * r_width and r_qk % LANES == 0 and L % R_CHUNK == 0
    wr = r_w_in[0].astype(BF16)
    wq, wk, wv, wg = wr[:, :D], wr[:, D:2 * D], wr[:, 2 * D:2 * D + r_width], wr[:, 2 * D + r_width:]
    pos = jnp.arange(L, dtype=F32)
    angle = 1.0 / (10000.0 ** jnp.linspace(0.0, 1.0, r_qk // 2, dtype=F32))
    theta = pos[:, None] * jnp.repeat(angle, 2)[None, :]
    even = (jnp.arange(r_qk) % 2 == 0)[None, :]
    cos_t = jnp.cos(theta)
    sin_next = jnp.where(even, -jnp.sin(theta), 0.0)
    sin_prev = jnp.where(even, 0.0, jnp.sin(theta))
    tm1 = 256
    tok = lambda w: pl.BlockSpec((tm1, w), lambda b, i: (b * (L // tm1) + i, 0))
    tab = pl.BlockSpec((tm1, r_qk), lambda b, i: (i, 0))
    h1, rq, rk, rv, sg1 = pl.pallas_call(
        functools.partial(_mid_kernel, k_scale=r_qk ** -0.5, qk_dim=r_qk),
        grid=(B, L // tm1),
        in_specs=[tok(a_width), tok(D), tok(ple_dim), _const_spec((a_width, D)), _const_spec((1, D)),
                  _const_spec((D, D)), _const_spec((ple_dim, D)), _const_spec((1, D)),
                  _const_spec((D, D)), _const_spec((D, D)), _const_spec((D, r_width)),
                  _const_spec((D, r_width)), tab, tab, tab],
        out_specs=[tok(D), tok(D), tok(D), tok(r_width), tok(r_width)],
        out_shape=[jax.ShapeDtypeStruct((N, D), F32), jax.ShapeDtypeStruct((N, D), BF16),
                   jax.ShapeDtypeStruct((N, D), BF16), jax.ShapeDtypeStruct((N, r_width), BF16),
                   jax.ShapeDtypeStruct((N, r_width), BF16)],
        compiler_params=_params("parallel", "arbitrary"),
        name="mid_proj",
    )(og0.reshape(N, a_width), x2, p[0].reshape(N, ple_dim), a_w_out[0].astype(BF16), row(g_ple[0]),
      w_ple_gate[0].astype(BF16), w_ple[0].astype(BF16), row(g_pre[1]), wq, wk, wv, wg,
      cos_t, sin_next, sin_prev)

    log_gamma = jnp.log(1.0 - 2.0 ** (-5.0 - jnp.arange(r_heads, dtype=F32)))
    ci = jnp.arange(R_CHUNK, dtype=F32)
    diff = ci[:, None] - ci[None, :]
    dmask = jnp.where(diff[None] >= 0, jnp.exp(diff[None] * log_gamma[:, None, None]), 0.0)
    xi = jnp.exp((ci[None, :] + 1.0) * log_gamma[:, None])
    zeta = jnp.exp((R_CHUNK - 1.0 - ci[None, :]) * log_gamma[:, None])
    decay = jnp.exp(R_CHUNK * log_gamma)
    xi_b = jnp.broadcast_to(xi[:, :, None], (r_heads, R_CHUNK, r_v))
    zeta_b = jnp.broadcast_to(zeta[:, :, None], (r_heads, R_CHUNK, r_v))
    decay_b = jnp.broadcast_to(decay[:, None, None], (r_heads, 1, r_v))
    nch = L // R_CHUNK
    cblk = lambda w: pl.BlockSpec((None, R_CHUNK, w), lambda b, c: (b, c, 0))
    og1 = pl.pallas_call(
        functools.partial(_ret_kernel, heads=r_heads, qk_dim=r_qk, v_dim=r_v),
        grid=(B, nch),
        in_specs=[cblk(D), cblk(D), cblk(r_width), cblk(r_width),
                  _const_spec((r_heads, R_CHUNK, R_CHUNK)), _const_spec((r_heads, R_CHUNK, r_v)),
                  _const_spec((r_heads, R_CHUNK, r_v)), _const_spec((r_heads, 1, r_v))],
        out_specs=cblk(r_width),
        out_shape=jax.ShapeDtypeStruct((B, L, r_width), BF16),
        scratch_shapes=[pltpu.VMEM((r_heads, r_qk, r_v), F32)],
        compiler_params=_params("parallel", "arbitrary"),
        name="retention",
    )(rq.reshape(B, L, D), rk.reshape(B, L, D), rv.reshape(B, L, r_width), sg1.reshape(B, L, r_width),
      dmask, xi_b, zeta_b, decay_b)

    tm2 = 512
    tk2 = lambda w: pl.BlockSpec((tm2, w), lambda i: (i, 0))
    out = pl.pallas_call(
        _out_kernel,
        grid=(N // tm2,),
        in_specs=[tk2(r_width), tk2(D), tk2(ple_dim), _const_spec((r_width, D)), _const_spec((1, D)),
                  _const_spec((D, D)), _const_spec((ple_dim, D)), _const_spec((1, D))],
        out_specs=tk2(D),
        out_shape=jax.ShapeDtypeStruct((N, D), F32),
        compiler_params=_params("parallel"),
        name="out_proj",
    )(og1.reshape(N, r_width), h1, p[1].reshape(N, ple_dim), r_w_out[0].astype(BF16), row(g_ple[1]),
      w_ple_gate[1].astype(BF16), w_ple[1].astype(BF16), row(g_final))
    return out.reshape(B, L, D)
```

```python
import functools
import math

import jax
import jax.numpy as jnp
from jax import lax
from jax.experimental import pallas as pl
from jax.experimental.pallas import tpu as pltpu

F32 = jnp.float32
BF16 = jnp.bfloat16

NORM_EPS = 1e-6
TOPK_MAX = 256
Q_BLOCK = 128
KEY_CHUNK = 256
R_CHUNK = 128
LANES = 128
MASK_BIAS = -1e30
VALUE_BISECT_MAX = 34
VMEM_LIMIT = 56 * 1024 * 1024
HEAD_GROUPS = 2
R_STEP_CHUNKS = 4


def _passes_per_body(n_chunks):
    return max(1, min(4, round(8 / (n_chunks + 1))))


def _rms(x, g):
    return x * lax.rsqrt(jnp.mean(x * x, axis=-1, keepdims=True) + NORM_EPS) * g


def _dot(a, b):
    return jnp.dot(a, b, preferred_element_type=F32)


def _dot_nt(a, b):
    return lax.dot_general(a, b, (((1,), (1,)), ((), ())), preferred_element_type=F32)


def _dot_tn(a, b):
    return lax.dot_general(a, b, (((0,), (0,)), ((), ())), preferred_element_type=F32)


def _const_spec(shape):
    nd = len(shape)
    return pl.BlockSpec(shape, lambda *_: (0,) * nd)


def _proj0_kernel(x_ref, gpre_ref, w0_ref, gq_ref, gkv_ref, gik_ref, bik_ref, wqup_ref, wukt_ref,
                  widxq_ref, qabs_ref, qidx_ref, ka_ref, kb_ref, wi_ref, ckv_ref, sg_ref,
                  *, q_rank, kv_rank, idx_dim, heads, qk_dim, qk_scale, widx_scale):
    hn = _rms(x_ref[...], gpre_ref[...])
    z = _dot(hn.astype(BF16), w0_ref[...])
    o_kv = q_rank
    o_ki = q_rank + kv_rank
    o_g = o_ki + LANES
    cq = _rms(z[:, :o_kv], gq_ref[...]).astype(BF16)
    ckv_ref[...] = _rms(z[:, o_kv:o_ki], gkv_ref[...]).astype(BF16)
    ki = z[:, o_ki:o_g]
    lane = lax.broadcasted_iota(jnp.int32, (1, LANES), 1)
    in_k = lane < idx_dim
    mu = jnp.sum(jnp.where(in_k, ki, 0.0), axis=-1, keepdims=True) * (1.0 / idx_dim)
    d = jnp.where(in_k, ki - mu, 0.0)
    var = jnp.sum(d * d, axis=-1, keepdims=True) * (1.0 / idx_dim)
    kn = d * lax.rsqrt(var + NORM_EPS) * gik_ref[...] + bik_ref[...]
    ka_ref[...] = kn.astype(BF16)
    kb_ref[...] = pltpu.roll(kn, idx_dim, 1).astype(BF16)
    wi_ref[...] = ki * widx_scale
    g = z[:, o_g:]
    sg_ref[...] = (g * jax.nn.sigmoid(g)).astype(BF16)
    q = _dot(cq, wqup_ref[...])
    kvr = wukt_ref.shape[2]
    for h in range(heads):
        qh = q[:, h * qk_dim:(h + 1) * qk_dim].astype(BF16)
        qabs_ref[:, h * kvr:(h + 1) * kvr] = (_dot(qh, wukt_ref[h]) * qk_scale).astype(BF16)
    qidx_ref[...] = _dot(cq, widxq_ref[...]).astype(BF16)


def _attn_kernel(qidx_ref, ka_ref, kb_ref, wi_ref, qabs_ref, ckv_ref, sg_ref, wuv_ref, og_ref,
                 st_ref, bk_ref, s_ref, p_ref, m_ref, acc_ref, l_ref, bias_ref,
                 *, topk, heads, idx_dim, kv_rank, v_dim, idx_steps, max_chunks):
    qb = pl.program_id(1)
    n_kc = (qb * Q_BLOCK) // KEY_CHUNK + 1
    kf = float(topk)
    pairs = heads // 2
    sub = KEY_CHUNK // LANES

    qi = qidx_ref[...]
    qst = jnp.concatenate([qi[:, j * LANES:(j + 1) * LANES] for j in range(pairs)], axis=0)
    wit = wi_ref[...].T
    w_rows = [wit[idx_dim + h:idx_dim + h + 1, :] for h in range(heads)]
    tq = qb * Q_BLOCK + lax.broadcasted_iota(jnp.int32, (1, Q_BLOCK), 1)
    n_causal = (tq + 1).astype(F32)
    small = n_causal <= kf
    qa = qabs_ref[...]
    qa_st = jnp.concatenate([qa[:, h * kv_rank:(h + 1) * kv_rank] for h in range(heads)], axis=0)
    eye = (lax.broadcasted_iota(jnp.int32, (Q_BLOCK, Q_BLOCK), 0)
           == lax.broadcasted_iota(jnp.int32, (Q_BLOCK, Q_BLOCK), 1)).astype(BF16)

    def fold8(x, op):
        return op(x.reshape(x.shape[0] // 8, 8, Q_BLOCK), axis=0)

    def variant(nc):
        chunks = [(c, c * KEY_CHUNK) for c in range(nc)]

        mn = mx = cgt = cge = None
        for c, k0 in chunks:
            sa = _dot_nt(ka_ref[k0:k0 + KEY_CHUNK, :], qst)
            sb = _dot_nt(kb_ref[k0:k0 + KEY_CHUNK, :], qst)
            acc = None
            for j in range(pairs):
                ta = jnp.maximum(sa[:, j * LANES:(j + 1) * LANES], 0.0) * w_rows[2 * j]
                tb = jnp.maximum(sb[:, j * LANES:(j + 1) * LANES], 0.0) * w_rows[2 * j + 1]
                acc = ta + tb if acc is None else acc + ta + tb
            if c == nc - 1:
                kpos = k0 + lax.broadcasted_iota(jnp.int32, (KEY_CHUNK, Q_BLOCK), 0)
                causal = kpos <= tq
                s = jnp.where(causal, acc, -jnp.inf)
                s_min = jnp.where(causal, acc, jnp.inf)
            else:
                s = s_min = acc
            st_ref[k0:k0 + KEY_CHUNK, :] = s
            parts = (fold8(s_min, jnp.min), fold8(s, jnp.max),
                     fold8(jnp.where(s > 0.0, 1.0, 0.0), jnp.sum),
                     fold8(jnp.where(s >= 0.0, 1.0, 0.0), jnp.sum))
            if mn is None:
                mn, mx, cgt, cge = parts
            else:
                mn, mx = jnp.minimum(mn, parts[0]), jnp.maximum(mx, parts[1])
                cgt, cge = cgt + parts[2], cge + parts[3]
        mn = jnp.min(mn, axis=0, keepdims=True)
        mx = jnp.max(mx, axis=0, keepdims=True)
        cgt0 = jnp.sum(cgt, axis=0, keepdims=True)
        cge0 = jnp.sum(cge, axis=0, keepdims=True)

        def count_gt(ref, thr):
            acc = None
            for _, k0 in chunks:
                part = fold8(jnp.where(ref[k0:k0 + KEY_CHUNK, :] > thr, 1.0, 0.0), jnp.sum)
                acc = part if acc is None else acc + part
            return jnp.sum(acc, axis=0, keepdims=True)

        pos = cgt0 >= kf
        zero = jnp.logical_and(jnp.logical_not(pos), cge0 >= kf)
        zero = jnp.logical_and(zero, jnp.logical_not(small))
        lo = jnp.where(pos, 0.0, jnp.where(zero, 0.0, mn * 1.0001))
        hi = jnp.where(pos, mx, 0.0)
        c_lo = jnp.where(pos, cgt0, jnp.where(zero, cgt0, n_causal))
        c_hi = jnp.where(pos, 0.0, cgt0)
        lo = jnp.where(small, -jnp.inf, lo)
        hi = jnp.where(small, -jnp.inf, hi)
        searching = jnp.logical_not(jnp.logical_or(small, zero))

        def unresolved(c_lo):
            bad = jnp.logical_and(c_lo != kf, searching)
            return (jnp.max(jnp.where(bad, 1.0, 0.0)) > 0.0).astype(jnp.int32)

        def bis_cond(carry):
            return jnp.logical_and(carry[0] < VALUE_BISECT_MAX, carry[1] > 0)

        def bis_body(carry):
            it, _, lo, hi, c_lo, c_hi = carry
            flag = unresolved(c_lo)
            for _ in range(_passes_per_body(nc)):
                mid = 0.5 * (lo + hi)
                c = count_gt(st_ref, mid)
                ge = c >= kf
                lo, hi = jnp.where(ge, mid, lo), jnp.where(ge, hi, mid)
                c_lo, c_hi = jnp.where(ge, c, c_lo), jnp.where(ge, c_hi, c)
            return it + _passes_per_body(nc), flag, lo, hi, c_lo, c_hi

        _, _, lo, hi, c_lo, c_hi = lax.while_loop(
            bis_cond, bis_body, (jnp.int32(0), jnp.int32(1), lo, hi, c_lo, c_hi))
        done = jnp.logical_or(c_lo == kf, small)
        thr = jnp.where(done, lo, hi)

        @pl.when(jnp.max(jnp.where(done, 0.0, 1.0)) > 0.0)
        def _():
            need = jnp.where(done, 0.0, kf - c_hi)
            tied = jnp.logical_not(jnp.logical_or(done, zero))
            for _, k0 in chunks:
                s = st_ref[k0:k0 + KEY_CHUNK, :]
                band = jnp.logical_or(jnp.logical_and(zero, s == 0.0),
                                      jnp.logical_and(tied, jnp.logical_and(s > lo, s <= hi)))
                kpos = (k0 + lax.broadcasted_iota(jnp.int32, (KEY_CHUNK, Q_BLOCK), 0)).astype(F32)
                bk_ref[k0:k0 + KEY_CHUNK, :] = jnp.where(band, -kpos, -jnp.inf)

            def idx_body(_, carry):
                jlo, jhi = carry
                jm = jnp.floor(0.5 * (jlo + jhi))
                ok = count_gt(bk_ref, -jm - 0.5) >= need
                return jnp.where(ok, jlo, jm), jnp.where(ok, jm, jhi)
            jlo0 = jnp.full((1, Q_BLOCK), -1.0, F32)
            jhi0 = jnp.full((1, Q_BLOCK), float(nc * KEY_CHUNK - 1), F32)
            _, jhi = lax.fori_loop(0, idx_steps, idx_body, (jlo0, jhi0))
            jsel = jnp.where(done, -1.0, jhi)
            for _, k0 in chunks:
                take = bk_ref[k0:k0 + KEY_CHUNK, :] >= -jsel
                st_ref[k0:k0 + KEY_CHUNK, :] = jnp.where(take, jnp.inf, st_ref[k0:k0 + KEY_CHUNK, :])

        for c, k0 in chunks:
            sel_t = jnp.where(st_ref[k0:k0 + KEY_CHUNK, :] > thr, 1.0, 0.0).astype(BF16)
            sel = _dot_nt(eye, sel_t)
            bias_ref[:, k0:k0 + KEY_CHUNK] = (sel - 1.0) * (-MASK_BIAS)

        nk = nc * KEY_CHUNK
        gsz = heads // HEAD_GROUPS
        grows = [slice(g * gsz * Q_BLOCK, (g + 1) * gsz * Q_BLOCK) for g in range(HEAD_GROUPS)]
        for g in range(HEAD_GROUPS):
            for c, k0 in chunks:
                s = _dot_nt(qa_st[grows[g], :], ckv_ref[k0:k0 + KEY_CHUNK, :])
                for i in range(gsz):
                    rows = slice((g * gsz + i) * Q_BLOCK, (g * gsz + i + 1) * Q_BLOCK)
                    sh = s[i * Q_BLOCK:(i + 1) * Q_BLOCK, :] + bias_ref[:, k0:k0 + KEY_CHUNK]
                    s_ref[rows, k0:k0 + KEY_CHUNK] = sh
                    mh = sh[:, :LANES]
                    for t in range(1, sub):
                        mh = jnp.maximum(mh, sh[:, t * LANES:(t + 1) * LANES])
                    m_ref[rows, :] = mh if c == 0 else jnp.maximum(m_ref[rows, :], mh)
        for g in range(HEAD_GROUPS):
            for i in range(gsz):
                rows = slice((g * gsz + i) * Q_BLOCK, (g * gsz + i + 1) * Q_BLOCK)
                m = jnp.broadcast_to(jnp.max(m_ref[rows, :], axis=-1, keepdims=True), (Q_BLOCK, LANES))
                lsum = None
                for t in range(nc * sub):
                    pt = jnp.exp(s_ref[rows, t * LANES:(t + 1) * LANES] - m)
                    p_ref[rows, t * LANES:(t + 1) * LANES] = pt.astype(BF16)
                    lsum = pt if lsum is None else lsum + pt
                l_ref[rows, :] = lsum
            acc_ref[grows[g], :] = _dot(p_ref[grows[g], :nk], ckv_ref[:nk, :])

    for nc in range(1, max_chunks + 1):
        pl.when(n_kc == nc)(functools.partial(variant, nc))

    inv_l = 1.0 / jnp.sum(l_ref[...], axis=-1, keepdims=True)
    for h in range(heads):
        oh = (acc_ref[h * Q_BLOCK:(h + 1) * Q_BLOCK, :] * inv_l[h * Q_BLOCK:(h + 1) * Q_BLOCK]).astype(BF16)
        ov = _dot(oh, wuv_ref[h])
        og_ref[:, h * v_dim:(h + 1) * v_dim] = (
            ov * sg_ref[:, h * v_dim:(h + 1) * v_dim].astype(F32)).astype(BF16)


def _ple(h, p, gple, wpg, wple):
    gate = jax.nn.sigmoid(_dot(_rms(h, gple).astype(BF16), wpg))
    return h + _dot(p.astype(BF16), wple) * gate


def _rotate(x, cos, sin_next, sin_prev):
    nxt = pltpu.roll(x, LANES - 1, 1)
    prv = pltpu.roll(x, 1, 1)
    return x * cos + nxt * sin_next + prv * sin_prev


def _mid_kernel(og_ref, x_ref, p_ref, wout_ref, gple_ref, wpg_ref, wple_ref, gpre_ref,
                wq_ref, wk_ref, wv_ref, wg_ref, cos_ref, sn_ref, sp_ref,
                h_ref, q_ref, k_ref, v_ref, sg_ref, *, k_scale, qk_dim):
    h = x_ref[...] + _dot(og_ref[...], wout_ref[...])
    h = _ple(h, p_ref[...], gple_ref[...], wpg_ref[...], wple_ref[...])
    h_ref[...] = h
    hb = _rms(h, gpre_ref[...]).astype(BF16)
    q = _dot(hb, wq_ref[...])
    k = _dot(hb, wk_ref[...]) * k_scale
    for j in range(q.shape[1] // LANES):
        t = (j * LANES) % qk_dim
        cs = cos_ref[:, t:t + LANES]
        sn = sn_ref[:, t:t + LANES]
        sp = sp_ref[:, t:t + LANES]
        q_ref[:, j * LANES:(j + 1) * LANES] = _rotate(q[:, j * LANES:(j + 1) * LANES], cs, sn, sp).astype(BF16)
        k_ref[:, j * LANES:(j + 1) * LANES] = _rotate(k[:, j * LANES:(j + 1) * LANES], cs, sn, sp).astype(BF16)
    v_ref[...] = _dot(hb, wv_ref[...]).astype(BF16)
    g = _dot(hb, wg_ref[...])
    sg_ref[...] = (g * jax.nn.sigmoid(g)).astype(BF16)


def _ret_kernel(q_ref, k_ref, v_ref, sg_ref, dmask_ref, xi_ref, zeta_ref, decay_ref, og_ref,
                state_ref, *, heads, qk_dim, v_dim):
    @pl.when(pl.program_id(1) == 0)
    def _():
        state_ref[...] = jnp.zeros(state_ref.shape, F32)

    for c in range(q_ref.shape[0] // R_CHUNK):
        r = slice(c * R_CHUNK, (c + 1) * R_CHUNK)
        for h in range(heads):
            qh = q_ref[r, h * qk_dim:(h + 1) * qk_dim]
            kh = k_ref[r, h * qk_dim:(h + 1) * qk_dim]
            vh = v_ref[r, h * v_dim:(h + 1) * v_dim]
            s = _dot_nt(qh, kh) * dmask_ref[h]
            o = _dot(s.astype(BF16), vh) + _dot(qh, state_ref[h].astype(BF16)) * xi_ref[h]
            kz = (kh.astype(F32) * zeta_ref[h]).astype(BF16)
            state_ref[h] = state_ref[h] * decay_ref[h] + _dot_tn(kz, vh)
            mu = jnp.mean(o, axis=-1, keepdims=True)
            d = o - mu
            var = jnp.mean(d * d, axis=-1, keepdims=True)
            on = d * lax.rsqrt(var + NORM_EPS)
            og_ref[r, h * v_dim:(h + 1) * v_dim] = (
                on * sg_ref[r, h * v_dim:(h + 1) * v_dim].astype(F32)).astype(BF16)


def _out_kernel(og_ref, h_ref, p_ref, wout_ref, gple_ref, wpg_ref, wple_ref, gfin_ref, o_ref):
    h = h_ref[...] + _dot(og_ref[...], wout_ref[...])
    h = _ple(h, p_ref[...], gple_ref[...], wpg_ref[...], wple_ref[...])
    o_ref[...] = _rms(h, gfin_ref[...])


def _params(*sem):
    return pltpu.CompilerParams(dimension_semantics=sem, vmem_limit_bytes=VMEM_LIMIT)


def kernel(x, p, g_pre, a_w_in, a_g_q, a_g_kv, a_w_q_up, a_w_idx_q, a_g_ik, a_b_ik, a_w_uk, a_w_uv,
           a_w_out, r_w_in, r_w_out, w_ple_gate, g_ple, w_ple, g_final):
    B, L, D = x.shape
    N = B * L
    q_rank = a_g_q.shape[-1]
    kv_rank = a_g_kv.shape[-1]
    idx_dim = a_g_ik.shape[-1]
    heads, _, qk_dim = a_w_uk.shape[1:]
    v_dim = a_w_uv.shape[-1]
    idx_heads = a_w_idx_q.shape[-1] // idx_dim
    a_width = heads * v_dim
    ple_dim = p.shape[-1]
    assert idx_heads == heads and 2 * idx_dim == LANES and L % KEY_CHUNK == 0
    assert q_rank % LANES == 0 and kv_rank % LANES == 0 and a_w_in.shape[0] == 1 and r_w_in.shape[0] == 1
    topk = min(TOPK_MAX, L // 4)
    x2 = x.reshape(N, D)
    row = lambda a: a.reshape(1, -1).astype(F32)

    w_in = a_w_in[0]
    o_ki = q_rank + kv_rank
    o_g = o_ki + idx_dim + idx_heads
    w_ki = jnp.pad(w_in[:, o_ki:o_g], ((0, 0), (0, LANES - (o_g - o_ki))))
    w0 = jnp.concatenate([w_in[:, :o_ki], w_ki, w_in[:, o_g:]], axis=1).astype(BF16)
    pad_k = lambda a: jnp.pad(a.reshape(1, -1).astype(F32), ((0, 0), (0, LANES - idx_dim)))
    wukt = jnp.swapaxes(a_w_uk[0], 1, 2).astype(BF16)
    tm0 = 512
    n0 = w0.shape[1]
    outs0 = pl.pallas_call(
        functools.partial(_proj0_kernel, q_rank=q_rank, kv_rank=kv_rank, idx_dim=idx_dim, heads=heads,
                          qk_dim=qk_dim, qk_scale=qk_dim ** -0.5,
                          widx_scale=idx_heads ** -0.5 * idx_dim ** -0.5),
        grid=(N // tm0,),
        in_specs=[pl.BlockSpec((tm0, D), lambda i: (i, 0)), _const_spec((1, D)), _const_spec((D, n0)),
                  _const_spec((1, q_rank)), _const_spec((1, kv_rank)), _const_spec((1, LANES)),
                  _const_spec((1, LANES)), _const_spec((q_rank, heads * qk_dim)),
                  _const_spec((heads, qk_dim, kv_rank)), _const_spec((q_rank, idx_heads * idx_dim))],
        out_specs=[pl.BlockSpec((tm0, heads * kv_rank), lambda i: (i, 0)),
                   pl.BlockSpec((tm0, idx_heads * idx_dim), lambda i: (i, 0)),
                   pl.BlockSpec((tm0, LANES), lambda i: (i, 0)),
                   pl.BlockSpec((tm0, LANES), lambda i: (i, 0)),
                   pl.BlockSpec((tm0, LANES), lambda i: (i, 0)),
                   pl.BlockSpec((tm0, kv_rank), lambda i: (i, 0)),
                   pl.BlockSpec((tm0, a_width), lambda i: (i, 0))],
        out_shape=[jax.ShapeDtypeStruct((N, heads * kv_rank), BF16),
                   jax.ShapeDtypeStruct((N, idx_heads * idx_dim), BF16),
                   jax.ShapeDtypeStruct((N, LANES), BF16),
                   jax.ShapeDtypeStruct((N, LANES), BF16),
                   jax.ShapeDtypeStruct((N, LANES), F32),
                   jax.ShapeDtypeStruct((N, kv_rank), BF16),
                   jax.ShapeDtypeStruct((N, a_width), BF16)],
        compiler_params=_params("parallel"),
        name="dsa_proj",
    )(x2, row(g_pre[0]), w0, row(a_g_q[0]), row(a_g_kv[0]), pad_k(a_g_ik[0]), pad_k(a_b_ik[0]),
      a_w_q_up[0].astype(BF16), wukt, a_w_idx_q[0].astype(BF16))
    qabs, qidx, ka, kb, wi, ckv, sg0 = outs0

    nb = L // Q_BLOCK
    nkc = L // KEY_CHUNK
    b3 = lambda a: a.reshape(B, L, a.shape[-1])
    qblk = lambda w: pl.BlockSpec((None, Q_BLOCK, w), lambda b, i: (b, i, 0))
    seq = lambda w: pl.BlockSpec((None, L, w), lambda b, i: (b, 0, 0))
    og0 = pl.pallas_call(
        functools.partial(_attn_kernel, topk=topk, heads=heads, idx_dim=idx_dim, kv_rank=kv_rank,
                          v_dim=v_dim, idx_steps=max(1, math.ceil(math.log2(L))), max_chunks=nkc),
        grid=(B, nb),
        in_specs=[qblk(idx_heads * idx_dim), seq(LANES), seq(LANES), qblk(LANES), qblk(heads * kv_rank),
                  seq(kv_rank), qblk(a_width), _const_spec((heads, kv_rank, v_dim))],
        out_specs=qblk(a_width),
        out_shape=jax.ShapeDtypeStruct((B, L, a_width), BF16),
        scratch_shapes=[pltpu.VMEM((L, Q_BLOCK), F32), pltpu.VMEM((L, Q_BLOCK), F32),
                        pltpu.VMEM((heads * Q_BLOCK, L), F32),
                        pltpu.VMEM((heads * Q_BLOCK, L), BF16),
                        pltpu.VMEM((heads * Q_BLOCK, LANES), F32),
                        pltpu.VMEM((heads * Q_BLOCK, kv_rank), F32),
                        pltpu.VMEM((heads * Q_BLOCK, LANES), F32),
                        pltpu.VMEM((Q_BLOCK, L), F32)],
        compiler_params=_params("parallel", "arbitrary"),
        name="dsa_attn",
    )(b3(qidx), b3(ka), b3(kb), b3(wi), b3(qabs), b3(ckv), b3(sg0), a_w_uv[0].astype(BF16))

    r_heads = 4
    r_qk = D // r_heads
    r_v = 2 * D // r_heads
    r_width = r_heads * r_v
    assert r_w_in.shape[-1] == 2 * D + 2 * r_width and r_qk % LANES == 0 and L % R_CHUNK == 0
    wr = r_w_in[0].astype(BF16)
    wq, wk, wv, wg = wr[:, :D], wr[:, D:2 * D], wr[:, 2 * D:2 * D + r_width], wr[:, 2 * D + r_width:]
    pos = jnp.arange(L, dtype=F32)
    angle = 1.0 / (10000.0 ** jnp.linspace(0.0, 1.0, r_qk // 2, dtype=F32))
    theta = pos[:, None] * jnp.repeat(angle, 2)[None, :]
    even = (jnp.arange(r_qk) % 2 == 0)[None, :]
    cos_t = jnp.cos(theta)
    sin_next = jnp.where(even, -jnp.sin(theta), 0.0)
    sin_prev = jnp.where(even, 0.0, jnp.sin(theta))
    tm1 = 256
    tok = lambda w: pl.BlockSpec((tm1, w), lambda b, i: (b * (L // tm1) + i, 0))
    tab = pl.BlockSpec((tm1, r_qk), lambda b, i: (i, 0))
    h1, rq, rk, rv, sg1 = pl.pallas_call(
        functools.partial(_mid_kernel, k_scale=r_qk ** -0.5, qk_dim=r_qk),
        grid=(B, L // tm1),
        in_specs=[tok(a_width), tok(D), tok(ple_dim), _const_spec((a_width, D)), _const_spec((1, D)),
                  _const_spec((D, D)), _const_spec((ple_dim, D)), _const_spec((1, D)),
                  _const_spec((D, D)), _const_spec((D, D)), _const_spec((D, r_width)),
                  _const_spec((D, r_width)), tab, tab, tab],
        out_specs=[tok(D), tok(D), tok(D), tok(r_width), tok(r_width)],
        out_shape=[jax.ShapeDtypeStruct((N, D), F32), jax.ShapeDtypeStruct((N, D), BF16),
                   jax.ShapeDtypeStruct((N, D), BF16), jax.ShapeDtypeStruct((N, r_width), BF16),
                   jax.ShapeDtypeStruct((N, r_width), BF16)],
        compiler_params=_params("parallel", "arbitrary"),
        name="mid_proj",
    )(og0.reshape(N, a_width), x2, p[0].reshape(N, ple_dim), a_w_out[0].astype(BF16), row(g_ple[0]),
      w_ple_gate[0].astype(BF16), w_ple[0].astype(BF16), row(g_pre[1]), wq, wk, wv, wg,
      cos_t, sin_next, sin_prev)

    log_gamma = jnp.log(1.0 - 2.0 ** (-5.0 - jnp.arange(r_heads, dtype=F32)))
    ci = jnp.arange(R_CHUNK, dtype=F32)
    diff = ci[:, None] - ci[None, :]
    dmask = jnp.where(diff[None] >= 0, jnp.exp(diff[None] * log_gamma[:, None, None]), 0.0)
    xi = jnp.exp((ci[None, :] + 1.0) * log_gamma[:, None])
    zeta = jnp.exp((R_CHUNK - 1.0 - ci[None, :]) * log_gamma[:, None])
    decay = jnp.exp(R_CHUNK * log_gamma)
    xi_b = jnp.broadcast_to(xi[:, :, None], (r_heads, R_CHUNK, r_v))
    zeta_b = jnp.broadcast_to(zeta[:, :, None], (r_heads, R_CHUNK, r_qk))
    decay_b = jnp.broadcast_to(decay[:, None, None], (r_heads, 1, r_v))
    rows_step = R_CHUNK * math.gcd(R_STEP_CHUNKS, L // R_CHUNK)
    cblk = lambda w: pl.BlockSpec((None, rows_step, w), lambda b, c: (b, c, 0))
    og1 = pl.pallas_call(
        functools.partial(_ret_kernel, heads=r_heads, qk_dim=r_qk, v_dim=r_v),
        grid=(B, L // rows_step),
        in_specs=[cblk(D), cblk(D), cblk(r_width), cblk(r_width),
                  _const_spec((r_heads, R_CHUNK, R_CHUNK)), _const_spec((r_heads, R_CHUNK, r_v)),
                  _const_spec((r_heads, R_CHUNK, r_qk)), _const_spec((r_heads, 1, r_v))],
        out_specs=cblk(r_width),
        out_shape=jax.ShapeDtypeStruct((B, L, r_width), BF16),
        scratch_shapes=[pltpu.VMEM((r_heads, r_qk, r_v), F32)],
        compiler_params=_params("parallel", "arbitrary"),
        name="retention",
    )(rq.reshape(B, L, D), rk.reshape(B, L, D), rv.reshape(B, L, r_width), sg1.reshape(B, L, r_width),
      dmask, xi_b, zeta_b, decay_b)

    tm2 = 512
    tk2 = lambda w: pl.BlockSpec((tm2, w), lambda i: (i, 0))
    out = pl.pallas_call(
        _out_kernel,
        grid=(N // tm2,),
        in_specs=[tk2(r_width), tk2(D), tk2(ple_dim), _const_spec((r_width, D)), _const_spec((1, D)),
                  _const_spec((D, D)), _const_spec((ple_dim, D)), _const_spec((1, D))],
        out_specs=tk2(D),
        out_shape=jax.ShapeDtypeStruct((N, D), F32),
        compiler_params=_params("parallel"),
        name="out_proj",
    )(og1.reshape(N, r_width), h1, p[1].reshape(N, ple_dim), r_w_out[0].astype(BF16), row(g_ple[1]),
      w_ple_gate[1].astype(BF16), w_ple[1].astype(BF16), row(g_final))
    return out.reshape(B, L, D)
```

```python
import functools
import math

import jax
import jax.numpy as jnp
from jax import lax
from jax.experimental import pallas as pl
from jax.experimental.pallas import tpu as pltpu

F32 = jnp.float32
BF16 = jnp.bfloat16

NORM_EPS = 1e-6
TOPK_MAX = 256
Q_BLOCK = 256
KEY_CHUNK = 256
R_CHUNK = 128
LANES = 128
MASK_BIAS = -1e30
VALUE_BISECT_MAX = 34
VMEM_LIMIT = 56 * 1024 * 1024
HEAD_GROUPS = 4
R_STEP_CHUNKS = 4


def _passes_per_body(n_chunks):
    return max(1, min(4, round(4 / n_chunks)))


def _rms(x, g):
    return x * lax.rsqrt(jnp.mean(x * x, axis=-1, keepdims=True) + NORM_EPS) * g


def _dot(a, b):
    return jnp.dot(a, b, preferred_element_type=F32)


def _dot_nt(a, b):
    return lax.dot_general(a, b, (((1,), (1,)), ((), ())), preferred_element_type=F32)


def _dot_tn(a, b):
    return lax.dot_general(a, b, (((0,), (0,)), ((), ())), preferred_element_type=F32)


def _const_spec(shape):
    nd = len(shape)
    return pl.BlockSpec(shape, lambda *_: (0,) * nd)


def _proj0_kernel(x_ref, gpre_ref, w0_ref, gq_ref, gkv_ref, gik_ref, bik_ref, wqup_ref, wukt_ref,
                  widxq_ref, qabs_ref, qidx_ref, ka_ref, kb_ref, wi_ref, ckv_ref, sg_ref,
                  *, q_rank, kv_rank, idx_dim, heads, qk_dim, qk_scale, widx_scale):
    hn = _rms(x_ref[...], gpre_ref[...])
    z = _dot(hn.astype(BF16), w0_ref[...])
    o_kv = q_rank
    o_ki = q_rank + kv_rank
    o_g = o_ki + LANES
    cq = _rms(z[:, :o_kv], gq_ref[...]).astype(BF16)
    ckv_ref[...] = _rms(z[:, o_kv:o_ki], gkv_ref[...]).astype(BF16)
    ki = z[:, o_ki:o_g]
    lane = lax.broadcasted_iota(jnp.int32, (1, LANES), 1)
    in_k = lane < idx_dim
    mu = jnp.sum(jnp.where(in_k, ki, 0.0), axis=-1, keepdims=True) * (1.0 / idx_dim)
    d = jnp.where(in_k, ki - mu, 0.0)
    var = jnp.sum(d * d, axis=-1, keepdims=True) * (1.0 / idx_dim)
    kn = d * lax.rsqrt(var + NORM_EPS) * gik_ref[...] + bik_ref[...]
    ka_ref[...] = kn.astype(BF16)
    kb_ref[...] = pltpu.roll(kn, idx_dim, 1).astype(BF16)
    wi_ref[...] = ki * widx_scale
    g = z[:, o_g:]
    sg_ref[...] = (g * jax.nn.sigmoid(g)).astype(BF16)
    q = _dot(cq, wqup_ref[...])
    kvr = wukt_ref.shape[2]
    for h in range(heads):
        qh = q[:, h * qk_dim:(h + 1) * qk_dim].astype(BF16)
        qabs_ref[:, h * kvr:(h + 1) * kvr] = (_dot(qh, wukt_ref[h]) * qk_scale).astype(BF16)
    qidx_ref[...] = _dot(cq, widxq_ref[...]).astype(BF16)


def _attn_kernel(qidx_ref, ka_ref, kb_ref, wi_ref, qabs_ref, ckv_ref, sg_ref, wuv_ref, og_ref,
                 st_ref, bk_ref, s_ref, p_ref, m_ref, acc_ref, l_ref, bias_ref,
                 *, topk, heads, idx_dim, kv_rank, v_dim, idx_steps, max_chunks):
    qb = pl.program_id(1)
    n_kc = (qb * Q_BLOCK) // KEY_CHUNK + 1
    kf = float(topk)
    pairs = heads // 2
    sub = KEY_CHUNK // LANES

    qi = qidx_ref[...]
    qst = jnp.concatenate([qi[:, j * LANES:(j + 1) * LANES] for j in range(pairs)], axis=0)
    wit = wi_ref[...].T
    w_rows = [wit[idx_dim + h:idx_dim + h + 1, :] for h in range(heads)]
    tq = qb * Q_BLOCK + lax.broadcasted_iota(jnp.int32, (1, Q_BLOCK), 1)
    n_causal = (tq + 1).astype(F32)
    small = n_causal <= kf
    qa = qabs_ref[...]
    qa_st = jnp.concatenate([qa[:, h * kv_rank:(h + 1) * kv_rank] for h in range(heads)], axis=0)
    eye = (lax.broadcasted_iota(jnp.int32, (Q_BLOCK, Q_BLOCK), 0)
           == lax.broadcasted_iota(jnp.int32, (Q_BLOCK, Q_BLOCK), 1)).astype(BF16)

    def fold8(x, op):
        return op(x.reshape(x.shape[0] // 8, 8, Q_BLOCK), axis=0)

    def variant(nc):
        chunks = [(c, c * KEY_CHUNK) for c in range(nc)]

        mn = mx = cgt = cge = None
        for c, k0 in chunks:
            sa = _dot_nt(ka_ref[k0:k0 + KEY_CHUNK, :], qst)
            sb = _dot_nt(kb_ref[k0:k0 + KEY_CHUNK, :], qst)
            acc = None
            for j in range(pairs):
                ta = jnp.maximum(sa[:, j * Q_BLOCK:(j + 1) * Q_BLOCK], 0.0) * w_rows[2 * j]
                tb = jnp.maximum(sb[:, j * Q_BLOCK:(j + 1) * Q_BLOCK], 0.0) * w_rows[2 * j + 1]
                acc = ta + tb if acc is None else acc + ta + tb
            if c == nc - 1:
                kpos = k0 + lax.broadcasted_iota(jnp.int32, (KEY_CHUNK, Q_BLOCK), 0)
                causal = kpos <= tq
                s = jnp.where(causal, acc, -jnp.inf)
                s_min = jnp.where(causal, acc, jnp.inf)
            else:
                s = s_min = acc
            st_ref[k0:k0 + KEY_CHUNK, :] = s
            parts = (fold8(s_min, jnp.min), fold8(s, jnp.max),
                     fold8(jnp.where(s > 0.0, 1.0, 0.0), jnp.sum),
                     fold8(jnp.where(s >= 0.0, 1.0, 0.0), jnp.sum))
            if mn is None:
                mn, mx, cgt, cge = parts
            else:
                mn, mx = jnp.minimum(mn, parts[0]), jnp.maximum(mx, parts[1])
                cgt, cge = cgt + parts[2], cge + parts[3]
        mn = jnp.min(mn, axis=0, keepdims=True)
        mx = jnp.max(mx, axis=0, keepdims=True)
        cgt0 = jnp.sum(cgt, axis=0, keepdims=True)
        cge0 = jnp.sum(cge, axis=0, keepdims=True)

        def count_gt(ref, thr):
            acc = None
            for _, k0 in chunks:
                part = fold8(jnp.where(ref[k0:k0 + KEY_CHUNK, :] > thr, 1.0, 0.0), jnp.sum)
                acc = part if acc is None else acc + part
            return jnp.sum(acc, axis=0, keepdims=True)

        pos = cgt0 >= kf
        zero = jnp.logical_and(jnp.logical_not(pos), cge0 >= kf)
        zero = jnp.logical_and(zero, jnp.logical_not(small))
        lo = jnp.where(pos, 0.0, jnp.where(zero, 0.0, mn * 1.0001))
        hi = jnp.where(pos, mx, 0.0)
        c_lo = jnp.where(pos, cgt0, jnp.where(zero, cgt0, n_causal))
        c_hi = jnp.where(pos, 0.0, cgt0)
        lo = jnp.where(small, -jnp.inf, lo)
        hi = jnp.where(small, -jnp.inf, hi)
        searching = jnp.logical_not(jnp.logical_or(small, zero))

        def unresolved(c_lo):
            bad = jnp.logical_and(c_lo != kf, searching)
            return (jnp.max(jnp.where(bad, 1.0, 0.0)) > 0.0).astype(jnp.int32)

        def bis_cond(carry):
            return jnp.logical_and(carry[0] < VALUE_BISECT_MAX, carry[1] > 0)

        def bis_body(carry):
            it, _, lo, hi, c_lo, c_hi = carry
            flag = unresolved(c_lo)
            for _ in range(_passes_per_body(nc)):
                mid = 0.5 * (lo + hi)
                c = count_gt(st_ref, mid)
                ge = c >= kf
                lo, hi = jnp.where(ge, mid, lo), jnp.where(ge, hi, mid)
                c_lo, c_hi = jnp.where(ge, c, c_lo), jnp.where(ge, c_hi, c)
            return it + _passes_per_body(nc), flag, lo, hi, c_lo, c_hi

        _, _, lo, hi, c_lo, c_hi = lax.while_loop(
            bis_cond, bis_body, (jnp.int32(0), jnp.int32(1), lo, hi, c_lo, c_hi))
        done = jnp.logical_or(c_lo == kf, small)
        thr = jnp.where(done, lo, hi)

        @pl.when(jnp.max(jnp.where(done, 0.0, 1.0)) > 0.0)
        def _():
            need = jnp.where(done, 0.0, kf - c_hi)
            tied = jnp.logical_not(jnp.logical_or(done, zero))
            for _, k0 in chunks:
                s = st_ref[k0:k0 + KEY_CHUNK, :]
                band = jnp.logical_or(jnp.logical_and(zero, s == 0.0),
                                      jnp.logical_and(tied, jnp.logical_and(s > lo, s <= hi)))
                kpos = (k0 + lax.broadcasted_iota(jnp.int32, (KEY_CHUNK, Q_BLOCK), 0)).astype(F32)
                bk_ref[k0:k0 + KEY_CHUNK, :] = jnp.where(band, -kpos, -jnp.inf)

            def idx_body(_, carry):
                jlo, jhi = carry
                jm = jnp.floor(0.5 * (jlo + jhi))
                ok = count_gt(bk_ref, -jm - 0.5) >= need
                return jnp.where(ok, jlo, jm), jnp.where(ok, jm, jhi)
            jlo0 = jnp.full((1, Q_BLOCK), -1.0, F32)
            jhi0 = jnp.full((1, Q_BLOCK), float(nc * KEY_CHUNK - 1), F32)
            _, jhi = lax.fori_loop(0, idx_steps, idx_body, (jlo0, jhi0))
            jsel = jnp.where(done, -1.0, jhi)
            for _, k0 in chunks:
                take = bk_ref[k0:k0 + KEY_CHUNK, :] >= -jsel
                st_ref[k0:k0 + KEY_CHUNK, :] = jnp.where(take, jnp.inf, st_ref[k0:k0 + KEY_CHUNK, :])

        for c, k0 in chunks:
            sel_t = jnp.where(st_ref[k0:k0 + KEY_CHUNK, :] > thr, 1.0, 0.0).astype(BF16)
            sel = _dot_nt(eye, sel_t)
            bias_ref[:, k0:k0 + KEY_CHUNK] = (sel - 1.0) * (-MASK_BIAS)

        nk = nc * KEY_CHUNK
        gsz = heads // HEAD_GROUPS
        grows = [slice(g * gsz * Q_BLOCK, (g + 1) * gsz * Q_BLOCK) for g in range(HEAD_GROUPS)]
        for g in range(HEAD_GROUPS):
            for c, k0 in chunks:
                s = _dot_nt(qa_st[grows[g], :], ckv_ref[k0:k0 + KEY_CHUNK, :])
                for i in range(gsz):
                    rows = slice((g * gsz + i) * Q_BLOCK, (g * gsz + i + 1) * Q_BLOCK)
                    sh = s[i * Q_BLOCK:(i + 1) * Q_BLOCK, :] + bias_ref[:, k0:k0 + KEY_CHUNK]
                    s_ref[rows, k0:k0 + KEY_CHUNK] = sh
                    mh = sh[:, :LANES]
                    for t in range(1, sub):
                        mh = jnp.maximum(mh, sh[:, t * LANES:(t + 1) * LANES])
                    m_ref[rows, :] = mh if c == 0 else jnp.maximum(m_ref[rows, :], mh)
        for g in range(HEAD_GROUPS):
            for i in range(gsz):
                rows = slice((g * gsz + i) * Q_BLOCK, (g * gsz + i + 1) * Q_BLOCK)
                m = jnp.broadcast_to(jnp.max(m_ref[rows, :], axis=-1, keepdims=True), (Q_BLOCK, LANES))
                lsum = None
                for t in range(nc * sub):
                    pt = jnp.exp2(s_ref[rows, t * LANES:(t + 1) * LANES] - m)
                    p_ref[rows, t * LANES:(t + 1) * LANES] = pt.astype(BF16)
                    lsum = pt if lsum is None else lsum + pt
                l_ref[rows, :] = lsum
            acc_ref[grows[g], :] = _dot(p_ref[grows[g], :nk], ckv_ref[:nk, :])

    for nc in range(1, max_chunks + 1):
        pl.when(n_kc == nc)(functools.partial(variant, nc))

    inv_l = 1.0 / jnp.sum(l_ref[...], axis=-1, keepdims=True)
    for h in range(heads):
        oh = (acc_ref[h * Q_BLOCK:(h + 1) * Q_BLOCK, :] * inv_l[h * Q_BLOCK:(h + 1) * Q_BLOCK]).astype(BF16)
        ov = _dot(oh, wuv_ref[h])
        og_ref[:, h * v_dim:(h + 1) * v_dim] = (
            ov * sg_ref[:, h * v_dim:(h + 1) * v_dim].astype(F32)).astype(BF16)


def _ple(h, p, gple, wpg, wple):
    gate = jax.nn.sigmoid(_dot(_rms(h, gple).astype(BF16), wpg))
    return h + _dot(p.astype(BF16), wple) * gate


def _rotate(x, cos, sin_next, sin_prev):
    nxt = pltpu.roll(x, LANES - 1, 1)
    prv = pltpu.roll(x, 1, 1)
    return x * cos + nxt * sin_next + prv * sin_prev


def _mid_kernel(og_ref, x_ref, p_ref, wout_ref, gple_ref, wpg_ref, wple_ref, gpre_ref,
                wq_ref, wk_ref, wv_ref, wg_ref, cos_ref, sn_ref, sp_ref,
                h_ref, q_ref, k_ref, v_ref, sg_ref, *, k_scale, qk_dim):
    h = x_ref[...] + _dot(og_ref[...], wout_ref[...])
    h = _ple(h, p_ref[...], gple_ref[...], wpg_ref[...], wple_ref[...])
    h_ref[...] = h
    hb = _rms(h, gpre_ref[...]).astype(BF16)
    q = _dot(hb, wq_ref[...])
    k = _dot(hb, wk_ref[...]) * k_scale
    for j in range(q.shape[1] // LANES):
        t = (j * LANES) % qk_dim
        cs = cos_ref[:, t:t + LANES]
        sn = sn_ref[:, t:t + LANES]
        sp = sp_ref[:, t:t + LANES]
        q_ref[:, j * LANES:(j + 1) * LANES] = _rotate(q[:, j * LANES:(j + 1) * LANES], cs, sn, sp).astype(BF16)
        k_ref[:, j * LANES:(j + 1) * LANES] = _rotate(k[:, j * LANES:(j + 1) * LANES], cs, sn, sp).astype(BF16)
    v_ref[...] = _dot(hb, wv_ref[...]).astype(BF16)
    g = _dot(hb, wg_ref[...])
    sg_ref[...] = (g * jax.nn.sigmoid(g)).astype(BF16)


def _ret_kernel(q_ref, k_ref, v_ref, sg_ref, dmask_ref, xi_ref, zeta_ref, decay_ref, og_ref,
                state_ref, *, heads, qk_dim, v_dim):
    @pl.when(pl.program_id(1) == 0)
    def _():
        state_ref[...] = jnp.zeros(state_ref.shape, F32)

    for c in range(q_ref.shape[0] // R_CHUNK):
        r = slice(c * R_CHUNK, (c + 1) * R_CHUNK)
        for h in range(heads):
            qh = q_ref[r, h * qk_dim:(h + 1) * qk_dim]
            kh = k_ref[r, h * qk_dim:(h + 1) * qk_dim]
            vh = v_ref[r, h * v_dim:(h + 1) * v_dim]
            s = _dot_nt(qh, kh) * dmask_ref[h]
            o = _dot(s.astype(BF16), vh) + _dot(qh, state_ref[h].astype(BF16)) * xi_ref[h]
            kz = (kh.astype(F32) * zeta_ref[h]).astype(BF16)
            state_ref[h] = state_ref[h] * decay_ref[h] + _dot_tn(kz, vh)
            mu = jnp.mean(o, axis=-1, keepdims=True)
            d = o - mu
            var = jnp.mean(d * d, axis=-1, keepdims=True)
            on = d * lax.rsqrt(var + NORM_EPS)
            og_ref[r, h * v_dim:(h + 1) * v_dim] = (
                on * sg_ref[r, h * v_dim:(h + 1) * v_dim].astype(F32)).astype(BF16)


def _out_kernel(og_ref, h_ref, p_ref, wout_ref, gple_ref, wpg_ref, wple_ref, gfin_ref, o_ref):
    h = h_ref[...] + _dot(og_ref[...], wout_ref[...])
    h = _ple(h, p_ref[...], gple_ref[...], wpg_ref[...], wple_ref[...])
    o_ref[...] = _rms(h, gfin_ref[...])


def _params(*sem):
    return pltpu.CompilerParams(dimension_semantics=sem, vmem_limit_bytes=VMEM_LIMIT)


def kernel(x, p, g_pre, a_w_in, a_g_q, a_g_kv, a_w_q_up, a_w_idx_q, a_g_ik, a_b_ik, a_w_uk, a_w_uv,
           a_w_out, r_w_in, r_w_out, w_ple_gate, g_ple, w_ple, g_final):
    B, L, D = x.shape
    N = B * L
    q_rank = a_g_q.shape[-1]
    kv_rank = a_g_kv.shape[-1]
    idx_dim = a_g_ik.shape[-1]
    heads, _, qk_dim = a_w_uk.shape[1:]
    v_dim = a_w_uv.shape[-1]
    idx_heads = a_w_idx_q.shape[-1] // idx_dim
    a_width = heads * v_dim
    ple_dim = p.shape[-1]
    assert idx_heads == heads and 2 * idx_dim == LANES and L % KEY_CHUNK == 0
    assert KEY_CHUNK % Q_BLOCK == 0 and L % Q_BLOCK == 0 and heads % HEAD_GROUPS == 0
    assert q_rank % LANES == 0 and kv_rank % LANES == 0 and a_w_in.shape[0] == 1 and r_w_in.shape[0] == 1
    topk = min(TOPK_MAX, L // 4)
    x2 = x.reshape(N, D)
    row = lambda a: a.reshape(1, -1).astype(F32)

    w_in = a_w_in[0]
    o_ki = q_rank + kv_rank
    o_g = o_ki + idx_dim + idx_heads
    w_ki = jnp.pad(w_in[:, o_ki:o_g], ((0, 0), (0, LANES - (o_g - o_ki))))
    w0 = jnp.concatenate([w_in[:, :o_ki], w_ki, w_in[:, o_g:]], axis=1).astype(BF16)
    pad_k = lambda a: jnp.pad(a.reshape(1, -1).astype(F32), ((0, 0), (0, LANES - idx_dim)))
    wukt = jnp.swapaxes(a_w_uk[0], 1, 2).astype(BF16)
    tm0 = 512
    n0 = w0.shape[1]
    outs0 = pl.pallas_call(
        functools.partial(_proj0_kernel, q_rank=q_rank, kv_rank=kv_rank, idx_dim=idx_dim, heads=heads,
                          qk_dim=qk_dim, qk_scale=qk_dim ** -0.5 * math.log2(math.e),
                          widx_scale=idx_heads ** -0.5 * idx_dim ** -0.5),
        grid=(N // tm0,),
        in_specs=[pl.BlockSpec((tm0, D), lambda i: (i, 0)), _const_spec((1, D)), _const_spec((D, n0)),
                  _const_spec((1, q_rank)), _const_spec((1, kv_rank)), _const_spec((1, LANES)),
                  _const_spec((1, LANES)), _const_spec((q_rank, heads * qk_dim)),
                  _const_spec((heads, qk_dim, kv_rank)), _const_spec((q_rank, idx_heads * idx_dim))],
        out_specs=[pl.BlockSpec((tm0, heads * kv_rank), lambda i: (i, 0)),
                   pl.BlockSpec((tm0, idx_heads * idx_dim), lambda i: (i, 0)),
                   pl.BlockSpec((tm0, LANES), lambda i: (i, 0)),
                   pl.BlockSpec((tm0, LANES), lambda i: (i, 0)),
                   pl.BlockSpec((tm0, LANES), lambda i: (i, 0)),
                   pl.BlockSpec((tm0, kv_rank), lambda i: (i, 0)),
                   pl.BlockSpec((tm0, a_width), lambda i: (i, 0))],
        out_shape=[jax.ShapeDtypeStruct((N, heads * kv_rank), BF16),
                   jax.ShapeDtypeStruct((N, idx_heads * idx_dim), BF16),
                   jax.ShapeDtypeStruct((N, LANES), BF16),
                   jax.ShapeDtypeStruct((N, LANES), BF16),
                   jax.ShapeDtypeStruct((N, LANES), F32),
                   jax.ShapeDtypeStruct((N, kv_rank), BF16),
                   jax.ShapeDtypeStruct((N, a_width), BF16)],
        compiler_params=_params("parallel"),
        name="dsa_proj",
    )(x2, row(g_pre[0]), w0, row(a_g_q[0]), row(a_g_kv[0]), pad_k(a_g_ik[0]), pad_k(a_b_ik[0]),
      a_w_q_up[0].astype(BF16), wukt, a_w_idx_q[0].astype(BF16))
    qabs, qidx, ka, kb, wi, ckv, sg0 = outs0

    nb = L // Q_BLOCK
    nkc = L // KEY_CHUNK
    b3 = lambda a: a.reshape(B, L, a.shape[-1])
    qblk = lambda w: pl.BlockSpec((None, Q_BLOCK, w), lambda b, i: (b, i, 0))
    seq = lambda w: pl.BlockSpec((None, L, w), lambda b, i: (b, 0, 0))
    og0 = pl.pallas_call(
        functools.partial(_attn_kernel, topk=topk, heads=heads, idx_dim=idx_dim, kv_rank=kv_rank,
                          v_dim=v_dim, idx_steps=max(1, math.ceil(math.log2(L))), max_chunks=nkc),
        grid=(B, nb),
        in_specs=[qblk(idx_heads * idx_dim), seq(LANES), seq(LANES), qblk(LANES), qblk(heads * kv_rank),
                  seq(kv_rank), qblk(a_width), _const_spec((heads, kv_rank, v_dim))],
        out_specs=qblk(a_width),
        out_shape=jax.ShapeDtypeStruct((B, L, a_width), BF16),
        scratch_shapes=[pltpu.VMEM((L, Q_BLOCK), F32), pltpu.VMEM((L, Q_BLOCK), F32),
                        pltpu.VMEM((heads * Q_BLOCK, L), F32),
                        pltpu.VMEM((heads * Q_BLOCK, L), BF16),
                        pltpu.VMEM((heads * Q_BLOCK, LANES), F32),
                        pltpu.VMEM((heads * Q_BLOCK, kv_rank), F32),
                        pltpu.VMEM((heads * Q_BLOCK, LANES), F32),
                        pltpu.VMEM((Q_BLOCK, L), F32)],
        compiler_params=_params("parallel", "arbitrary"),
        name="dsa_attn",
    )(b3(qidx), b3(ka), b3(kb), b3(wi), b3(qabs), b3(ckv), b3(sg0), a_w_uv[0].astype(BF16))

    r_heads = 4
    r_qk = D // r_heads
    r_v = 2 * D // r_heads
    r_width = r_heads * r_v
    assert r_w_in.shape[-1] == 2 * D + 2 * r_width and r_qk % LANES == 0 and L % R_CHUNK == 0
    wr = r_w_in[0].astype(BF16)
    wq, wk, wv, wg = wr[:, :D], wr[:, D:2 * D], wr[:, 2 * D:2 * D + r_width], wr[:, 2 * D + r_width:]
    pos = jnp.arange(L, dtype=F32)
    angle = 1.0 / (10000.0 ** jnp.linspace(0.0, 1.0, r_qk // 2, dtype=F32))
    theta = pos[:, None] * jnp.repeat(angle, 2)[None, :]
    even = (jnp.arange(r_qk) % 2 == 0)[None, :]
    cos_t = jnp.cos(theta)
    sin_next = jnp.where(even, -jnp.sin(theta), 0.0)
    sin_prev = jnp.where(even, 0.0, jnp.sin(theta))
    tm1 = 256
    tok = lambda w: pl.BlockSpec((tm1, w), lambda b, i: (b * (L // tm1) + i, 0))
    tab = pl.BlockSpec((tm1, r_qk), lambda b, i: (i, 0))
    h1, rq, rk, rv, sg1 = pl.pallas_call(
        functools.partial(_mid_kernel, k_scale=r_qk ** -0.5, qk_dim=r_qk),
        grid=(B, L // tm1),
        in_specs=[tok(a_width), tok(D), tok(ple_dim), _const_spec((a_width, D)), _const_spec((1, D)),
                  _const_spec((D, D)), _const_spec((ple_dim, D)), _const_spec((1, D)),
                  _const_spec((D, D)), _const_spec((D, D)), _const_spec((D, r_width)),
                  _const_spec((D, r_width)), tab, tab, tab],
        out_specs=[tok(D), tok(D), tok(D), tok(r_width), tok(r_width)],
        out_shape=[jax.ShapeDtypeStruct((N, D), F32), jax.ShapeDtypeStruct((N, D), BF16),
                   jax.ShapeDtypeStruct((N, D), BF16), jax.ShapeDtypeStruct((N, r_width), BF16),
                   jax.ShapeDtypeStruct((N, r_width), BF16)],
        compiler_params=_params("parallel", "arbitrary"),
        name="mid_proj",
    )(og0.reshape(N, a_width), x2, p[0].reshape(N, ple_dim), a_w_out[0].astype(BF16), row(g_ple[0]),
      w_ple_gate[0].astype(BF16), w_ple[0].astype(BF16), row(g_pre[1]), wq, wk, wv, wg,
      cos_t, sin_next, sin_prev)

    log_gamma = jnp.log(1.0 - 2.0 ** (-5.0 - jnp.arange(r_heads, dtype=F32)))
    ci = jnp.arange(R_CHUNK, dtype=F32)
    diff = ci[:, None] - ci[None, :]
    dmask = jnp.where(diff[None] >= 0, jnp.exp(diff[None] * log_gamma[:, None, None]), 0.0)
    xi = jnp.exp((ci[None, :] + 1.0) * log_gamma[:, None])
    zeta = jnp.exp((R_CHUNK - 1.0 - ci[None, :]) * log_gamma[:, None])
    decay = jnp.exp(R_CHUNK * log_gamma)
    xi_b = jnp.broadcast_to(xi[:, :, None], (r_heads, R_CHUNK, r_v))
    zeta_b = jnp.broadcast_to(zeta[:, :, None], (r_heads, R_CHUNK, r_qk))
    decay_b = jnp.broadcast_to(decay[:, None, None], (r_heads, 1, r_v))
    rows_step = R_CHUNK * math.gcd(R_STEP_CHUNKS, L // R_CHUNK)
    cblk = lambda w: pl.BlockSpec((None, rows_step, w), lambda b, c: (b, c, 0))
    og1 = pl.pallas_call(
        functools.partial(_ret_kernel, heads=r_heads, qk_dim=r_qk, v_dim=r_v),
        grid=(B, L // rows_step),
        in_specs=[cblk(D), cblk(D), cblk(r_width), cblk(r_width),
                  _const_spec((r_heads, R_CHUNK, R_CHUNK)), _const_spec((r_heads, R_CHUNK, r_v)),
                  _const_spec((r_heads, R_CHUNK, r_qk)), _const_spec((r_heads, 1, r_v))],
        out_specs=cblk(r_width),
        out_shape=jax.ShapeDtypeStruct((B, L, r_width), BF16),
        scratch_shapes=[pltpu.VMEM((r_heads, r_qk, r_v), F32)],
        compiler_params=_params("parallel", "arbitrary"),
        name="retention",
    )(rq.reshape(B, L, D), rk.reshape(B, L, D), rv.reshape(B, L, r_width), sg1.reshape(B, L, r_width),
      dmask, xi_b, zeta_b, decay_b)

    tm2 = 512
    tk2 = lambda w: pl.BlockSpec((tm2, w), lambda i: (i, 0))
    out = pl.pallas_call(
        _out_kernel,
        grid=(N // tm2,),
        in_specs=[tk2(r_width), tk2(D), tk2(ple_dim), _const_spec((r_width, D)), _const_spec((1, D)),
                  _const_spec((D, D)), _const_spec((ple_dim, D)), _const_spec((1, D))],
        out_specs=tk2(D),
        out_shape=jax.ShapeDtypeStruct((N, D), F32),
        compiler_params=_params("parallel"),
        name="out_proj",
    )(og1.reshape(N, r_width), h1, p[1].reshape(N, ple_dim), r_w_out[0].astype(BF16), row(g_ple[1]),
      w_ple_gate[1].astype(BF16), w_ple[1].astype(BF16), row(g_final))
    return out.reshape(B, L, D)
```

```python
import functools
import math

import jax
import jax.numpy as jnp
from jax import lax
from jax.experimental import pallas as pl
from jax.experimental.pallas import tpu as pltpu

F32 = jnp.float32
BF16 = jnp.bfloat16

NORM_EPS = 1e-6
TOPK_MAX = 256
Q_BLOCK = 256
KEY_CHUNK = 256
R_CHUNK = 128
LANES = 128
MASK_BIAS = -1e30
VALUE_BISECT_MAX = 34
VMEM_LIMIT = 56 * 1024 * 1024
HEAD_GROUPS = 4
R_STEP_CHUNKS = 4


def _variant_chunk_counts(max_chunks):
    counts = [n for n in range(1, max_chunks + 1) if n <= 4 or n % 2 == 0]
    if counts[-1] != max_chunks:
        counts.append(max_chunks)
    return tuple(counts)


def _passes_per_body(n_chunks):
    return max(1, min(4, round(4 / n_chunks)))


def _rms(x, g):
    return x * lax.rsqrt(jnp.mean(x * x, axis=-1, keepdims=True) + NORM_EPS) * g


def _dot(a, b):
    return jnp.dot(a, b, preferred_element_type=F32)


def _dot_nt(a, b):
    return lax.dot_general(a, b, (((1,), (1,)), ((), ())), preferred_element_type=F32)


def _dot_tn(a, b):
    return lax.dot_general(a, b, (((0,), (0,)), ((), ())), preferred_element_type=F32)


def _const_spec(shape):
    nd = len(shape)
    return pl.BlockSpec(shape, lambda *_: (0,) * nd)


def _proj0_kernel(x_ref, gpre_ref, w0_ref, gq_ref, gkv_ref, gik_ref, bik_ref, wqup_ref, wukt_ref,
                  widxq_ref, qabs_ref, qidx_ref, ka_ref, kb_ref, wi_ref, ckv_ref, sg_ref,
                  *, q_rank, kv_rank, idx_dim, heads, qk_dim, qk_scale, widx_scale):
    hn = _rms(x_ref[...], gpre_ref[...])
    z = _dot(hn.astype(BF16), w0_ref[...])
    o_kv = q_rank
    o_ki = q_rank + kv_rank
    o_g = o_ki + LANES
    cq = _rms(z[:, :o_kv], gq_ref[...]).astype(BF16)
    ckv_ref[...] = _rms(z[:, o_kv:o_ki], gkv_ref[...]).astype(BF16)
    ki = z[:, o_ki:o_g]
    lane = lax.broadcasted_iota(jnp.int32, (1, LANES), 1)
    in_k = lane < idx_dim
    mu = jnp.sum(jnp.where(in_k, ki, 0.0), axis=-1, keepdims=True) * (1.0 / idx_dim)
    d = jnp.where(in_k, ki - mu, 0.0)
    var = jnp.sum(d * d, axis=-1, keepdims=True) * (1.0 / idx_dim)
    kn = d * lax.rsqrt(var + NORM_EPS) * gik_ref[...] + bik_ref[...]
    ka_ref[...] = kn.astype(BF16)
    kb_ref[...] = pltpu.roll(kn, idx_dim, 1).astype(BF16)
    wi_ref[...] = ki * widx_scale
    g = z[:, o_g:]
    sg_ref[...] = (g * jax.nn.sigmoid(g)).astype(BF16)
    q = _dot(cq, wqup_ref[...])
    kvr = wukt_ref.shape[2]
    for h in range(heads):
        qh = q[:, h * qk_dim:(h + 1) * qk_dim].astype(BF16)
        qabs_ref[:, h * kvr:(h + 1) * kvr] = (_dot(qh, wukt_ref[h]) * qk_scale).astype(BF16)
    qidx_ref[...] = _dot(cq, widxq_ref[...]).astype(BF16)


def _attn_kernel(qidx_ref, ka_ref, kb_ref, wi_ref, qabs_ref, ckv_ref, sg_ref, wuv_ref, og_ref,
                 st_ref, bk_ref, s_ref, p_ref, m_ref, acc_ref, l_ref, bias_ref,
                 *, topk, heads, idx_dim, kv_rank, v_dim, idx_steps, max_chunks):
    qb = pl.program_id(1)
    n_kc = (qb * Q_BLOCK) // KEY_CHUNK + 1
    kf = float(topk)
    pairs = heads // 2
    sub = KEY_CHUNK // LANES

    qi = qidx_ref[...]
    qst = jnp.concatenate([qi[:, j * LANES:(j + 1) * LANES] for j in range(pairs)], axis=0)
    wit = wi_ref[...].T
    w_rows = [wit[idx_dim + h:idx_dim + h + 1, :] for h in range(heads)]
    tq = qb * Q_BLOCK + lax.broadcasted_iota(jnp.int32, (1, Q_BLOCK), 1)
    n_causal = (tq + 1).astype(F32)
    small = n_causal <= kf
    qa = qabs_ref[...]
    qa_st = jnp.concatenate([qa[:, h * kv_rank:(h + 1) * kv_rank] for h in range(heads)], axis=0)
    eye = (lax.broadcasted_iota(jnp.int32, (Q_BLOCK, Q_BLOCK), 0)
           == lax.broadcasted_iota(jnp.int32, (Q_BLOCK, Q_BLOCK), 1)).astype(BF16)

    def fold8(x, op):
        return op(x.reshape(x.shape[0] // 8, 8, Q_BLOCK), axis=0)

    def variant(nc, nc_min):
        chunks = [(c, c * KEY_CHUNK) for c in range(nc)]

        mn = mx = cgt = cge = None
        for c, k0 in chunks:
            sa = _dot_nt(ka_ref[k0:k0 + KEY_CHUNK, :], qst)
            sb = _dot_nt(kb_ref[k0:k0 + KEY_CHUNK, :], qst)
            acc = None
            for j in range(pairs):
                ta = jnp.maximum(sa[:, j * Q_BLOCK:(j + 1) * Q_BLOCK], 0.0) * w_rows[2 * j]
                tb = jnp.maximum(sb[:, j * Q_BLOCK:(j + 1) * Q_BLOCK], 0.0) * w_rows[2 * j + 1]
                acc = ta + tb if acc is None else acc + ta + tb
            if c >= nc_min - 1:
                kpos = k0 + lax.broadcasted_iota(jnp.int32, (KEY_CHUNK, Q_BLOCK), 0)
                causal = kpos <= tq
                s = jnp.where(causal, acc, -jnp.inf)
                s_min = jnp.where(causal, acc, jnp.inf)
            else:
                s = s_min = acc
            st_ref[k0:k0 + KEY_CHUNK, :] = s
            parts = (fold8(s_min, jnp.min), fold8(s, jnp.max),
                     fold8(jnp.where(s > 0.0, 1.0, 0.0), jnp.sum),
                     fold8(jnp.where(s >= 0.0, 1.0, 0.0), jnp.sum))
            if mn is None:
                mn, mx, cgt, cge = parts
            else:
                mn, mx = jnp.minimum(mn, parts[0]), jnp.maximum(mx, parts[1])
                cgt, cge = cgt + parts[2], cge + parts[3]
        mn = jnp.min(mn, axis=0, keepdims=True)
        mx = jnp.max(mx, axis=0, keepdims=True)
        cgt0 = jnp.sum(cgt, axis=0, keepdims=True)
        cge0 = jnp.sum(cge, axis=0, keepdims=True)

        def count_gt(ref, thr):
            acc = None
            for _, k0 in chunks:
                part = fold8(jnp.where(ref[k0:k0 + KEY_CHUNK, :] > thr, 1.0, 0.0), jnp.sum)
                acc = part if acc is None else acc + part
            return jnp.sum(acc, axis=0, keepdims=True)

        pos = cgt0 >= kf
        zero = jnp.logical_and(jnp.logical_not(pos), cge0 >= kf)
        zero = jnp.logical_and(zero, jnp.logical_not(small))
        lo = jnp.where(pos, 0.0, jnp.where(zero, 0.0, mn * 1.0001))
        hi = jnp.where(pos, mx, 0.0)
        c_lo = jnp.where(pos, cgt0, jnp.where(zero, cgt0, n_causal))
        c_hi = jnp.where(pos, 0.0, cgt0)
        lo = jnp.where(small, -jnp.inf, lo)
        hi = jnp.where(small, -jnp.inf, hi)
        searching = jnp.logical_not(jnp.logical_or(small, zero))

        def unresolved(c_lo):
            bad = jnp.logical_and(c_lo != kf, searching)
            return (jnp.max(jnp.where(bad, 1.0, 0.0)) > 0.0).astype(jnp.int32)

        def bis_cond(carry):
            return jnp.logical_and(carry[0] < VALUE_BISECT_MAX, carry[1] > 0)

        def bis_body(carry):
            it, _, lo, hi, c_lo, c_hi = carry
            flag = unresolved(c_lo)
            for _ in range(_passes_per_body(nc)):
                mid = 0.5 * (lo + hi)
                c = count_gt(st_ref, mid)
                ge = c >= kf
                lo, hi = jnp.where(ge, mid, lo), jnp.where(ge, hi, mid)
                c_lo, c_hi = jnp.where(ge, c, c_lo), jnp.where(ge, c_hi, c)
            return it + _passes_per_body(nc), flag, lo, hi, c_lo, c_hi

        _, _, lo, hi, c_lo, c_hi = lax.while_loop(
            bis_cond, bis_body, (jnp.int32(0), jnp.int32(1), lo, hi, c_lo, c_hi))
        done = jnp.logical_or(c_lo == kf, small)
        thr = jnp.where(done, lo, hi)

        @pl.when(jnp.max(jnp.where(done, 0.0, 1.0)) > 0.0)
        def _():
            need = jnp.where(done, 0.0, kf - c_hi)
            tied = jnp.logical_not(jnp.logical_or(done, zero))
            for _, k0 in chunks:
                s = st_ref[k0:k0 + KEY_CHUNK, :]
                band = jnp.logical_or(jnp.logical_and(zero, s == 0.0),
                                      jnp.logical_and(tied, jnp.logical_and(s > lo, s <= hi)))
                kpos = (k0 + lax.broadcasted_iota(jnp.int32, (KEY_CHUNK, Q_BLOCK), 0)).astype(F32)
                bk_ref[k0:k0 + KEY_CHUNK, :] = jnp.where(band, -kpos, -jnp.inf)

            def idx_body(_, carry):
                jlo, jhi = carry
                jm = jnp.floor(0.5 * (jlo + jhi))
                ok = count_gt(bk_ref, -jm - 0.5) >= need
                return jnp.where(ok, jlo, jm), jnp.where(ok, jm, jhi)
            jlo0 = jnp.full((1, Q_BLOCK), -1.0, F32)
            jhi0 = jnp.full((1, Q_BLOCK), float(nc * KEY_CHUNK - 1), F32)
            _, jhi = lax.fori_loop(0, idx_steps, idx_body, (jlo0, jhi0))
            jsel = jnp.where(done, -1.0, jhi)
            for _, k0 in chunks:
                take = bk_ref[k0:k0 + KEY_CHUNK, :] >= -jsel
                st_ref[k0:k0 + KEY_CHUNK, :] = jnp.where(take, jnp.inf, st_ref[k0:k0 + KEY_CHUNK, :])

        for c, k0 in chunks:
            sel_t = jnp.where(st_ref[k0:k0 + KEY_CHUNK, :] > thr, 1.0, 0.0).astype(BF16)
            sel = _dot_nt(eye, sel_t)
            bias_ref[:, k0:k0 + KEY_CHUNK] = (sel - 1.0) * (-MASK_BIAS)

        nk = nc * KEY_CHUNK
        gsz = heads // HEAD_GROUPS
        grows = [slice(g * gsz * Q_BLOCK, (g + 1) * gsz * Q_BLOCK) for g in range(HEAD_GROUPS)]
        for g in range(HEAD_GROUPS):
            for c, k0 in chunks:
                s = _dot_nt(qa_st[grows[g], :], ckv_ref[k0:k0 + KEY_CHUNK, :])
                for i in range(gsz):
                    rows = slice((g * gsz + i) * Q_BLOCK, (g * gsz + i + 1) * Q_BLOCK)
                    sh = s[i * Q_BLOCK:(i + 1) * Q_BLOCK, :] + bias_ref[:, k0:k0 + KEY_CHUNK]
                    s_ref[rows, k0:k0 + KEY_CHUNK] = sh
                    mh = sh[:, :LANES]
                    for t in range(1, sub):
                        mh = jnp.maximum(mh, sh[:, t * LANES:(t + 1) * LANES])
                    m_ref[rows, :] = mh if c == 0 else jnp.maximum(m_ref[rows, :], mh)
        for g in range(HEAD_GROUPS):
            for i in range(gsz):
                rows = slice((g * gsz + i) * Q_BLOCK, (g * gsz + i + 1) * Q_BLOCK)
                m = jnp.broadcast_to(jnp.max(m_ref[rows, :], axis=-1, keepdims=True), (Q_BLOCK, LANES))
                lsum = None
                for t in range(nc * sub):
                    pt = jnp.exp2(s_ref[rows, t * LANES:(t + 1) * LANES] - m)
                    p_ref[rows, t * LANES:(t + 1) * LANES] = pt.astype(BF16)
                    lsum = pt if lsum is None else lsum + pt
                l_ref[rows, :] = lsum
            acc_ref[grows[g], :] = _dot(p_ref[grows[g], :nk], ckv_ref[:nk, :])

    counts = _variant_chunk_counts(max_chunks)
    for prev, nc in zip((0,) + counts[:-1], counts):
        pl.when(jnp.logical_and(n_kc > prev, n_kc <= nc))(functools.partial(variant, nc, prev + 1))

    inv_l = 1.0 / jnp.sum(l_ref[...], axis=-1, keepdims=True)
    for h in range(heads):
        oh = (acc_ref[h * Q_BLOCK:(h + 1) * Q_BLOCK, :] * inv_l[h * Q_BLOCK:(h + 1) * Q_BLOCK]).astype(BF16)
        ov = _dot(oh, wuv_ref[h])
        og_ref[:, h * v_dim:(h + 1) * v_dim] = (
            ov * sg_ref[:, h * v_dim:(h + 1) * v_dim].astype(F32)).astype(BF16)


def _ple(h, p, gple, wpg, wple):
    gate = jax.nn.sigmoid(_dot(_rms(h, gple).astype(BF16), wpg))
    return h + _dot(p.astype(BF16), wple) * gate


def _rotate(x, cos, sin_next, sin_prev):
    nxt = pltpu.roll(x, LANES - 1, 1)
    prv = pltpu.roll(x, 1, 1)
    return x * cos + nxt * sin_next + prv * sin_prev


def _mid_kernel(og_ref, x_ref, p_ref, wout_ref, gple_ref, wpg_ref, wple_ref, gpre_ref,
                wq_ref, wk_ref, wv_ref, wg_ref, cos_ref, sn_ref, sp_ref,
                h_ref, q_ref, k_ref, v_ref, sg_ref, *, k_scale, qk_dim):
    h = x_ref[...] + _dot(og_ref[...], wout_ref[...])
    h = _ple(h, p_ref[...], gple_ref[...], wpg_ref[...], wple_ref[...])
    h_ref[...] = h
    hb = _rms(h, gpre_ref[...]).astype(BF16)
    q = _dot(hb, wq_ref[...])
    k = _dot(hb, wk_ref[...]) * k_scale
    for j in range(q.shape[1] // LANES):
        t = (j * LANES) % qk_dim
        cs = cos_ref[:, t:t + LANES]
        sn = sn_ref[:, t:t + LANES]
        sp = sp_ref[:, t:t + LANES]
        q_ref[:, j * LANES:(j + 1) * LANES] = _rotate(q[:, j * LANES:(j + 1) * LANES], cs, sn, sp).astype(BF16)
        k_ref[:, j * LANES:(j + 1) * LANES] = _rotate(k[:, j * LANES:(j + 1) * LANES], cs, sn, sp).astype(BF16)
    v_ref[...] = _dot(hb, wv_ref[...]).astype(BF16)
    g = _dot(hb, wg_ref[...])
    sg_ref[...] = (g * jax.nn.sigmoid(g)).astype(BF16)


def _ret_kernel(q_ref, k_ref, v_ref, sg_ref, dmask_ref, xi_ref, zeta_ref, decay_ref, og_ref,
                state_ref, *, heads, qk_dim, v_dim):
    @pl.when(pl.program_id(1) == 0)
    def _():
        state_ref[...] = jnp.zeros(state_ref.shape, F32)

    for c in range(q_ref.shape[0] // R_CHUNK):
        r = slice(c * R_CHUNK, (c + 1) * R_CHUNK)
        for h in range(heads):
            qh = q_ref[r, h * qk_dim:(h + 1) * qk_dim]
            kh = k_ref[r, h * qk_dim:(h + 1) * qk_dim]
            vh = v_ref[r, h * v_dim:(h + 1) * v_dim]
            s = _dot_nt(qh, kh) * dmask_ref[h]
            o = _dot(s.astype(BF16), vh) + _dot(qh, state_ref[h].astype(BF16)) * xi_ref[h]
            kz = (kh.astype(F32) * zeta_ref[h]).astype(BF16)
            state_ref[h] = state_ref[h] * decay_ref[h] + _dot_tn(kz, vh)
            mu = jnp.mean(o, axis=-1, keepdims=True)
            d = o - mu
            var = jnp.mean(d * d, axis=-1, keepdims=True)
            on = d * lax.rsqrt(var + NORM_EPS)
            og_ref[r, h * v_dim:(h + 1) * v_dim] = (
                on * sg_ref[r, h * v_dim:(h + 1) * v_dim].astype(F32)).astype(BF16)


def _out_kernel(og_ref, h_ref, p_ref, wout_ref, gple_ref, wpg_ref, wple_ref, gfin_ref, o_ref):
    h = h_ref[...] + _dot(og_ref[...], wout_ref[...])
    h = _ple(h, p_ref[...], gple_ref[...], wpg_ref[...], wple_ref[...])
    o_ref[...] = _rms(h, gfin_ref[...])


def _params(*sem):
    return pltpu.CompilerParams(dimension_semantics=sem, vmem_limit_bytes=VMEM_LIMIT)


def kernel(x, p, g_pre, a_w_in, a_g_q, a_g_kv, a_w_q_up, a_w_idx_q, a_g_ik, a_b_ik, a_w_uk, a_w_uv,
           a_w_out, r_w_in, r_w_out, w_ple_gate, g_ple, w_ple, g_final):
    B, L, D = x.shape
    N = B * L
    q_rank = a_g_q.shape[-1]
    kv_rank = a_g_kv.shape[-1]
    idx_dim = a_g_ik.shape[-1]
    heads, _, qk_dim = a_w_uk.shape[1:]
    v_dim = a_w_uv.shape[-1]
    idx_heads = a_w_idx_q.shape[-1] // idx_dim
    a_width = heads * v_dim
    ple_dim = p.shape[-1]
    assert idx_heads == heads and 2 * idx_dim == LANES and L % KEY_CHUNK == 0
    assert KEY_CHUNK % Q_BLOCK == 0 and L % Q_BLOCK == 0 and heads % HEAD_GROUPS == 0
    assert q_rank % LANES == 0 and kv_rank % LANES == 0 and a_w_in.shape[0] == 1 and r_w_in.shape[0] == 1
    topk = min(TOPK_MAX, L // 4)
    x2 = x.reshape(N, D)
    row = lambda a: a.reshape(1, -1).astype(F32)

    w_in = a_w_in[0]
    o_ki = q_rank + kv_rank
    o_g = o_ki + idx_dim + idx_heads
    w_ki = jnp.pad(w_in[:, o_ki:o_g], ((0, 0), (0, LANES - (o_g - o_ki))))
    w0 = jnp.concatenate([w_in[:, :o_ki], w_ki, w_in[:, o_g:]], axis=1).astype(BF16)
    pad_k = lambda a: jnp.pad(a.reshape(1, -1).astype(F32), ((0, 0), (0, LANES - idx_dim)))
    wukt = jnp.swapaxes(a_w_uk[0], 1, 2).astype(BF16)
    tm0 = 512
    n0 = w0.shape[1]
    outs0 = pl.pallas_call(
        functools.partial(_proj0_kernel, q_rank=q_rank, kv_rank=kv_rank, idx_dim=idx_dim, heads=heads,
                          qk_dim=qk_dim, qk_scale=qk_dim ** -0.5 * math.log2(math.e),
                          widx_scale=idx_heads ** -0.5 * idx_dim ** -0.5),
        grid=(N // tm0,),
        in_specs=[pl.BlockSpec((tm0, D), lambda i: (i, 0)), _const_spec((1, D)), _const_spec((D, n0)),
                  _const_spec((1, q_rank)), _const_spec((1, kv_rank)), _const_spec((1, LANES)),
                  _const_spec((1, LANES)), _const_spec((q_rank, heads * qk_dim)),
                  _const_spec((heads, qk_dim, kv_rank)), _const_spec((q_rank, idx_heads * idx_dim))],
        out_specs=[pl.BlockSpec((tm0, heads * kv_rank), lambda i: (i, 0)),
                   pl.BlockSpec((tm0, idx_heads * idx_dim), lambda i: (i, 0)),
                   pl.BlockSpec((tm0, LANES), lambda i: (i, 0)),
                   pl.BlockSpec((tm0, LANES), lambda i: (i, 0)),
                   pl.BlockSpec((tm0, LANES), lambda i: (i, 0)),
                   pl.BlockSpec((tm0, kv_rank), lambda i: (i, 0)),
                   pl.BlockSpec((tm0, a_width), lambda i: (i, 0))],
        out_shape=[jax.ShapeDtypeStruct((N, heads * kv_rank), BF16),
                   jax.ShapeDtypeStruct((N, idx_heads * idx_dim), BF16),
                   jax.ShapeDtypeStruct((N, LANES), BF16),
                   jax.ShapeDtypeStruct((N, LANES), BF16),
                   jax.ShapeDtypeStruct((N, LANES), F32),
                   jax.ShapeDtypeStruct((N, kv_rank), BF16),
                   jax.ShapeDtypeStruct((N, a_width), BF16)],
        compiler_params=_params("parallel"),
        name="dsa_proj",
    )(x2, row(g_pre[0]), w0, row(a_g_q[0]), row(a_g_kv[0]), pad_k(a_g_ik[0]), pad_k(a_b_ik[0]),
      a_w_q_up[0].astype(BF16), wukt, a_w_idx_q[0].astype(BF16))
    qabs, qidx, ka, kb, wi, ckv, sg0 = outs0

    nb = L // Q_BLOCK
    nkc = L // KEY_CHUNK
    b3 = lambda a: a.reshape(B, L, a.shape[-1])
    qblk = lambda w: pl.BlockSpec((None, Q_BLOCK, w), lambda b, i: (b, i, 0))
    seq = lambda w: pl.BlockSpec((None, L, w), lambda b, i: (b, 0, 0))
    og0 = pl.pallas_call(
        functools.partial(_attn_kernel, topk=topk, heads=heads, idx_dim=idx_dim, kv_rank=kv_rank,
                          v_dim=v_dim, idx_steps=max(1, math.ceil(math.log2(L))), max_chunks=nkc),
        grid=(B, nb),
        in_specs=[qblk(idx_heads * idx_dim), seq(LANES), seq(LANES), qblk(LANES), qblk(heads * kv_rank),
                  seq(kv_rank), qblk(a_width), _const_spec((heads, kv_rank, v_dim))],
        out_specs=qblk(a_width),
        out_shape=jax.ShapeDtypeStruct((B, L, a_width), BF16),
        scratch_shapes=[pltpu.VMEM((L, Q_BLOCK), F32), pltpu.VMEM((L, Q_BLOCK), F32),
                        pltpu.VMEM((heads * Q_BLOCK, L), F32),
                        pltpu.VMEM((heads * Q_BLOCK, L), BF16),
                        pltpu.VMEM((heads * Q_BLOCK, LANES), F32),
                        pltpu.VMEM((heads * Q_BLOCK, kv_rank), F32),
                        pltpu.VMEM((heads * Q_BLOCK, LANES), F32),
                        pltpu.VMEM((Q_BLOCK, L), F32)],
        compiler_params=_params("parallel", "arbitrary"),
        name="dsa_attn",
    )(b3(qidx), b3(ka), b3(kb), b3(wi), b3(qabs), b3(ckv), b3(sg0), a_w_uv[0].astype(BF16))

    r_heads = 4
    r_qk = D // r_heads
    r_v = 2 * D // r_heads
    r_width = r_heads * r_v
    assert r_w_in.shape[-1] == 2 * D + 2 * r_width and r_qk % LANES == 0 and L % R_CHUNK == 0
    wr = r_w_in[0].astype(BF16)
    wq, wk, wv, wg = wr[:, :D], wr[:, D:2 * D], wr[:, 2 * D:2 * D + r_width], wr[:, 2 * D + r_width:]
    pos = jnp.arange(L, dtype=F32)
    angle = 1.0 / (10000.0 ** jnp.linspace(0.0, 1.0, r_qk // 2, dtype=F32))
    theta = pos[:, None] * jnp.repeat(angle, 2)[None, :]
    even = (jnp.arange(r_qk) % 2 == 0)[None, :]
    cos_t = jnp.cos(theta)
    sin_next = jnp.where(even, -jnp.sin(theta), 0.0)
    sin_prev = jnp.where(even, 0.0, jnp.sin(theta))
    tm1 = 256
    tok = lambda w: pl.BlockSpec((tm1, w), lambda b, i: (b * (L // tm1) + i, 0))
    tab = pl.BlockSpec((tm1, r_qk), lambda b, i: (i, 0))
    h1, rq, rk, rv, sg1 = pl.pallas_call(
        functools.partial(_mid_kernel, k_scale=r_qk ** -0.5, qk_dim=r_qk),
        grid=(B, L // tm1),
        in_specs=[tok(a_width), tok(D), tok(ple_dim), _const_spec((a_width, D)), _const_spec((1, D)),
                  _const_spec((D, D)), _const_spec((ple_dim, D)), _const_spec((1, D)),
                  _const_spec((D, D)), _const_spec((D, D)), _const_spec((D, r_width)),
                  _const_spec((D, r_width)), tab, tab, tab],
        out_specs=[tok(D), tok(D), tok(D), tok(r_width), tok(r_width)],
        out_shape=[jax.ShapeDtypeStruct((N, D), F32), jax.ShapeDtypeStruct((N, D), BF16),
                   jax.ShapeDtypeStruct((N, D), BF16), jax.ShapeDtypeStruct((N, r_width), BF16),
                   jax.ShapeDtypeStruct((N, r_width), BF16)],
        compiler_params=_params("parallel", "arbitrary"),
        name="mid_proj",
    )(og0.reshape(N, a_width), x2, p[0].reshape(N, ple_dim), a_w_out[0].astype(BF16), row(g_ple[0]),
      w_ple_gate[0].astype(BF16), w_ple[0].astype(BF16), row(g_pre[1]), wq, wk, wv, wg,
      cos_t, sin_next, sin_prev)

    log_gamma = jnp.log(1.0 - 2.0 ** (-5.0 - jnp.arange(r_heads, dtype=F32)))
    ci = jnp.arange(R_CHUNK, dtype=F32)
    diff = ci[:, None] - ci[None, :]
    dmask = jnp.where(diff[None] >= 0, jnp.exp(diff[None] * log_gamma[:, None, None]), 0.0)
    xi = jnp.exp((ci[None, :] + 1.0) * log_gamma[:, None])
    zeta = jnp.exp((R_CHUNK - 1.0 - ci[None, :]) * log_gamma[:, None])
    decay = jnp.exp(R_CHUNK * log_gamma)
    xi_b = jnp.broadcast_to(xi[:, :, None], (r_heads, R_CHUNK, r_v))
    zeta_b = jnp.broadcast_to(zeta[:, :, None], (r_heads, R_CHUNK, r_qk))
    decay_b = jnp.broadcast_to(decay[:, None, None], (r_heads, 1, r_v))
    rows_step = R_CHUNK * math.gcd(R_STEP_CHUNKS, L // R_CHUNK)
    cblk = lambda w: pl.BlockSpec((None, rows_step, w), lambda b, c: (b, c, 0))
    og1 = pl.pallas_call(
        functools.partial(_ret_kernel, heads=r_heads, qk_dim=r_qk, v_dim=r_v),
        grid=(B, L // rows_step),
        in_specs=[cblk(D), cblk(D), cblk(r_width), cblk(r_width),
                  _const_spec((r_heads, R_CHUNK, R_CHUNK)), _const_spec((r_heads, R_CHUNK, r_v)),
                  _const_spec((r_heads, R_CHUNK, r_qk)), _const_spec((r_heads, 1, r_v))],
        out_specs=cblk(r_width),
        out_shape=jax.ShapeDtypeStruct((B, L, r_width), BF16),
        scratch_shapes=[pltpu.VMEM((r_heads, r_qk, r_v), F32)],
        compiler_params=_params("parallel", "arbitrary"),
        name="retention",
    )(rq.reshape(B, L, D), rk.reshape(B, L, D), rv.reshape(B, L, r_width), sg1.reshape(B, L, r_width),
      dmask, xi_b, zeta_b, decay_b)

    tm2 = 512
    tk2 = lambda w: pl.BlockSpec((tm2, w), lambda i: (i, 0))
    out = pl.pallas_call(
        _out_kernel,
        grid=(N // tm2,),
        in_specs=[tk2(r_width), tk2(D), tk2(ple_dim), _const_spec((r_width, D)), _const_spec((1, D)),
                  _const_spec((D, D)), _const_spec((ple_dim, D)), _const_spec((1, D))],
        out_specs=tk2(D),
        out_shape=jax.ShapeDtypeStruct((N, D), F32),
        compiler_params=_params("parallel"),
        name="out_proj",
    )(og1.reshape(N, r_width), h1, p[1].reshape(N, ple_dim), r_w_out[0].astype(BF16), row(g_ple[1]),
      w_ple_gate[1].astype(BF16), w_ple[1].astype(BF16), row(g_final))
    return out.reshape(B, L, D)
```

```python
import functools
import math

import jax
import jax.numpy as jnp
from jax import lax
from jax.experimental import pallas as pl
from jax.experimental.pallas import tpu as pltpu

F32 = jnp.float32
BF16 = jnp.bfloat16

NORM_EPS = 1e-6
TOPK_MAX = 256
Q_BLOCK = 256
KEY_CHUNK = 256
R_CHUNK = 128
LANES = 128
MASK_BIAS = -1e30
VALUE_BISECT_MAX = 34
VMEM_LIMIT = 56 * 1024 * 1024
HEAD_GROUPS = 4
R_STEP_CHUNKS = 4


def _passes_per_body(n_chunks):
    return max(1, min(4, round(4 / n_chunks)))


def _rms(x, g):
    return x * lax.rsqrt(jnp.mean(x * x, axis=-1, keepdims=True) + NORM_EPS) * g


def _dot(a, b):
    return jnp.dot(a, b, preferred_element_type=F32)


def _dot_nt(a, b):
    return lax.dot_general(a, b, (((1,), (1,)), ((), ())), preferred_element_type=F32)


def _dot_tn(a, b):
    return lax.dot_general(a, b, (((0,), (0,)), ((), ())), preferred_element_type=F32)


def _const_spec(shape):
    nd = len(shape)
    return pl.BlockSpec(shape, lambda *_: (0,) * nd)


def _proj0_kernel(x_ref, gpre_ref, w0_ref, gq_ref, gkv_ref, gik_ref, bik_ref, wqup_ref, wukt_ref,
                  widxq_ref, qabs_ref, qidx_ref, ka_ref, kb_ref, wi_ref, ckv_ref, sg_ref,
                  *, q_rank, kv_rank, idx_dim, heads, qk_dim, qk_scale, widx_scale):
    hn = _rms(x_ref[...], gpre_ref[...])
    z = _dot(hn.astype(BF16), w0_ref[...])
    o_kv = q_rank
    o_ki = q_rank + kv_rank
    o_g = o_ki + LANES
    cq = _rms(z[:, :o_kv], gq_ref[...]).astype(BF16)
    ckv_ref[...] = _rms(z[:, o_kv:o_ki], gkv_ref[...]).astype(BF16)
    ki = z[:, o_ki:o_g]
    lane = lax.broadcasted_iota(jnp.int32, (1, LANES), 1)
    in_k = lane < idx_dim
    mu = jnp.sum(jnp.where(in_k, ki, 0.0), axis=-1, keepdims=True) * (1.0 / idx_dim)
    d = jnp.where(in_k, ki - mu, 0.0)
    var = jnp.sum(d * d, axis=-1, keepdims=True) * (1.0 / idx_dim)
    kn = d * lax.rsqrt(var + NORM_EPS) * gik_ref[...] + bik_ref[...]
    ka_ref[...] = kn.astype(BF16)
    kb_ref[...] = pltpu.roll(kn, idx_dim, 1).astype(BF16)
    wi_ref[...] = ki * widx_scale
    g = z[:, o_g:]
    sg_ref[...] = (g * jax.nn.sigmoid(g)).astype(BF16)
    q = _dot(cq, wqup_ref[...])
    for h in range(heads):
        qh = q[:, h * qk_dim:(h + 1) * qk_dim].astype(BF16)
        qabs_ref[h] = (_dot(qh, wukt_ref[h]) * qk_scale).astype(BF16)
    qidx_ref[...] = _dot(cq, widxq_ref[...]).astype(BF16)


def _select_kernel(qidx_ref, ka_ref, kb_ref, wi_ref, bias_ref, st_ref, *, topk, heads, idx_dim, max_chunks):
    qb = pl.program_id(1)
    n_kc = (qb * Q_BLOCK) // KEY_CHUNK + 1
    kf = float(topk)
    pairs = heads // 2

    qi = qidx_ref[...]
    qst = jnp.concatenate([qi[:, j * LANES:(j + 1) * LANES] for j in range(pairs)], axis=0)
    wit = wi_ref[...].T
    w_rows = [wit[idx_dim + h:idx_dim + h + 1, :] for h in range(heads)]
    tq = qb * Q_BLOCK + lax.broadcasted_iota(jnp.int32, (1, Q_BLOCK), 1)
    n_causal = (tq + 1).astype(F32)
    small = n_causal <= kf
    row_i = lax.broadcasted_iota(jnp.int32, (KEY_CHUNK, KEY_CHUNK), 0)
    col_i = lax.broadcasted_iota(jnp.int32, (KEY_CHUNK, KEY_CHUNK), 1)
    eye = (row_i == col_i).astype(BF16)[:Q_BLOCK, :Q_BLOCK]
    lower_tri = (col_i <= row_i).astype(BF16)

    def fold8(x, op):
        return op(x.reshape(x.shape[0] // 8, 8, Q_BLOCK), axis=0)

    def variant(nc):
        chunks = [(c, c * KEY_CHUNK) for c in range(nc)]

        mn = mx = cgt = cge = None
        for c, k0 in chunks:
            sa = _dot_nt(ka_ref[k0:k0 + KEY_CHUNK, :], qst)
            sb = _dot_nt(kb_ref[k0:k0 + KEY_CHUNK, :], qst)
            acc = None
            for j in range(pairs):
                ta = jnp.maximum(sa[:, j * Q_BLOCK:(j + 1) * Q_BLOCK], 0.0) * w_rows[2 * j]
                tb = jnp.maximum(sb[:, j * Q_BLOCK:(j + 1) * Q_BLOCK], 0.0) * w_rows[2 * j + 1]
                acc = ta + tb if acc is None else acc + ta + tb
            if c == nc - 1:
                kpos = k0 + lax.broadcasted_iota(jnp.int32, (KEY_CHUNK, Q_BLOCK), 0)
                causal = kpos <= tq
                s = jnp.where(causal, acc, -jnp.inf)
                s_min = jnp.where(causal, acc, jnp.inf)
            else:
                s = s_min = acc
            st_ref[k0:k0 + KEY_CHUNK, :] = s
            parts = (fold8(s_min, jnp.min), fold8(s, jnp.max),
                     fold8(jnp.where(s > 0.0, 1.0, 0.0), jnp.sum),
                     fold8(jnp.where(s >= 0.0, 1.0, 0.0), jnp.sum))
            if mn is None:
                mn, mx, cgt, cge = parts
            else:
                mn, mx = jnp.minimum(mn, parts[0]), jnp.maximum(mx, parts[1])
                cgt, cge = cgt + parts[2], cge + parts[3]
        mn = jnp.min(mn, axis=0, keepdims=True)
        mx = jnp.max(mx, axis=0, keepdims=True)
        cgt0 = jnp.sum(cgt, axis=0, keepdims=True)
        cge0 = jnp.sum(cge, axis=0, keepdims=True)

        def count_gt(ref, thr):
            acc = None
            for _, k0 in chunks:
                part = fold8(jnp.where(ref[k0:k0 + KEY_CHUNK, :] > thr, 1.0, 0.0), jnp.sum)
                acc = part if acc is None else acc + part
            return jnp.sum(acc, axis=0, keepdims=True)

        pos = cgt0 >= kf
        zero = jnp.logical_and(jnp.logical_not(pos), cge0 >= kf)
        zero = jnp.logical_and(zero, jnp.logical_not(small))
        lo = jnp.where(pos, 0.0, jnp.where(zero, 0.0, mn * 1.0001))
        hi = jnp.where(pos, mx, 0.0)
        c_lo = jnp.where(pos, cgt0, jnp.where(zero, cgt0, n_causal))
        c_hi = jnp.where(pos, 0.0, cgt0)
        lo = jnp.where(small, -jnp.inf, lo)
        hi = jnp.where(small, -jnp.inf, hi)
        searching = jnp.logical_not(jnp.logical_or(small, zero))

        def unresolved(c_lo):
            bad = jnp.logical_and(c_lo != kf, searching)
            return (jnp.max(jnp.where(bad, 1.0, 0.0)) > 0.0).astype(jnp.int32)

        def bis_cond(carry):
            return jnp.logical_and(carry[0] < VALUE_BISECT_MAX, carry[1] > 0)

        def bisect(state):
            lo, hi, c_lo, c_hi = state
            mid = 0.5 * (lo + hi)
            c = count_gt(st_ref, mid)
            ge = c >= kf
            return (jnp.where(ge, mid, lo), jnp.where(ge, hi, mid),
                    jnp.where(ge, c, c_lo), jnp.where(ge, c_hi, c))

        def bis_body(carry):
            it, _, state = carry
            flag = unresolved(state[2])
            for _ in range(_passes_per_body(nc)):
                state = bisect(state)
            return it + _passes_per_body(nc), flag, state

        _, _, (lo, hi, c_lo, c_hi) = lax.while_loop(
            bis_cond, bis_body, (jnp.int32(0), jnp.int32(1), (lo, hi, c_lo, c_hi)))
        done = jnp.logical_or(c_lo == kf, small)
        thr = jnp.where(done, lo, hi)

        @pl.when(jnp.max(jnp.where(done, 0.0, 1.0)) > 0.0)
        def _():
            need = jnp.where(done, 0.0, kf - c_hi)
            tied = jnp.logical_not(jnp.logical_or(done, zero))
            before = jnp.zeros((1, Q_BLOCK), F32)
            for _, k0 in chunks:
                s = st_ref[k0:k0 + KEY_CHUNK, :]
                band = jnp.logical_or(jnp.logical_and(zero, s == 0.0),
                                      jnp.logical_and(tied, jnp.logical_and(s > lo, s <= hi)))
                rank = _dot(lower_tri, jnp.where(band, 1.0, 0.0).astype(BF16)) + before
                take = jnp.logical_and(band, rank <= need)
                st_ref[k0:k0 + KEY_CHUNK, :] = jnp.where(take, jnp.inf, s)
                before = rank[KEY_CHUNK - 1:KEY_CHUNK, :]

        for _, k0 in chunks:
            sel_t = jnp.where(st_ref[k0:k0 + KEY_CHUNK, :] > thr, 1.0, 0.0).astype(BF16)
            sel = _dot_nt(eye, sel_t)
            bias_ref[:, k0:k0 + KEY_CHUNK] = ((sel - 1.0) * (-MASK_BIAS)).astype(BF16)
        if nc < max_chunks:
            bias_ref[:, nc * KEY_CHUNK:] = jnp.full((Q_BLOCK, (max_chunks - nc) * KEY_CHUNK), MASK_BIAS, BF16)

    for nc in range(1, max_chunks + 1):
        pl.when(n_kc == nc)(functools.partial(variant, nc))


def _attend_kernel(qabs_ref, ckv_ref, bias_ref, sg_ref, wuv_ref, og_ref, s_ref, p_ref, m_ref, acc_ref,
                   l_ref, *, heads, v_dim, max_chunks):
    n_kc = (pl.program_id(1) * Q_BLOCK) // KEY_CHUNK + 1
    sub = KEY_CHUNK // LANES
    gsz = heads // HEAD_GROUPS
    grows = gsz * Q_BLOCK

    def variant(nc):
        nk = nc * KEY_CHUNK
        for g in range(HEAD_GROUPS):
            qg = jnp.concatenate([qabs_ref[g * gsz + i] for i in range(gsz)], axis=0)
            for c in range(nc):
                k0 = c * KEY_CHUNK
                s = _dot_nt(qg, ckv_ref[k0:k0 + KEY_CHUNK, :])
                bias = bias_ref[:, k0:k0 + KEY_CHUNK].astype(F32)
                for i in range(gsz):
                    rows = slice((g * gsz + i) * Q_BLOCK, (g * gsz + i + 1) * Q_BLOCK)
                    sh = s[i * Q_BLOCK:(i + 1) * Q_BLOCK, :] + bias
                    s_ref[rows, k0:k0 + KEY_CHUNK] = sh
                    mh = sh[:, :LANES]
                    for t in range(1, sub):
                        mh = jnp.maximum(mh, sh[:, t * LANES:(t + 1) * LANES])
                    m_ref[rows, :] = mh if c == 0 else jnp.maximum(m_ref[rows, :], mh)
        for g in range(HEAD_GROUPS):
            for i in range(gsz):
                rows = slice((g * gsz + i) * Q_BLOCK, (g * gsz + i + 1) * Q_BLOCK)
                m = jnp.broadcast_to(jnp.max(m_ref[rows, :], axis=-1, keepdims=True), (Q_BLOCK, LANES))
                lsum = None
                for t in range(nc * sub):
                    pt = jnp.exp2(s_ref[rows, t * LANES:(t + 1) * LANES] - m)
                    p_ref[rows, t * LANES:(t + 1) * LANES] = pt.astype(BF16)
                    lsum = pt if lsum is None else lsum + pt
                l_ref[rows, :] = lsum
            acc_ref[g * grows:(g + 1) * grows, :] = _dot(p_ref[g * grows:(g + 1) * grows, :nk], ckv_ref[:nk, :])

    for nc in range(1, max_chunks + 1):
        pl.when(n_kc == nc)(functools.partial(variant, nc))

    inv_l = 1.0 / jnp.sum(l_ref[...], axis=-1, keepdims=True)
    for h in range(heads):
        oh = (acc_ref[h * Q_BLOCK:(h + 1) * Q_BLOCK, :] * inv_l[h * Q_BLOCK:(h + 1) * Q_BLOCK]).astype(BF16)
        ov = _dot(oh, wuv_ref[h])
        og_ref[:, h * v_dim:(h + 1) * v_dim] = (
            ov * sg_ref[:, h * v_dim:(h + 1) * v_dim].astype(F32)).astype(BF16)


def _ple(h, p, gple, wpg, wple):
    gate = jax.nn.sigmoid(_dot(_rms(h, gple).astype(BF16), wpg))
    return h + _dot(p.astype(BF16), wple) * gate


def _rotate(x, cos, sin_next, sin_prev):
    nxt = pltpu.roll(x, LANES - 1, 1)
    prv = pltpu.roll(x, 1, 1)
    return x * cos + nxt * sin_next + prv * sin_prev


def _mid_kernel(og_ref, x_ref, p_ref, wout_ref, gple_ref, wpg_ref, wple_ref, gpre_ref,
                wq_ref, wk_ref, wv_ref, wg_ref, cos_ref, sn_ref, sp_ref,
                h_ref, q_ref, k_ref, v_ref, sg_ref, *, k_scale, qk_dim):
    h = x_ref[...] + _dot(og_ref[...], wout_ref[...])
    h = _ple(h, p_ref[...], gple_ref[...], wpg_ref[...], wple_ref[...])
    h_ref[...] = h
    hb = _rms(h, gpre_ref[...]).astype(BF16)
    q = _dot(hb, wq_ref[...])
    k = _dot(hb, wk_ref[...]) * k_scale
    for j in range(q.shape[1] // LANES):
        t = (j * LANES) % qk_dim
        cs = cos_ref[:, t:t + LANES]
        sn = sn_ref[:, t:t + LANES]
        sp = sp_ref[:, t:t + LANES]
        q_ref[:, j * LANES:(j + 1) * LANES] = _rotate(q[:, j * LANES:(j + 1) * LANES], cs, sn, sp).astype(BF16)
        k_ref[:, j * LANES:(j + 1) * LANES] = _rotate(k[:, j * LANES:(j + 1) * LANES], cs, sn, sp).astype(BF16)
    v_ref[...] = _dot(hb, wv_ref[...]).astype(BF16)
    g = _dot(hb, wg_ref[...])
    sg_ref[...] = (g * jax.nn.sigmoid(g)).astype(BF16)


def _ret_kernel(q_ref, k_ref, v_ref, sg_ref, dmask_ref, xi_ref, zeta_ref, decay_ref, og_ref,
                state_ref, *, heads, qk_dim, v_dim):
    @pl.when(pl.program_id(1) == 0)
    def _():
        state_ref[...] = jnp.zeros(state_ref.shape, F32)

    for c in range(q_ref.shape[0] // R_CHUNK):
        r = slice(c * R_CHUNK, (c + 1) * R_CHUNK)
        for h in range(heads):
            qh = q_ref[r, h * qk_dim:(h + 1) * qk_dim]
            kh = k_ref[r, h * qk_dim:(h + 1) * qk_dim]
            vh = v_ref[r, h * v_dim:(h + 1) * v_dim]
            s = _dot_nt(qh, kh) * dmask_ref[h]
            o = _dot(s.astype(BF16), vh) + _dot(qh, state_ref[h].astype(BF16)) * xi_ref[h]
            kz = (kh.astype(F32) * zeta_ref[h]).astype(BF16)
            state_ref[h] = state_ref[h] * decay_ref[h] + _dot_tn(kz, vh)
            mu = jnp.mean(o, axis=-1, keepdims=True)
            d = o - mu
            var = jnp.mean(d * d, axis=-1, keepdims=True)
            on = d * lax.rsqrt(var + NORM_EPS)
            og_ref[r, h * v_dim:(h + 1) * v_dim] = (
                on * sg_ref[r, h * v_dim:(h + 1) * v_dim].astype(F32)).astype(BF16)


def _out_kernel(og_ref, h_ref, p_ref, wout_ref, gple_ref, wpg_ref, wple_ref, gfin_ref, o_ref):
    h = h_ref[...] + _dot(og_ref[...], wout_ref[...])
    h = _ple(h, p_ref[...], gple_ref[...], wpg_ref[...], wple_ref[...])
    o_ref[...] = _rms(h, gfin_ref[...])


def _params(*sem):
    return pltpu.CompilerParams(dimension_semantics=sem, vmem_limit_bytes=VMEM_LIMIT)


def kernel(x, p, g_pre, a_w_in, a_g_q, a_g_kv, a_w_q_up, a_w_idx_q, a_g_ik, a_b_ik, a_w_uk, a_w_uv,
           a_w_out, r_w_in, r_w_out, w_ple_gate, g_ple, w_ple, g_final):
    B, L, D = x.shape
    N = B * L
    q_rank = a_g_q.shape[-1]
    kv_rank = a_g_kv.shape[-1]
    idx_dim = a_g_ik.shape[-1]
    heads, _, qk_dim = a_w_uk.shape[1:]
    v_dim = a_w_uv.shape[-1]
    idx_heads = a_w_idx_q.shape[-1] // idx_dim
    a_width = heads * v_dim
    ple_dim = p.shape[-1]
    assert idx_heads == heads and 2 * idx_dim == LANES and L % KEY_CHUNK == 0
    assert KEY_CHUNK % Q_BLOCK == 0 and L % Q_BLOCK == 0 and heads % HEAD_GROUPS == 0
    assert q_rank % LANES == 0 and kv_rank % LANES == 0 and a_w_in.shape[0] == 1 and r_w_in.shape[0] == 1
    topk = min(TOPK_MAX, L // 4)
    x2 = x.reshape(N, D)
    row = lambda a: a.reshape(1, -1).astype(F32)

    w_in = a_w_in[0]
    o_ki = q_rank + kv_rank
    o_g = o_ki + idx_dim + idx_heads
    w_ki = jnp.pad(w_in[:, o_ki:o_g], ((0, 0), (0, LANES - (o_g - o_ki))))
    w0 = jnp.concatenate([w_in[:, :o_ki], w_ki, w_in[:, o_g:]], axis=1).astype(BF16)
    pad_k = lambda a: jnp.pad(a.reshape(1, -1).astype(F32), ((0, 0), (0, LANES - idx_dim)))
    wukt = jnp.swapaxes(a_w_uk[0], 1, 2).astype(BF16)
    tm0 = 512
    n0 = w0.shape[1]
    outs0 = pl.pallas_call(
        functools.partial(_proj0_kernel, q_rank=q_rank, kv_rank=kv_rank, idx_dim=idx_dim, heads=heads,
                          qk_dim=qk_dim, qk_scale=qk_dim ** -0.5 * math.log2(math.e),
                          widx_scale=idx_heads ** -0.5 * idx_dim ** -0.5),
        grid=(N // tm0,),
        in_specs=[pl.BlockSpec((tm0, D), lambda i: (i, 0)), _const_spec((1, D)), _const_spec((D, n0)),
                  _const_spec((1, q_rank)), _const_spec((1, kv_rank)), _const_spec((1, LANES)),
                  _const_spec((1, LANES)), _const_spec((q_rank, heads * qk_dim)),
                  _const_spec((heads, qk_dim, kv_rank)), _const_spec((q_rank, idx_heads * idx_dim))],
        out_specs=[pl.BlockSpec((heads, tm0, kv_rank), lambda i: (0, i, 0)),
                   pl.BlockSpec((tm0, idx_heads * idx_dim), lambda i: (i, 0)),
                   pl.BlockSpec((tm0, LANES), lambda i: (i, 0)),
                   pl.BlockSpec((tm0, LANES), lambda i: (i, 0)),
                   pl.BlockSpec((tm0, LANES), lambda i: (i, 0)),
                   pl.BlockSpec((tm0, kv_rank), lambda i: (i, 0)),
                   pl.BlockSpec((tm0, a_width), lambda i: (i, 0))],
        out_shape=[jax.ShapeDtypeStruct((heads, N, kv_rank), BF16),
                   jax.ShapeDtypeStruct((N, idx_heads * idx_dim), BF16),
                   jax.ShapeDtypeStruct((N, LANES), BF16),
                   jax.ShapeDtypeStruct((N, LANES), BF16),
                   jax.ShapeDtypeStruct((N, LANES), F32),
                   jax.ShapeDtypeStruct((N, kv_rank), BF16),
                   jax.ShapeDtypeStruct((N, a_width), BF16)],
        compiler_params=_params("parallel"),
        name="dsa_proj",
    )(x2, row(g_pre[0]), w0, row(a_g_q[0]), row(a_g_kv[0]), pad_k(a_g_ik[0]), pad_k(a_b_ik[0]),
      a_w_q_up[0].astype(BF16), wukt, a_w_idx_q[0].astype(BF16))
    qabs, qidx, ka, kb, wi, ckv, sg0 = outs0

    nb = L // Q_BLOCK
    nkc = L // KEY_CHUNK
    b3 = lambda a: a.reshape(B, L, a.shape[-1])
    qblk = lambda w: pl.BlockSpec((None, Q_BLOCK, w), lambda b, i: (b, i, 0))
    seq = lambda w: pl.BlockSpec((None, L, w), lambda b, i: (b, 0, 0))
    bias = pl.pallas_call(
        functools.partial(_select_kernel, topk=topk, heads=heads, idx_dim=idx_dim, max_chunks=nkc),
        grid=(B, nb),
        in_specs=[qblk(idx_heads * idx_dim), seq(LANES), seq(LANES), qblk(LANES)],
        out_specs=qblk(L),
        out_shape=jax.ShapeDtypeStruct((B, L, L), BF16),
        scratch_shapes=[pltpu.VMEM((L, Q_BLOCK), F32)],
        compiler_params=_params("parallel", "arbitrary"),
        name="dsa_select",
    )(b3(qidx), b3(ka), b3(kb), b3(wi))
    og0 = pl.pallas_call(
        functools.partial(_attend_kernel, heads=heads, v_dim=v_dim, max_chunks=nkc),
        grid=(B, nb),
        in_specs=[pl.BlockSpec((heads, None, Q_BLOCK, kv_rank), lambda b, i: (0, b, i, 0)),
                  seq(kv_rank), qblk(L), qblk(a_width), _const_spec((heads, kv_rank, v_dim))],
        out_specs=qblk(a_width),
        out_shape=jax.ShapeDtypeStruct((B, L, a_width), BF16),
        scratch_shapes=[pltpu.VMEM((heads * Q_BLOCK, L), F32),
                        pltpu.VMEM((heads * Q_BLOCK, L), BF16),
                        pltpu.VMEM((heads * Q_BLOCK, LANES), F32),
                        pltpu.VMEM((heads * Q_BLOCK, kv_rank), F32),
                        pltpu.VMEM((heads * Q_BLOCK, LANES), F32)],
        compiler_params=_params("parallel", "arbitrary"),
        name="dsa_attend",
    )(qabs.reshape(heads, B, L, kv_rank), b3(ckv), bias, b3(sg0), a_w_uv[0].astype(BF16))

    r_heads = 4
    r_qk = D // r_heads
    r_v = 2 * D // r_heads
    r_width = r_heads * r_v
    assert r_w_in.shape[-1] == 2 * D + 2 * r_width and r_qk % LANES == 0 and L % R_CHUNK == 0
    wr = r_w_in[0].astype(BF16)
    wq, wk, wv, wg = wr[:, :D], wr[:, D:2 * D], wr[:, 2 * D:2 * D + r_width], wr[:, 2 * D + r_width:]
    pos = jnp.arange(L, dtype=F32)
    angle = 1.0 / (10000.0 ** jnp.linspace(0.0, 1.0, r_qk // 2, dtype=F32))
    theta = pos[:, None] * jnp.repeat(angle, 2)[None, :]
    even = (jnp.arange(r_qk) % 2 == 0)[None, :]
    cos_t = jnp.cos(theta)
    sin_next = jnp.where(even, -jnp.sin(theta), 0.0)
    sin_prev = jnp.where(even, 0.0, jnp.sin(theta))
    tm1 = 256
    tok = lambda w: pl.BlockSpec((tm1, w), lambda b, i: (b * (L // tm1) + i, 0))
    tab = pl.BlockSpec((tm1, r_qk), lambda b, i: (i, 0))
    h1, rq, rk, rv, sg1 = pl.pallas_call(
        functools.partial(_mid_kernel, k_scale=r_qk ** -0.5, qk_dim=r_qk),
        grid=(B, L // tm1),
        in_specs=[tok(a_width), tok(D), tok(ple_dim), _const_spec((a_width, D)), _const_spec((1, D)),
                  _const_spec((D, D)), _const_spec((ple_dim, D)), _const_spec((1, D)),
                  _const_spec((D, D)), _const_spec((D, D)), _const_spec((D, r_width)),
                  _const_spec((D, r_width)), tab, tab, tab],
        out_specs=[tok(D), tok(D), tok(D), tok(r_width), tok(r_width)],
        out_shape=[jax.ShapeDtypeStruct((N, D), F32), jax.ShapeDtypeStruct((N, D), BF16),
                   jax.ShapeDtypeStruct((N, D), BF16), jax.ShapeDtypeStruct((N, r_width), BF16),
                   jax.ShapeDtypeStruct((N, r_width), BF16)],
        compiler_params=_params("parallel", "arbitrary"),
        name="mid_proj",
    )(og0.reshape(N, a_width), x2, p[0].reshape(N, ple_dim), a_w_out[0].astype(BF16), row(g_ple[0]),
      w_ple_gate[0].astype(BF16), w_ple[0].astype(BF16), row(g_pre[1]), wq, wk, wv, wg,
      cos_t, sin_next, sin_prev)

    log_gamma = jnp.log(1.0 - 2.0 ** (-5.0 - jnp.arange(r_heads, dtype=F32)))
    ci = jnp.arange(R_CHUNK, dtype=F32)
    diff = ci[:, None] - ci[None, :]
    dmask = jnp.where(diff[None] >= 0, jnp.exp(diff[None] * log_gamma[:, None, None]), 0.0)
    xi = jnp.exp((ci[None, :] + 1.0) * log_gamma[:, None])
    zeta = jnp.exp((R_CHUNK - 1.0 - ci[None, :]) * log_gamma[:, None])
    decay = jnp.exp(R_CHUNK * log_gamma)
    xi_b = jnp.broadcast_to(xi[:, :, None], (r_heads, R_CHUNK, r_v))
    zeta_b = jnp.broadcast_to(zeta[:, :, None], (r_heads, R_CHUNK, r_qk))
    decay_b = jnp.broadcast_to(decay[:, None, None], (r_heads, 1, r_v))
    rows_step = R_CHUNK * math.gcd(R_STEP_CHUNKS, L // R_CHUNK)
    cblk = lambda w: pl.BlockSpec((None, rows_step, w), lambda b, c: (b, c, 0))
    og1 = pl.pallas_call(
        functools.partial(_ret_kernel, heads=r_heads, qk_dim=r_qk, v_dim=r_v),
        grid=(B, L // rows_step),
        in_specs=[cblk(D), cblk(D), cblk(r_width), cblk(r_width),
                  _const_spec((r_heads, R_CHUNK, R_CHUNK)), _const_spec((r_heads, R_CHUNK, r_v)),
                  _const_spec((r_heads, R_CHUNK, r_qk)), _const_spec((r_heads, 1, r_v))],
        out_specs=cblk(r_width),
        out_shape=jax.ShapeDtypeStruct((B, L, r_width), BF16),
        scratch_shapes=[pltpu.VMEM((r_heads, r_qk, r_v), F32)],
        compiler_params=_params("parallel", "arbitrary"),
        name="retention",
    )(rq.reshape(B, L, D), rk.reshape(B, L, D), rv.reshape(B, L, r_width), sg1.reshape(B, L, r_width),
      dmask, xi_b, zeta_b, decay_b)

    tm2 = 512
    tk2 = lambda w: pl.BlockSpec((tm2, w), lambda i: (i, 0))
    out = pl.pallas_call(
        _out_kernel,
        grid=(N // tm2,),
        in_specs=[tk2(r_width), tk2(D), tk2(ple_dim), _const_spec((r_width, D)), _const_spec((1, D)),
                  _const_spec((D, D)), _const_spec((ple_dim, D)), _const_spec((1, D))],
        out_specs=tk2(D),
        out_shape=jax.ShapeDtypeStruct((N, D), F32),
        compiler_params=_params("parallel"),
        name="out_proj",
    )(og1.reshape(N, r_width), h1, p[1].reshape(N, ple_dim), r_w_out[0].astype(BF16), row(g_ple[1]),
      w_ple_gate[1].astype(BF16), w_ple[1].astype(BF16), row(g_final))
    return out.reshape(B, L, D)
```

```python
import functools
import math

import jax
import jax.numpy as jnp
import numpy as np
from jax import lax
from jax.experimental import pallas as pl
from jax.experimental.pallas import tpu as pltpu

F32 = jnp.float32
BF16 = jnp.bfloat16

NORM_EPS = 1e-6
TOPK_MAX = 256
Q_BLOCK = 256
KEY_CHUNK = 256
R_CHUNK = 128
LANES = 128
MASK_BIAS = -1e30
VALUE_BISECT_MAX = 34
VMEM_LIMIT = 56 * 1024 * 1024
HEAD_GROUPS = 4
R_STEP_CHUNKS = 4


def _passes_per_body(n_chunks):
    return max(1, min(4, round(6 / n_chunks)))


def _rms(x, g):
    return x * lax.rsqrt(jnp.mean(x * x, axis=-1, keepdims=True) + NORM_EPS) * g


def _dot(a, b):
    return jnp.dot(a, b, preferred_element_type=F32)


def _dot_nt(a, b):
    return lax.dot_general(a, b, (((1,), (1,)), ((), ())), preferred_element_type=F32)


def _dot_tn(a, b):
    return lax.dot_general(a, b, (((0,), (0,)), ((), ())), preferred_element_type=F32)


def _const_spec(shape):
    nd = len(shape)
    return pl.BlockSpec(shape, lambda *_: (0,) * nd)


def _proj0_kernel(x_ref, gpre_ref, w0_ref, gq_ref, gkv_ref, gik_ref, bik_ref, wqup_ref, wukt_ref,
                  widxq_ref, qabs_ref, qidx_ref, ka_ref, kb_ref, wi_ref, ckv_ref, sg_ref,
                  *, q_rank, kv_rank, idx_dim, heads, qk_dim, qk_scale, widx_scale):
    hn = _rms(x_ref[...], gpre_ref[...])
    z = _dot(hn.astype(BF16), w0_ref[...])
    o_kv = q_rank
    o_ki = q_rank + kv_rank
    o_g = o_ki + LANES
    cq = _rms(z[:, :o_kv], gq_ref[...]).astype(BF16)
    ckv_ref[...] = _rms(z[:, o_kv:o_ki], gkv_ref[...]).astype(BF16)
    ki = z[:, o_ki:o_g]
    lane = lax.broadcasted_iota(jnp.int32, (1, LANES), 1)
    in_k = lane < idx_dim
    mu = jnp.sum(jnp.where(in_k, ki, 0.0), axis=-1, keepdims=True) * (1.0 / idx_dim)
    d = jnp.where(in_k, ki - mu, 0.0)
    var = jnp.sum(d * d, axis=-1, keepdims=True) * (1.0 / idx_dim)
    kn = d * lax.rsqrt(var + NORM_EPS) * gik_ref[...] + bik_ref[...]
    ka_ref[...] = kn.astype(BF16)
    kb_ref[...] = pltpu.roll(kn, idx_dim, 1).astype(BF16)
    wi_ref[...] = ki * widx_scale
    g = z[:, o_g:]
    sg_ref[...] = (g * jax.nn.sigmoid(g)).astype(BF16)
    q = _dot(cq, wqup_ref[...])
    for h in range(heads):
        qh = q[:, h * qk_dim:(h + 1) * qk_dim].astype(BF16)
        qabs_ref[h] = (_dot(qh, wukt_ref[h]) * qk_scale).astype(BF16)
    qidx_ref[...] = _dot(cq, widxq_ref[...]).astype(BF16)


def _select_kernel(qidx_ref, ka_ref, kb_ref, wi_ref, bias_ref, st_ref, *, topk, heads, idx_dim, max_chunks):
    qb = pl.program_id(1)
    n_kc = (qb * Q_BLOCK) // KEY_CHUNK + 1
    kf = float(topk)
    pairs = heads // 2

    qi = qidx_ref[...]
    qst = jnp.concatenate([qi[:, j * LANES:(j + 1) * LANES] for j in range(pairs)], axis=0)
    wit = wi_ref[...].T
    w_rows = [wit[idx_dim + h:idx_dim + h + 1, :] for h in range(heads)]
    tq = qb * Q_BLOCK + lax.broadcasted_iota(jnp.int32, (1, Q_BLOCK), 1)
    n_causal = (tq + 1).astype(F32)
    small = n_causal <= kf
    row_i = lax.broadcasted_iota(jnp.int32, (KEY_CHUNK, KEY_CHUNK), 0)
    col_i = lax.broadcasted_iota(jnp.int32, (KEY_CHUNK, KEY_CHUNK), 1)
    eye = (row_i == col_i).astype(BF16)[:Q_BLOCK, :Q_BLOCK]
    lower_tri = (col_i <= row_i).astype(BF16)

    def fold8(x, op):
        return op(x.reshape(x.shape[0] // 8, 8, Q_BLOCK), axis=0)

    def variant(nc):
        chunks = [(c, c * KEY_CHUNK) for c in range(nc)]

        mn = mx = cgt = cge = None
        for c, k0 in chunks:
            sa = _dot_nt(ka_ref[k0:k0 + KEY_CHUNK, :], qst)
            sb = _dot_nt(kb_ref[k0:k0 + KEY_CHUNK, :], qst)
            acc = None
            for j in range(pairs):
                ta = jnp.maximum(sa[:, j * Q_BLOCK:(j + 1) * Q_BLOCK], 0.0) * w_rows[2 * j]
                tb = jnp.maximum(sb[:, j * Q_BLOCK:(j + 1) * Q_BLOCK], 0.0) * w_rows[2 * j + 1]
                acc = ta + tb if acc is None else acc + ta + tb
            if c == nc - 1:
                kpos = k0 + lax.broadcasted_iota(jnp.int32, (KEY_CHUNK, Q_BLOCK), 0)
                causal = kpos <= tq
                s = jnp.where(causal, acc, -jnp.inf)
                s_min = jnp.where(causal, acc, jnp.inf)
            else:
                s = s_min = acc
            st_ref[k0:k0 + KEY_CHUNK, :] = s
            parts = (fold8(s_min, jnp.min), fold8(s, jnp.max),
                     fold8(jnp.where(s > 0.0, 1.0, 0.0), jnp.sum),
                     fold8(jnp.where(s >= 0.0, 1.0, 0.0), jnp.sum))
            if mn is None:
                mn, mx, cgt, cge = parts
            else:
                mn, mx = jnp.minimum(mn, parts[0]), jnp.maximum(mx, parts[1])
                cgt, cge = cgt + parts[2], cge + parts[3]
        mn = jnp.min(mn, axis=0, keepdims=True)
        mx = jnp.max(mx, axis=0, keepdims=True)
        cgt0 = jnp.sum(cgt, axis=0, keepdims=True)
        cge0 = jnp.sum(cge, axis=0, keepdims=True)

        def count_gt(ref, thr):
            acc = None
            for _, k0 in chunks:
                part = fold8(jnp.where(ref[k0:k0 + KEY_CHUNK, :] > thr, 1.0, 0.0), jnp.sum)
                acc = part if acc is None else acc + part
            return jnp.sum(acc, axis=0, keepdims=True)

        pos = cgt0 >= kf
        zero = jnp.logical_and(jnp.logical_not(pos), cge0 >= kf)
        zero = jnp.logical_and(zero, jnp.logical_not(small))
        lo = jnp.where(pos, 0.0, jnp.where(zero, 0.0, mn * 1.0001))
        hi = jnp.where(pos, mx, 0.0)
        c_lo = jnp.where(pos, cgt0, jnp.where(zero, cgt0, n_causal))
        c_hi = jnp.where(pos, 0.0, cgt0)
        lo = jnp.where(small, -jnp.inf, lo)
        hi = jnp.where(small, -jnp.inf, hi)
        searching = jnp.logical_not(jnp.logical_or(small, zero))

        def unresolved(c_lo):
            bad = jnp.logical_and(c_lo != kf, searching)
            return (jnp.max(jnp.where(bad, 1.0, 0.0)) > 0.0).astype(jnp.int32)

        def bis_cond(carry):
            return jnp.logical_and(carry[0] < VALUE_BISECT_MAX, carry[1] > 0)

        def bisect(state):
            lo, hi, c_lo, c_hi = state
            mid = 0.5 * (lo + hi)
            c = count_gt(st_ref, mid)
            ge = c >= kf
            return (jnp.where(ge, mid, lo), jnp.where(ge, hi, mid),
                    jnp.where(ge, c, c_lo), jnp.where(ge, c_hi, c))

        def bis_body(carry):
            it, _, state = carry
            flag = unresolved(state[2])
            for _ in range(_passes_per_body(nc)):
                state = bisect(state)
            return it + _passes_per_body(nc), flag, state

        _, _, (lo, hi, c_lo, c_hi) = lax.while_loop(
            bis_cond, bis_body, (jnp.int32(0), jnp.int32(1), (lo, hi, c_lo, c_hi)))
        done = jnp.logical_or(c_lo == kf, small)
        thr = jnp.where(done, lo, hi)

        @pl.when(jnp.max(jnp.where(done, 0.0, 1.0)) > 0.0)
        def _():
            need = jnp.where(done, 0.0, kf - c_hi)
            tied = jnp.logical_not(jnp.logical_or(done, zero))
            before = jnp.zeros((1, Q_BLOCK), F32)
            for _, k0 in chunks:
                s = st_ref[k0:k0 + KEY_CHUNK, :]
                band = jnp.logical_or(jnp.logical_and(zero, s == 0.0),
                                      jnp.logical_and(tied, jnp.logical_and(s > lo, s <= hi)))
                band_f = jnp.where(band, 1.0, 0.0)
                rank = _dot(lower_tri, band_f.astype(BF16)) + before
                take = jnp.logical_and(band, rank <= need)
                st_ref[k0:k0 + KEY_CHUNK, :] = jnp.where(take, jnp.inf, s)
                before = before + jnp.sum(fold8(band_f, jnp.sum), axis=0, keepdims=True)

        for _, k0 in chunks:
            sel_t = jnp.where(st_ref[k0:k0 + KEY_CHUNK, :] > thr, 1.0, 0.0).astype(BF16)
            sel = _dot_nt(eye, sel_t)
            bias_ref[:, k0:k0 + KEY_CHUNK] = ((sel - 1.0) * (-MASK_BIAS)).astype(BF16)
        if nc < max_chunks:
            bias_ref[:, nc * KEY_CHUNK:] = jnp.full((Q_BLOCK, (max_chunks - nc) * KEY_CHUNK), MASK_BIAS, BF16)

    for nc in range(1, max_chunks + 1):
        pl.when(n_kc == nc)(functools.partial(variant, nc))


def _attend_kernel(qabs_ref, ckv_ref, bias_ref, sg_ref, wuv_ref, og_ref, s_ref, p_ref, m_ref, acc_ref,
                   l_ref, *, heads, v_dim, max_chunks):
    n_kc = (pl.program_id(1) * Q_BLOCK) // KEY_CHUNK + 1
    sub = KEY_CHUNK // LANES
    gsz = heads // HEAD_GROUPS
    grows = gsz * Q_BLOCK

    def variant(nc):
        nk = nc * KEY_CHUNK
        for g in range(HEAD_GROUPS):
            qg = jnp.concatenate([qabs_ref[g * gsz + i] for i in range(gsz)], axis=0)
            for c in range(nc):
                k0 = c * KEY_CHUNK
                s = _dot_nt(qg, ckv_ref[k0:k0 + KEY_CHUNK, :])
                bias = bias_ref[:, k0:k0 + KEY_CHUNK].astype(F32)
                for i in range(gsz):
                    rows = slice((g * gsz + i) * Q_BLOCK, (g * gsz + i + 1) * Q_BLOCK)
                    sh = s[i * Q_BLOCK:(i + 1) * Q_BLOCK, :] + bias
                    s_ref[rows, k0:k0 + KEY_CHUNK] = sh
                    mh = sh[:, :LANES]
                    for t in range(1, sub):
                        mh = jnp.maximum(mh, sh[:, t * LANES:(t + 1) * LANES])
                    m_ref[rows, :] = mh if c == 0 else jnp.maximum(m_ref[rows, :], mh)
        for g in range(HEAD_GROUPS):
            for i in range(gsz):
                rows = slice((g * gsz + i) * Q_BLOCK, (g * gsz + i + 1) * Q_BLOCK)
                m = jnp.broadcast_to(jnp.max(m_ref[rows, :], axis=-1, keepdims=True), (Q_BLOCK, LANES))
                lsum = None
                for t in range(nc * sub):
                    pt = jnp.exp2(s_ref[rows, t * LANES:(t + 1) * LANES] - m)
                    p_ref[rows, t * LANES:(t + 1) * LANES] = pt.astype(BF16)
                    lsum = pt if lsum is None else lsum + pt
                l_ref[rows, :] = lsum
            acc_ref[g * grows:(g + 1) * grows, :] = _dot(p_ref[g * grows:(g + 1) * grows, :nk], ckv_ref[:nk, :])

    for nc in range(1, max_chunks + 1):
        pl.when(n_kc == nc)(functools.partial(variant, nc))

    inv_l = 1.0 / jnp.sum(l_ref[...], axis=-1, keepdims=True)
    for h in range(heads):
        oh = (acc_ref[h * Q_BLOCK:(h + 1) * Q_BLOCK, :] * inv_l[h * Q_BLOCK:(h + 1) * Q_BLOCK]).astype(BF16)
        ov = _dot(oh, wuv_ref[h])
        og_ref[:, h * v_dim:(h + 1) * v_dim] = (
            ov * sg_ref[:, h * v_dim:(h + 1) * v_dim].astype(F32)).astype(BF16)


def _ple(h, p, gple, wpg, wple):
    gate = jax.nn.sigmoid(_dot(_rms(h, gple).astype(BF16), wpg))
    return h + _dot(p.astype(BF16), wple) * gate


def _rotate(x, cos, sin_next, sin_prev):
    nxt = pltpu.roll(x, LANES - 1, 1)
    prv = pltpu.roll(x, 1, 1)
    return x * cos + nxt * sin_next + prv * sin_prev


def _mid_kernel(og_ref, x_ref, p_ref, wout_ref, gple_ref, wpg_ref, wple_ref, gpre_ref,
                wq_ref, wk_ref, wv_ref, wg_ref, cos_ref, sn_ref, sp_ref,
                h_ref, q_ref, k_ref, v_ref, sg_ref, *, k_scale, qk_dim):
    h = x_ref[...] + _dot(og_ref[...], wout_ref[...])
    h = _ple(h, p_ref[...], gple_ref[...], wpg_ref[...], wple_ref[...])
    h_ref[...] = h
    hb = _rms(h, gpre_ref[...]).astype(BF16)
    q = _dot(hb, wq_ref[...])
    k = _dot(hb, wk_ref[...]) * k_scale
    for j in range(q.shape[1] // LANES):
        t = (j * LANES) % qk_dim
        cs = cos_ref[:, t:t + LANES]
        sn = sn_ref[:, t:t + LANES]
        sp = sp_ref[:, t:t + LANES]
        q_ref[:, j * LANES:(j + 1) * LANES] = _rotate(q[:, j * LANES:(j + 1) * LANES], cs, sn, sp).astype(BF16)
        k_ref[:, j * LANES:(j + 1) * LANES] = _rotate(k[:, j * LANES:(j + 1) * LANES], cs, sn, sp).astype(BF16)
    v_ref[...] = _dot(hb, wv_ref[...]).astype(BF16)
    g = _dot(hb, wg_ref[...])
    sg_ref[...] = (g * jax.nn.sigmoid(g)).astype(BF16)


def _ret_kernel(q_ref, k_ref, v_ref, sg_ref, dmask_ref, xi_ref, zeta_ref, decay_ref, og_ref,
                state_ref, *, heads, qk_dim, v_dim):
    @pl.when(pl.program_id(1) == 0)
    def _():
        state_ref[...] = jnp.zeros(state_ref.shape, F32)

    for c in range(q_ref.shape[0] // R_CHUNK):
        r = slice(c * R_CHUNK, (c + 1) * R_CHUNK)
        for h in range(heads):
            qh = q_ref[r, h * qk_dim:(h + 1) * qk_dim]
            kh = k_ref[r, h * qk_dim:(h + 1) * qk_dim]
            vh = v_ref[r, h * v_dim:(h + 1) * v_dim]
            s = _dot_nt(qh, kh) * dmask_ref[h]
            o = _dot(s.astype(BF16), vh) + _dot(qh, state_ref[h].astype(BF16)) * xi_ref[h]
            kz = (kh.astype(F32) * zeta_ref[h]).astype(BF16)
            state_ref[h] = state_ref[h] * decay_ref[h] + _dot_tn(kz, vh)
            mu = jnp.mean(o, axis=-1, keepdims=True)
            d = o - mu
            var = jnp.mean(d * d, axis=-1, keepdims=True)
            on = d * lax.rsqrt(var + NORM_EPS)
            og_ref[r, h * v_dim:(h + 1) * v_dim] = (
                on * sg_ref[r, h * v_dim:(h + 1) * v_dim].astype(F32)).astype(BF16)


def _out_kernel(og_ref, h_ref, p_ref, wout_ref, gple_ref, wpg_ref, wple_ref, gfin_ref, o_ref):
    h = h_ref[...] + _dot(og_ref[...], wout_ref[...])
    h = _ple(h, p_ref[...], gple_ref[...], wpg_ref[...], wple_ref[...])
    o_ref[...] = _rms(h, gfin_ref[...])


def _params(*sem):
    return pltpu.CompilerParams(dimension_semantics=sem, vmem_limit_bytes=VMEM_LIMIT)


def kernel(x, p, g_pre, a_w_in, a_g_q, a_g_kv, a_w_q_up, a_w_idx_q, a_g_ik, a_b_ik, a_w_uk, a_w_uv,
           a_w_out, r_w_in, r_w_out, w_ple_gate, g_ple, w_ple, g_final):
    B, L, D = x.shape
    N = B * L
    q_rank = a_g_q.shape[-1]
    kv_rank = a_g_kv.shape[-1]
    idx_dim = a_g_ik.shape[-1]
    heads, _, qk_dim = a_w_uk.shape[1:]
    v_dim = a_w_uv.shape[-1]
    idx_heads = a_w_idx_q.shape[-1] // idx_dim
    a_width = heads * v_dim
    ple_dim = p.shape[-1]
    assert idx_heads == heads and 2 * idx_dim == LANES and L % KEY_CHUNK == 0
    assert KEY_CHUNK % Q_BLOCK == 0 and L % Q_BLOCK == 0 and heads % HEAD_GROUPS == 0
    assert q_rank % LANES == 0 and kv_rank % LANES == 0 and a_w_in.shape[0] == 1 and r_w_in.shape[0] == 1
    topk = min(TOPK_MAX, L // 4)
    x2 = x.reshape(N, D)
    row = lambda a: a.reshape(1, -1).astype(F32)

    w_in = a_w_in[0]
    o_ki = q_rank + kv_rank
    o_g = o_ki + idx_dim + idx_heads
    w_ki = jnp.pad(w_in[:, o_ki:o_g], ((0, 0), (0, LANES - (o_g - o_ki))))
    w0 = jnp.concatenate([w_in[:, :o_ki], w_ki, w_in[:, o_g:]], axis=1).astype(BF16)
    pad_k = lambda a: jnp.pad(a.reshape(1, -1).astype(F32), ((0, 0), (0, LANES - idx_dim)))
    wukt = jnp.swapaxes(a_w_uk[0], 1, 2).astype(BF16)
    tm0 = 512
    n0 = w0.shape[1]
    outs0 = pl.pallas_call(
        functools.partial(_proj0_kernel, q_rank=q_rank, kv_rank=kv_rank, idx_dim=idx_dim, heads=heads,
                          qk_dim=qk_dim, qk_scale=qk_dim ** -0.5 * math.log2(math.e),
                          widx_scale=idx_heads ** -0.5 * idx_dim ** -0.5),
        grid=(N // tm0,),
        in_specs=[pl.BlockSpec((tm0, D), lambda i: (i, 0)), _const_spec((1, D)), _const_spec((D, n0)),
                  _const_spec((1, q_rank)), _const_spec((1, kv_rank)), _const_spec((1, LANES)),
                  _const_spec((1, LANES)), _const_spec((q_rank, heads * qk_dim)),
                  _const_spec((heads, qk_dim, kv_rank)), _const_spec((q_rank, idx_heads * idx_dim))],
        out_specs=[pl.BlockSpec((heads, tm0, kv_rank), lambda i: (0, i, 0)),
                   pl.BlockSpec((tm0, idx_heads * idx_dim), lambda i: (i, 0)),
                   pl.BlockSpec((tm0, LANES), lambda i: (i, 0)),
                   pl.BlockSpec((tm0, LANES), lambda i: (i, 0)),
                   pl.BlockSpec((tm0, LANES), lambda i: (i, 0)),
                   pl.BlockSpec((tm0, kv_rank), lambda i: (i, 0)),
                   pl.BlockSpec((tm0, a_width), lambda i: (i, 0))],
        out_shape=[jax.ShapeDtypeStruct((heads, N, kv_rank), BF16),
                   jax.ShapeDtypeStruct((N, idx_heads * idx_dim), BF16),
                   jax.ShapeDtypeStruct((N, LANES), BF16),
                   jax.ShapeDtypeStruct((N, LANES), BF16),
                   jax.ShapeDtypeStruct((N, LANES), F32),
                   jax.ShapeDtypeStruct((N, kv_rank), BF16),
                   jax.ShapeDtypeStruct((N, a_width), BF16)],
        compiler_params=_params("parallel"),
        name="dsa_proj",
    )(x2, row(g_pre[0]), w0, row(a_g_q[0]), row(a_g_kv[0]), pad_k(a_g_ik[0]), pad_k(a_b_ik[0]),
      a_w_q_up[0].astype(BF16), wukt, a_w_idx_q[0].astype(BF16))
    qabs, qidx, ka, kb, wi, ckv, sg0 = outs0

    nb = L // Q_BLOCK
    nkc = L // KEY_CHUNK
    b3 = lambda a: a.reshape(B, L, a.shape[-1])
    qblk = lambda w: pl.BlockSpec((None, Q_BLOCK, w), lambda b, i: (b, i, 0))
    seq = lambda w: pl.BlockSpec((None, L, w), lambda b, i: (b, 0, 0))
    bias = pl.pallas_call(
        functools.partial(_select_kernel, topk=topk, heads=heads, idx_dim=idx_dim, max_chunks=nkc),
        grid=(B, nb),
        in_specs=[qblk(idx_heads * idx_dim), seq(LANES), seq(LANES), qblk(LANES)],
        out_specs=qblk(L),
        out_shape=jax.ShapeDtypeStruct((B, L, L), BF16),
        scratch_shapes=[pltpu.VMEM((L, Q_BLOCK), F32)],
        compiler_params=_params("parallel", "arbitrary"),
        name="dsa_select",
    )(b3(qidx), b3(ka), b3(kb), b3(wi))
    og0 = pl.pallas_call(
        functools.partial(_attend_kernel, heads=heads, v_dim=v_dim, max_chunks=nkc),
        grid=(B, nb),
        in_specs=[pl.BlockSpec((heads, None, Q_BLOCK, kv_rank), lambda b, i: (0, b, i, 0)),
                  seq(kv_rank), qblk(L), qblk(a_width), _const_spec((heads, kv_rank, v_dim))],
        out_specs=qblk(a_width),
        out_shape=jax.ShapeDtypeStruct((B, L, a_width), BF16),
        scratch_shapes=[pltpu.VMEM((heads * Q_BLOCK, L), F32),
                        pltpu.VMEM((heads * Q_BLOCK, L), BF16),
                        pltpu.VMEM((heads * Q_BLOCK, LANES), F32),
                        pltpu.VMEM((heads * Q_BLOCK, kv_rank), F32),
                        pltpu.VMEM((heads * Q_BLOCK, LANES), F32)],
        compiler_params=_params("parallel", "arbitrary"),
        name="dsa_attend",
    )(qabs.reshape(heads, B, L, kv_rank), b3(ckv), bias, b3(sg0), a_w_uv[0].astype(BF16))

    r_heads = 4
    r_qk = D // r_heads
    r_v = 2 * D // r_heads
    r_width = r_heads * r_v
    assert r_w_in.shape[-1] == 2 * D + 2 * r_width and r_qk % LANES == 0 and L % R_CHUNK == 0
    wr = r_w_in[0].astype(BF16)
    wq, wk, wv, wg = wr[:, :D], wr[:, D:2 * D], wr[:, 2 * D:2 * D + r_width], wr[:, 2 * D + r_width:]
    angle = 1.0 / (10000.0 ** np.linspace(0.0, 1.0, r_qk // 2))
    theta = np.arange(L, dtype=np.float64)[:, None] * np.repeat(angle, 2)[None, :]
    even = (np.arange(r_qk) % 2 == 0)[None, :]
    cos_t = jnp.asarray(np.cos(theta), F32)
    sin_next = jnp.asarray(np.where(even, -np.sin(theta), 0.0), F32)
    sin_prev = jnp.asarray(np.where(even, 0.0, np.sin(theta)), F32)
    tm1 = 256
    tok = lambda w: pl.BlockSpec((tm1, w), lambda b, i: (b * (L // tm1) + i, 0))
    tab = pl.BlockSpec((tm1, r_qk), lambda b, i: (i, 0))
    h1, rq, rk, rv, sg1 = pl.pallas_call(
        functools.partial(_mid_kernel, k_scale=r_qk ** -0.5, qk_dim=r_qk),
        grid=(B, L // tm1),
        in_specs=[tok(a_width), tok(D), tok(ple_dim), _const_spec((a_width, D)), _const_spec((1, D)),
                  _const_spec((D, D)), _const_spec((ple_dim, D)), _const_spec((1, D)),
                  _const_spec((D, D)), _const_spec((D, D)), _const_spec((D, r_width)),
                  _const_spec((D, r_width)), tab, tab, tab],
        out_specs=[tok(D), tok(D), tok(D), tok(r_width), tok(r_width)],
        out_shape=[jax.ShapeDtypeStruct((N, D), F32), jax.ShapeDtypeStruct((N, D), BF16),
                   jax.ShapeDtypeStruct((N, D), BF16), jax.ShapeDtypeStruct((N, r_width), BF16),
                   jax.ShapeDtypeStruct((N, r_width), BF16)],
        compiler_params=_params("parallel", "arbitrary"),
        name="mid_proj",
    )(og0.reshape(N, a_width), x2, p[0].reshape(N, ple_dim), a_w_out[0].astype(BF16), row(g_ple[0]),
      w_ple_gate[0].astype(BF16), w_ple[0].astype(BF16), row(g_pre[1]), wq, wk, wv, wg,
      cos_t, sin_next, sin_prev)

    log_gamma = np.log(1.0 - 2.0 ** (-5.0 - np.arange(r_heads, dtype=np.float64)))
    ci = np.arange(R_CHUNK, dtype=np.float64)
    diff = ci[:, None] - ci[None, :]
    dmask = jnp.asarray(np.where(diff[None] >= 0, np.exp(diff[None] * log_gamma[:, None, None]), 0.0), F32)
    xi = np.exp((ci[None, :] + 1.0) * log_gamma[:, None])
    zeta = np.exp((R_CHUNK - 1.0 - ci[None, :]) * log_gamma[:, None])
    decay = np.exp(R_CHUNK * log_gamma)
    xi_b = jnp.asarray(np.broadcast_to(xi[:, :, None], (r_heads, R_CHUNK, r_v)), F32)
    zeta_b = jnp.asarray(np.broadcast_to(zeta[:, :, None], (r_heads, R_CHUNK, r_qk)), F32)
    decay_b = jnp.asarray(np.broadcast_to(decay[:, None, None], (r_heads, 1, r_v)), F32)
    rows_step = R_CHUNK * math.gcd(R_STEP_CHUNKS, L // R_CHUNK)
    cblk = lambda w: pl.BlockSpec((None, rows_step, w), lambda b, c: (b, c, 0))
    og1 = pl.pallas_call(
        functools.partial(_ret_kernel, heads=r_heads, qk_dim=r_qk, v_dim=r_v),
        grid=(B, L // rows_step),
        in_specs=[cblk(D), cblk(D), cblk(r_width), cblk(r_width),
                  _const_spec((r_heads, R_CHUNK, R_CHUNK)), _const_spec((r_heads, R_CHUNK, r_v)),
                  _const_spec((r_heads, R_CHUNK, r_qk)), _const_spec((r_heads, 1, r_v))],
        out_specs=cblk(r_width),
        out_shape=jax.ShapeDtypeStruct((B, L, r_width), BF16),
        scratch_shapes=[pltpu.VMEM((r_heads, r_qk, r_v), F32)],
        compiler_params=_params("parallel", "arbitrary"),
        name="retention",
    )(rq.reshape(B, L, D), rk.reshape(B, L, D), rv.reshape(B, L, r_width), sg1.reshape(B, L, r_width),
      dmask, xi_b, zeta_b, decay_b)

    tm2 = 512
    tk2 = lambda w: pl.BlockSpec((tm2, w), lambda i: (i, 0))
    out = pl.pallas_call(
        _out_kernel,
        grid=(N // tm2,),
        in_specs=[tk2(r_width), tk2(D), tk2(ple_dim), _const_spec((r_width, D)), _const_spec((1, D)),
                  _const_spec((D, D)), _const_spec((ple_dim, D)), _const_spec((1, D))],
        out_specs=tk2(D),
        out_shape=jax.ShapeDtypeStruct((N, D), F32),
        compiler_params=_params("parallel"),
        name="out_proj",
    )(og1.reshape(N, r_width), h1, p[1].reshape(N, ple_dim), r_w_out[0].astype(BF16), row(g_ple[1]),
      w_ple_gate[1].astype(BF16), w_ple[1].astype(BF16), row(g_final))
    return out.reshape(B, L, D)
```

```python
import functools
import math

import jax
import jax.numpy as jnp
import numpy as np
from jax import lax
from jax.experimental import pallas as pl
from jax.experimental.pallas import tpu as pltpu

F32 = jnp.float32
BF16 = jnp.bfloat16

NORM_EPS = 1e-6
TOPK_MAX = 256
Q_BLOCK = 256
KEY_CHUNK = 256
R_CHUNK = 128
LANES = 128
MASK_BIAS = -1e30
VALUE_BISECT_MAX = 34
VMEM_LIMIT = 56 * 1024 * 1024
HEAD_GROUPS = 4
R_STEP_CHUNKS = 8


def _passes_per_body(n_chunks):
    return max(1, min(4, round(6 / n_chunks)))


def _rms(x, g):
    return x * lax.rsqrt(jnp.mean(x * x, axis=-1, keepdims=True) + NORM_EPS) * g


def _dot(a, b):
    return jnp.dot(a, b, preferred_element_type=F32)


def _dot_nt(a, b):
    return lax.dot_general(a, b, (((1,), (1,)), ((), ())), preferred_element_type=F32)


def _dot_tn(a, b):
    return lax.dot_general(a, b, (((0,), (0,)), ((), ())), preferred_element_type=F32)


def _const_spec(shape):
    nd = len(shape)
    return pl.BlockSpec(shape, lambda *_: (0,) * nd)


def _proj0_kernel(x_ref, gpre_ref, w0_ref, gq_ref, gkv_ref, gik_ref, bik_ref, wqup_ref, wukt_ref,
                  widxq_ref, qabs_ref, qidx_ref, ka_ref, kb_ref, wi_ref, ckv_ref, sg_ref,
                  *, q_rank, kv_rank, idx_dim, heads, qk_dim, qk_scale, widx_scale):
    hn = _rms(x_ref[...], gpre_ref[...])
    z = _dot(hn.astype(BF16), w0_ref[...])
    o_kv = q_rank
    o_ki = q_rank + kv_rank
    o_g = o_ki + LANES
    cq = _rms(z[:, :o_kv], gq_ref[...]).astype(BF16)
    ckv_ref[...] = _rms(z[:, o_kv:o_ki], gkv_ref[...]).astype(BF16)
    ki = z[:, o_ki:o_g]
    lane = lax.broadcasted_iota(jnp.int32, (1, LANES), 1)
    in_k = lane < idx_dim
    mu = jnp.sum(jnp.where(in_k, ki, 0.0), axis=-1, keepdims=True) * (1.0 / idx_dim)
    d = jnp.where(in_k, ki - mu, 0.0)
    var = jnp.sum(d * d, axis=-1, keepdims=True) * (1.0 / idx_dim)
    kn = d * lax.rsqrt(var + NORM_EPS) * gik_ref[...] + bik_ref[...]
    ka_ref[...] = kn.astype(BF16)
    kb_ref[...] = pltpu.roll(kn, idx_dim, 1).astype(BF16)
    wi_ref[...] = ki * widx_scale
    g = z[:, o_g:]
    sg_ref[...] = (g * jax.nn.sigmoid(g)).astype(BF16)
    q = _dot(cq, wqup_ref[...])
    for h in range(heads):
        qh = q[:, h * qk_dim:(h + 1) * qk_dim].astype(BF16)
        qabs_ref[h] = (_dot(qh, wukt_ref[h]) * qk_scale).astype(BF16)
    qidx_ref[...] = _dot(cq, widxq_ref[...]).astype(BF16)


def _select_kernel(qidx_ref, ka_ref, kb_ref, wi_ref, bias_ref, st_ref, *, topk, heads, idx_dim, max_chunks):
    qb = pl.program_id(1)
    n_kc = (qb * Q_BLOCK) // KEY_CHUNK + 1
    kf = float(topk)
    pairs = heads // 2

    qi = qidx_ref[...]
    qst = jnp.concatenate([qi[:, j * LANES:(j + 1) * LANES] for j in range(pairs)], axis=0)
    wit = wi_ref[...].T
    w_rows = [wit[idx_dim + h:idx_dim + h + 1, :] for h in range(heads)]
    tq = qb * Q_BLOCK + lax.broadcasted_iota(jnp.int32, (1, Q_BLOCK), 1)
    n_causal = (tq + 1).astype(F32)
    small = n_causal <= kf
    row_i = lax.broadcasted_iota(jnp.int32, (KEY_CHUNK, KEY_CHUNK), 0)
    col_i = lax.broadcasted_iota(jnp.int32, (KEY_CHUNK, KEY_CHUNK), 1)
    eye = (row_i == col_i).astype(BF16)[:Q_BLOCK, :Q_BLOCK]
    lower_tri = (col_i <= row_i).astype(BF16)

    def fold8(x, op):
        return op(x.reshape(x.shape[0] // 8, 8, Q_BLOCK), axis=0)

    def variant(nc):
        chunks = [(c, c * KEY_CHUNK) for c in range(nc)]

        mn = mx = cgt = cge = None
        for c, k0 in chunks:
            sa = _dot_nt(ka_ref[k0:k0 + KEY_CHUNK, :], qst)
            sb = _dot_nt(kb_ref[k0:k0 + KEY_CHUNK, :], qst)
            acc = None
            for j in range(pairs):
                ta = jnp.maximum(sa[:, j * Q_BLOCK:(j + 1) * Q_BLOCK], 0.0) * w_rows[2 * j]
                tb = jnp.maximum(sb[:, j * Q_BLOCK:(j + 1) * Q_BLOCK], 0.0) * w_rows[2 * j + 1]
                acc = ta + tb if acc is None else acc + ta + tb
            if c == nc - 1:
                kpos = k0 + lax.broadcasted_iota(jnp.int32, (KEY_CHUNK, Q_BLOCK), 0)
                causal = kpos <= tq
                s = jnp.where(causal, acc, -jnp.inf)
                s_min = jnp.where(causal, acc, jnp.inf)
            else:
                s = s_min = acc
            st_ref[k0:k0 + KEY_CHUNK, :] = s
            parts = (fold8(s_min, jnp.min), fold8(s, jnp.max),
                     fold8(jnp.where(s > 0.0, 1.0, 0.0), jnp.sum),
                     fold8(jnp.where(s >= 0.0, 1.0, 0.0), jnp.sum))
            if mn is None:
                mn, mx, cgt, cge = parts
            else:
                mn, mx = jnp.minimum(mn, parts[0]), jnp.maximum(mx, parts[1])
                cgt, cge = cgt + parts[2], cge + parts[3]
        mn = jnp.min(mn, axis=0, keepdims=True)
        mx = jnp.max(mx, axis=0, keepdims=True)
        cgt0 = jnp.sum(cgt, axis=0, keepdims=True)
        cge0 = jnp.sum(cge, axis=0, keepdims=True)

        def count_gt(ref, thr):
            acc = None
            for _, k0 in chunks:
                part = fold8(jnp.where(ref[k0:k0 + KEY_CHUNK, :] > thr, 1.0, 0.0), jnp.sum)
                acc = part if acc is None else acc + part
            return jnp.sum(acc, axis=0, keepdims=True)

        pos = cgt0 >= kf
        zero = jnp.logical_and(jnp.logical_not(pos), cge0 >= kf)
        zero = jnp.logical_and(zero, jnp.logical_not(small))
        lo = jnp.where(pos, 0.0, jnp.where(zero, 0.0, mn * 1.0001))
        hi = jnp.where(pos, mx, 0.0)
        c_lo = jnp.where(pos, cgt0, jnp.where(zero, cgt0, n_causal))
        c_hi = jnp.where(pos, 0.0, cgt0)
        lo = jnp.where(small, -jnp.inf, lo)
        hi = jnp.where(small, -jnp.inf, hi)
        searching = jnp.logical_not(jnp.logical_or(small, zero))

        def unresolved(c_lo):
            bad = jnp.logical_and(c_lo != kf, searching)
            return (jnp.max(jnp.where(bad, 1.0, 0.0)) > 0.0).astype(jnp.int32)

        def bis_cond(carry):
            return jnp.logical_and(carry[0] < VALUE_BISECT_MAX, carry[1] > 0)

        def bisect(state):
            lo, hi, c_lo, c_hi = state
            mid = 0.5 * (lo + hi)
            c = count_gt(st_ref, mid)
            ge = c >= kf
            return (jnp.where(ge, mid, lo), jnp.where(ge, hi, mid),
                    jnp.where(ge, c, c_lo), jnp.where(ge, c_hi, c))

        def bis_body(carry):
            it, _, state = carry
            flag = unresolved(state[2])
            for _ in range(_passes_per_body(nc)):
                state = bisect(state)
            return it + _passes_per_body(nc), flag, state

        _, _, (lo, hi, c_lo, c_hi) = lax.while_loop(
            bis_cond, bis_body, (jnp.int32(0), jnp.int32(1), (lo, hi, c_lo, c_hi)))
        done = jnp.logical_or(c_lo == kf, small)
        thr = jnp.where(done, lo, hi)

        @pl.when(jnp.max(jnp.where(done, 0.0, 1.0)) > 0.0)
        def _():
            need = jnp.where(done, 0.0, kf - c_hi)
            tied = jnp.logical_not(jnp.logical_or(done, zero))
            before = jnp.zeros((1, Q_BLOCK), F32)
            for _, k0 in chunks:
                s = st_ref[k0:k0 + KEY_CHUNK, :]
                band = jnp.logical_or(jnp.logical_and(zero, s == 0.0),
                                      jnp.logical_and(tied, jnp.logical_and(s > lo, s <= hi)))
                band_f = jnp.where(band, 1.0, 0.0)
                rank = _dot(lower_tri, band_f.astype(BF16)) + before
                take = jnp.logical_and(band, rank <= need)
                st_ref[k0:k0 + KEY_CHUNK, :] = jnp.where(take, jnp.inf, s)
                before = before + jnp.sum(fold8(band_f, jnp.sum), axis=0, keepdims=True)

        for _, k0 in chunks:
            sel_t = jnp.where(st_ref[k0:k0 + KEY_CHUNK, :] > thr, 1.0, 0.0).astype(BF16)
            sel = _dot_nt(eye, sel_t)
            bias_ref[:, k0:k0 + KEY_CHUNK] = ((sel - 1.0) * (-MASK_BIAS)).astype(BF16)
        if nc < max_chunks:
            bias_ref[:, nc * KEY_CHUNK:] = jnp.full((Q_BLOCK, (max_chunks - nc) * KEY_CHUNK), MASK_BIAS, BF16)

    for nc in range(1, max_chunks + 1):
        pl.when(n_kc == nc)(functools.partial(variant, nc))


def _attend_kernel(qabs_ref, ckv_ref, bias_ref, sg_ref, wuv_ref, og_ref, s_ref, p_ref, m_ref, acc_ref,
                   l_ref, *, heads, v_dim, max_chunks):
    n_kc = (pl.program_id(1) * Q_BLOCK) // KEY_CHUNK + 1
    sub = KEY_CHUNK // LANES
    gsz = heads // HEAD_GROUPS
    grows = gsz * Q_BLOCK

    def variant(nc):
        nk = nc * KEY_CHUNK
        q_all = jnp.concatenate([qabs_ref[h] for h in range(heads)], axis=0)
        for c in range(nc):
            k0 = c * KEY_CHUNK
            s = _dot_nt(q_all, ckv_ref[k0:k0 + KEY_CHUNK, :])
            bias = bias_ref[:, k0:k0 + KEY_CHUNK].astype(F32)
            for h in range(heads):
                rows = slice(h * Q_BLOCK, (h + 1) * Q_BLOCK)
                sh = s[rows, :] + bias
                s_ref[rows, k0:k0 + KEY_CHUNK] = sh
                mh = sh[:, :LANES]
                for t in range(1, sub):
                    mh = jnp.maximum(mh, sh[:, t * LANES:(t + 1) * LANES])
                m_ref[rows, :] = mh if c == 0 else jnp.maximum(m_ref[rows, :], mh)
        for g in range(HEAD_GROUPS):
            for i in range(gsz):
                rows = slice((g * gsz + i) * Q_BLOCK, (g * gsz + i + 1) * Q_BLOCK)
                m = jnp.broadcast_to(jnp.max(m_ref[rows, :], axis=-1, keepdims=True), (Q_BLOCK, LANES))
                lsum = None
                for t in range(nc * sub):
                    pt = jnp.exp2(s_ref[rows, t * LANES:(t + 1) * LANES] - m)
                    p_ref[rows, t * LANES:(t + 1) * LANES] = pt.astype(BF16)
                    lsum = pt if lsum is None else lsum + pt
                l_ref[rows, :] = lsum
            acc_ref[g * grows:(g + 1) * grows, :] = _dot(p_ref[g * grows:(g + 1) * grows, :nk], ckv_ref[:nk, :])

    for nc in range(1, max_chunks + 1):
        pl.when(n_kc == nc)(functools.partial(variant, nc))

    inv_l = 1.0 / jnp.sum(l_ref[...], axis=-1, keepdims=True)
    for h in range(heads):
        oh = (acc_ref[h * Q_BLOCK:(h + 1) * Q_BLOCK, :] * inv_l[h * Q_BLOCK:(h + 1) * Q_BLOCK]).astype(BF16)
        ov = _dot(oh, wuv_ref[h])
        og_ref[:, h * v_dim:(h + 1) * v_dim] = (
            ov * sg_ref[:, h * v_dim:(h + 1) * v_dim].astype(F32)).astype(BF16)


def _ple(h, p, gple, wpg, wple):
    gate = jax.nn.sigmoid(_dot(_rms(h, gple).astype(BF16), wpg))
    return h + _dot(p.astype(BF16), wple) * gate


def _rotate(x, cos, sin_next, sin_prev):
    nxt = pltpu.roll(x, LANES - 1, 1)
    prv = pltpu.roll(x, 1, 1)
    return x * cos + nxt * sin_next + prv * sin_prev


def _mid_kernel(og_ref, x_ref, p_ref, wout_ref, gple_ref, wpg_ref, wple_ref, gpre_ref,
                wq_ref, wk_ref, wv_ref, wg_ref, cos_ref, sn_ref, sp_ref,
                h_ref, q_ref, k_ref, v_ref, sg_ref, *, k_scale, qk_dim):
    h = x_ref[...] + _dot(og_ref[...], wout_ref[...])
    h = _ple(h, p_ref[...], gple_ref[...], wpg_ref[...], wple_ref[...])
    h_ref[...] = h
    hb = _rms(h, gpre_ref[...]).astype(BF16)
    q = _dot(hb, wq_ref[...])
    k = _dot(hb, wk_ref[...]) * k_scale
    for j in range(q.shape[1] // LANES):
        t = (j * LANES) % qk_dim
        cs = cos_ref[:, t:t + LANES]
        sn = sn_ref[:, t:t + LANES]
        sp = sp_ref[:, t:t + LANES]
        q_ref[:, j * LANES:(j + 1) * LANES] = _rotate(q[:, j * LANES:(j + 1) * LANES], cs, sn, sp).astype(BF16)
        k_ref[:, j * LANES:(j + 1) * LANES] = _rotate(k[:, j * LANES:(j + 1) * LANES], cs, sn, sp).astype(BF16)
    v_ref[...] = _dot(hb, wv_ref[...]).astype(BF16)
    g = _dot(hb, wg_ref[...])
    sg_ref[...] = (g * jax.nn.sigmoid(g)).astype(BF16)


def _ret_kernel(q_ref, k_ref, v_ref, sg_ref, dmask_ref, xi_ref, zeta_ref, decay_ref, og_ref,
                state_ref, *, heads, qk_dim, v_dim):
    @pl.when(pl.program_id(1) == 0)
    def _():
        state_ref[...] = jnp.zeros(state_ref.shape, F32)

    for c in range(q_ref.shape[0] // R_CHUNK):
        r = slice(c * R_CHUNK, (c + 1) * R_CHUNK)
        for h in range(heads):
            qh = q_ref[r, h * qk_dim:(h + 1) * qk_dim]
            kh = k_ref[r, h * qk_dim:(h + 1) * qk_dim]
            vh = v_ref[r, h * v_dim:(h + 1) * v_dim]
            s = _dot_nt(qh, kh) * dmask_ref[h]
            o = _dot(s.astype(BF16), vh) + _dot(qh, state_ref[h].astype(BF16)) * xi_ref[h]
            kz = (kh.astype(F32) * zeta_ref[h]).astype(BF16)
            state_ref[h] = state_ref[h] * decay_ref[h] + _dot_tn(kz, vh)
            mu = jnp.mean(o, axis=-1, keepdims=True)
            d = o - mu
            var = jnp.mean(d * d, axis=-1, keepdims=True)
            on = d * lax.rsqrt(var + NORM_EPS)
            og_ref[r, h * v_dim:(h + 1) * v_dim] = (
                on * sg_ref[r, h * v_dim:(h + 1) * v_dim].astype(F32)).astype(BF16)


def _out_kernel(og_ref, h_ref, p_ref, wout_ref, gple_ref, wpg_ref, wple_ref, gfin_ref, o_ref):
    h = h_ref[...] + _dot(og_ref[...], wout_ref[...])
    h = _ple(h, p_ref[...], gple_ref[...], wpg_ref[...], wple_ref[...])
    o_ref[...] = _rms(h, gfin_ref[...])


def _params(*sem):
    return pltpu.CompilerParams(dimension_semantics=sem, vmem_limit_bytes=VMEM_LIMIT)


def kernel(x, p, g_pre, a_w_in, a_g_q, a_g_kv, a_w_q_up, a_w_idx_q, a_g_ik, a_b_ik, a_w_uk, a_w_uv,
           a_w_out, r_w_in, r_w_out, w_ple_gate, g_ple, w_ple, g_final):
    B, L, D = x.shape
    N = B * L
    q_rank = a_g_q.shape[-1]
    kv_rank = a_g_kv.shape[-1]
    idx_dim = a_g_ik.shape[-1]
    heads, _, qk_dim = a_w_uk.shape[1:]
    v_dim = a_w_uv.shape[-1]
    idx_heads = a_w_idx_q.shape[-1] // idx_dim
    a_width = heads * v_dim
    ple_dim = p.shape[-1]
    assert idx_heads == heads and 2 * idx_dim == LANES and L % KEY_CHUNK == 0
    assert KEY_CHUNK % Q_BLOCK == 0 and L % Q_BLOCK == 0 and heads % HEAD_GROUPS == 0
    assert q_rank % LANES == 0 and kv_rank % LANES == 0 and a_w_in.shape[0] == 1 and r_w_in.shape[0] == 1
    topk = min(TOPK_MAX, L // 4)
    x2 = x.reshape(N, D)
    p_layers = p.reshape(p.shape[0], N, ple_dim)
    row = lambda a: a.reshape(1, -1).astype(F32)

    w_in = a_w_in[0]
    o_ki = q_rank + kv_rank
    o_g = o_ki + idx_dim + idx_heads
    w_ki = jnp.pad(w_in[:, o_ki:o_g], ((0, 0), (0, LANES - (o_g - o_ki))))
    w0 = jnp.concatenate([w_in[:, :o_ki], w_ki, w_in[:, o_g:]], axis=1).astype(BF16)
    pad_k = lambda a: jnp.pad(a.reshape(1, -1).astype(F32), ((0, 0), (0, LANES - idx_dim)))
    wukt = jnp.swapaxes(a_w_uk[0], 1, 2).astype(BF16)
    tm0 = 512
    n0 = w0.shape[1]
    outs0 = pl.pallas_call(
        functools.partial(_proj0_kernel, q_rank=q_rank, kv_rank=kv_rank, idx_dim=idx_dim, heads=heads,
                          qk_dim=qk_dim, qk_scale=qk_dim ** -0.5 * math.log2(math.e),
                          widx_scale=idx_heads ** -0.5 * idx_dim ** -0.5),
        grid=(N // tm0,),
        in_specs=[pl.BlockSpec((tm0, D), lambda i: (i, 0)), _const_spec((1, D)), _const_spec((D, n0)),
                  _const_spec((1, q_rank)), _const_spec((1, kv_rank)), _const_spec((1, LANES)),
                  _const_spec((1, LANES)), _const_spec((q_rank, heads * qk_dim)),
                  _const_spec((heads, qk_dim, kv_rank)), _const_spec((q_rank, idx_heads * idx_dim))],
        out_specs=[pl.BlockSpec((heads, tm0, kv_rank), lambda i: (0, i, 0)),
                   pl.BlockSpec((tm0, idx_heads * idx_dim), lambda i: (i, 0)),
                   pl.BlockSpec((tm0, LANES), lambda i: (i, 0)),
                   pl.BlockSpec((tm0, LANES), lambda i: (i, 0)),
                   pl.BlockSpec((tm0, LANES), lambda i: (i, 0)),
                   pl.BlockSpec((tm0, kv_rank), lambda i: (i, 0)),
                   pl.BlockSpec((tm0, a_width), lambda i: (i, 0))],
        out_shape=[jax.ShapeDtypeStruct((heads, N, kv_rank), BF16),
                   jax.ShapeDtypeStruct((N, idx_heads * idx_dim), BF16),
                   jax.ShapeDtypeStruct((N, LANES), BF16),
                   jax.ShapeDtypeStruct((N, LANES), BF16),
                   jax.ShapeDtypeStruct((N, LANES), F32),
                   jax.ShapeDtypeStruct((N, kv_rank), BF16),
                   jax.ShapeDtypeStruct((N, a_width), BF16)],
        compiler_params=_params("parallel"),
        name="dsa_proj",
    )(x2, row(g_pre[0]), w0, row(a_g_q[0]), row(a_g_kv[0]), pad_k(a_g_ik[0]), pad_k(a_b_ik[0]),
      a_w_q_up[0].astype(BF16), wukt, a_w_idx_q[0].astype(BF16))
    qabs, qidx, ka, kb, wi, ckv, sg0 = outs0

    nb = L // Q_BLOCK
    nkc = L // KEY_CHUNK
    b3 = lambda a: a.reshape(B, L, a.shape[-1])
    qblk = lambda w: pl.BlockSpec((None, Q_BLOCK, w), lambda b, i: (b, i, 0))
    seq = lambda w: pl.BlockSpec((None, L, w), lambda b, i: (b, 0, 0))
    bias = pl.pallas_call(
        functools.partial(_select_kernel, topk=topk, heads=heads, idx_dim=idx_dim, max_chunks=nkc),
        grid=(B, nb),
        in_specs=[qblk(idx_heads * idx_dim), seq(LANES), seq(LANES), qblk(LANES)],
        out_specs=qblk(L),
        out_shape=jax.ShapeDtypeStruct((B, L, L), BF16),
        scratch_shapes=[pltpu.VMEM((L, Q_BLOCK), F32)],
        compiler_params=_params("parallel", "arbitrary"),
        name="dsa_select",
    )(b3(qidx), b3(ka), b3(kb), b3(wi))
    og0 = pl.pallas_call(
        functools.partial(_attend_kernel, heads=heads, v_dim=v_dim, max_chunks=nkc),
        grid=(B, nb),
        in_specs=[pl.BlockSpec((heads, None, Q_BLOCK, kv_rank), lambda b, i: (0, b, i, 0)),
                  seq(kv_rank), qblk(L), qblk(a_width), _const_spec((heads, kv_rank, v_dim))],
        out_specs=qblk(a_width),
        out_shape=jax.ShapeDtypeStruct((B, L, a_width), BF16),
        scratch_shapes=[pltpu.VMEM((heads * Q_BLOCK, L), F32),
                        pltpu.VMEM((heads * Q_BLOCK, L), BF16),
                        pltpu.VMEM((heads * Q_BLOCK, LANES), F32),
                        pltpu.VMEM((heads * Q_BLOCK, kv_rank), F32),
                        pltpu.VMEM((heads * Q_BLOCK, LANES), F32)],
        compiler_params=_params("parallel", "arbitrary"),
        name="dsa_attend",
    )(qabs.reshape(heads, B, L, kv_rank), b3(ckv), bias, b3(sg0), a_w_uv[0].astype(BF16))

    r_heads = 4
    r_qk = D // r_heads
    r_v = 2 * D // r_heads
    r_width = r_heads * r_v
    assert r_w_in.shape[-1] == 2 * D + 2 * r_width and r_qk % LANES == 0 and L % R_CHUNK == 0
    wr = r_w_in[0].astype(BF16)
    wq, wk, wv, wg = wr[:, :D], wr[:, D:2 * D], wr[:, 2 * D:2 * D + r_width], wr[:, 2 * D + r_width:]
    angle = 1.0 / (10000.0 ** np.linspace(0.0, 1.0, r_qk // 2))
    theta = np.arange(L, dtype=np.float64)[:, None] * np.repeat(angle, 2)[None, :]
    even = (np.arange(r_qk) % 2 == 0)[None, :]
    cos_t = jnp.asarray(np.cos(theta), F32)
    sin_next = jnp.asarray(np.where(even, -np.sin(theta), 0.0), F32)
    sin_prev = jnp.asarray(np.where(even, 0.0, np.sin(theta)), F32)
    tm1 = 256
    tok = lambda w: pl.BlockSpec((tm1, w), lambda b, i: (b * (L // tm1) + i, 0))
    tab = pl.BlockSpec((tm1, r_qk), lambda b, i: (i, 0))
    h1, rq, rk, rv, sg1 = pl.pallas_call(
        functools.partial(_mid_kernel, k_scale=r_qk ** -0.5, qk_dim=r_qk),
        grid=(B, L // tm1),
        in_specs=[tok(a_width), tok(D),
                  pl.BlockSpec((None, tm1, ple_dim), lambda b, i: (0, b * (L // tm1) + i, 0)),
                  _const_spec((a_width, D)), _const_spec((1, D)),
                  _const_spec((D, D)), _const_spec((ple_dim, D)), _const_spec((1, D)),
                  _const_spec((D, D)), _const_spec((D, D)), _const_spec((D, r_width)),
                  _const_spec((D, r_width)), tab, tab, tab],
        out_specs=[tok(D), tok(D), tok(D), tok(r_width), tok(r_width)],
        out_shape=[jax.ShapeDtypeStruct((N, D), F32), jax.ShapeDtypeStruct((N, D), BF16),
                   jax.ShapeDtypeStruct((N, D), BF16), jax.ShapeDtypeStruct((N, r_width), BF16),
                   jax.ShapeDtypeStruct((N, r_width), BF16)],
        compiler_params=_params("parallel", "arbitrary"),
        name="mid_proj",
    )(og0.reshape(N, a_width), x2, p_layers, a_w_out[0].astype(BF16), row(g_ple[0]),
      w_ple_gate[0].astype(BF16), w_ple[0].astype(BF16), row(g_pre[1]), wq, wk, wv, wg,
      cos_t, sin_next, sin_prev)

    log_gamma = np.log(1.0 - 2.0 ** (-5.0 - np.arange(r_heads, dtype=np.float64)))
    ci = np.arange(R_CHUNK, dtype=np.float64)
    diff = ci[:, None] - ci[None, :]
    dmask = jnp.asarray(np.where(diff[None] >= 0, np.exp(diff[None] * log_gamma[:, None, None]), 0.0), F32)
    xi = np.exp((ci[None, :] + 1.0) * log_gamma[:, None])
    zeta = np.exp((R_CHUNK - 1.0 - ci[None, :]) * log_gamma[:, None])
    decay = np.exp(R_CHUNK * log_gamma)
    xi_b = jnp.asarray(np.broadcast_to(xi[:, :, None], (r_heads, R_CHUNK, r_v)), F32)
    zeta_b = jnp.asarray(np.broadcast_to(zeta[:, :, None], (r_heads, R_CHUNK, r_qk)), F32)
    decay_b = jnp.asarray(np.broadcast_to(decay[:, None, None], (r_heads, 1, r_v)), F32)
    rows_step = R_CHUNK * math.gcd(R_STEP_CHUNKS, L // R_CHUNK)
    cblk = lambda w: pl.BlockSpec((None, rows_step, w), lambda b, c: (b, c, 0))
    og1 = pl.pallas_call(
        functools.partial(_ret_kernel, heads=r_heads, qk_dim=r_qk, v_dim=r_v),
        grid=(B, L // rows_step),
        in_specs=[cblk(D), cblk(D), cblk(r_width), cblk(r_width),
                  _const_spec((r_heads, R_CHUNK, R_CHUNK)), _const_spec((r_heads, R_CHUNK, r_v)),
                  _const_spec((r_heads, R_CHUNK, r_qk)), _const_spec((r_heads, 1, r_v))],
        out_specs=cblk(r_width),
        out_shape=jax.ShapeDtypeStruct((B, L, r_width), BF16),
        scratch_shapes=[pltpu.VMEM((r_heads, r_qk, r_v), F32)],
        compiler_params=_params("parallel", "arbitrary"),
        name="retention",
    )(rq.reshape(B, L, D), rk.reshape(B, L, D), rv.reshape(B, L, r_width), sg1.reshape(B, L, r_width),
      dmask, xi_b, zeta_b, decay_b)

    tm2 = 512
    tk2 = lambda w: pl.BlockSpec((tm2, w), lambda i: (i, 0))
    out = pl.pallas_call(
        _out_kernel,
        grid=(N // tm2,),
        in_specs=[tk2(r_width), tk2(D), pl.BlockSpec((None, tm2, ple_dim), lambda i: (1, i, 0)),
                  _const_spec((r_width, D)), _const_spec((1, D)),
                  _const_spec((D, D)), _const_spec((ple_dim, D)), _const_spec((1, D))],
        out_specs=tk2(D),
        out_shape=jax.ShapeDtypeStruct((N, D), F32),
        compiler_params=_params("parallel"),
        name="out_proj",
    )(og1.reshape(N, r_width), h1, p_layers, r_w_out[0].astype(BF16), row(g_ple[1]),
      w_ple_gate[1].astype(BF16), w_ple[1].astype(BF16), row(g_final))
    return out.reshape(B, L, D)
```

```python
import functools
import math

import jax
import jax.numpy as jnp
import numpy as np
from jax import lax
from jax.experimental import pallas as pl
from jax.experimental.pallas import tpu as pltpu

F32 = jnp.float32
BF16 = jnp.bfloat16

NORM_EPS = 1e-6
TOPK_MAX = 256
Q_BLOCK = 256
KEY_CHUNK = 256
R_CHUNK = 128
LANES = 128
MASK_BIAS = -1e30
VALUE_BISECT_MAX = 34
VMEM_LIMIT = 56 * 1024 * 1024
HEAD_GROUPS = 4
R_STEP_CHUNKS = 8


def _passes_per_body(n_chunks):
    return max(1, min(4, round(6 / n_chunks)))


def _rms(x, g):
    return x * lax.rsqrt(jnp.mean(x * x, axis=-1, keepdims=True) + NORM_EPS) * g


def _dot(a, b):
    return jnp.dot(a, b, preferred_element_type=F32)


def _dot_nt(a, b):
    return lax.dot_general(a, b, (((1,), (1,)), ((), ())), preferred_element_type=F32)


def _dot_tn(a, b):
    return lax.dot_general(a, b, (((0,), (0,)), ((), ())), preferred_element_type=F32)


def _const_spec(shape):
    nd = len(shape)
    return pl.BlockSpec(shape, lambda *_: (0,) * nd, pipeline_mode=pl.Buffered(1))


def _proj0_kernel(x_ref, gpre_ref, w0_ref, gq_ref, gkv_ref, gik_ref, bik_ref, wqup_ref, wukt_ref,
                  widxq_ref, qabs_ref, qidx_ref, ka_ref, kb_ref, wi_ref, ckv_ref, sg_ref,
                  *, q_rank, kv_rank, idx_dim, heads, qk_dim, qk_scale, widx_scale):
    hn = _rms(x_ref[...], gpre_ref[...])
    z = _dot(hn.astype(BF16), w0_ref[...])
    o_kv = q_rank
    o_ki = q_rank + kv_rank
    o_g = o_ki + LANES
    cq = _rms(z[:, :o_kv], gq_ref[...]).astype(BF16)
    ckv_ref[...] = _rms(z[:, o_kv:o_ki], gkv_ref[...]).astype(BF16)
    ki = z[:, o_ki:o_g]
    lane = lax.broadcasted_iota(jnp.int32, (1, LANES), 1)
    in_k = lane < idx_dim
    mu = jnp.sum(jnp.where(in_k, ki, 0.0), axis=-1, keepdims=True) * (1.0 / idx_dim)
    d = jnp.where(in_k, ki - mu, 0.0)
    var = jnp.sum(d * d, axis=-1, keepdims=True) * (1.0 / idx_dim)
    kn = d * lax.rsqrt(var + NORM_EPS) * gik_ref[...] + bik_ref[...]
    ka_ref[...] = kn.astype(BF16)
    kb_ref[...] = pltpu.roll(kn, idx_dim, 1).astype(BF16)
    wi_ref[...] = ki * widx_scale
    g = z[:, o_g:]
    sg_ref[...] = (g * jax.nn.sigmoid(g)).astype(BF16)
    q = _dot(cq, wqup_ref[...])
    for h in range(heads):
        qh = q[:, h * qk_dim:(h + 1) * qk_dim].astype(BF16)
        qabs_ref[h] = (_dot(qh, wukt_ref[h]) * qk_scale).astype(BF16)
    qidx_ref[...] = _dot(cq, widxq_ref[...]).astype(BF16)


def _select_kernel(qidx_ref, ka_ref, kb_ref, wi_ref, bias_ref, st_ref, *, topk, heads, idx_dim, max_chunks):
    qb = pl.program_id(1)
    n_kc = (qb * Q_BLOCK) // KEY_CHUNK + 1
    kf = float(topk)
    pairs = heads // 2

    qi = qidx_ref[...]
    qst = jnp.concatenate([qi[:, j * LANES:(j + 1) * LANES] for j in range(pairs)], axis=0)
    wit = wi_ref[...].T
    w_rows = [wit[idx_dim + h:idx_dim + h + 1, :] for h in range(heads)]
    tq = qb * Q_BLOCK + lax.broadcasted_iota(jnp.int32, (1, Q_BLOCK), 1)
    n_causal = (tq + 1).astype(F32)
    small = n_causal <= kf
    row_i = lax.broadcasted_iota(jnp.int32, (KEY_CHUNK, KEY_CHUNK), 0)
    col_i = lax.broadcasted_iota(jnp.int32, (KEY_CHUNK, KEY_CHUNK), 1)
    eye = (row_i == col_i).astype(BF16)[:Q_BLOCK, :Q_BLOCK]
    lower_tri = (col_i <= row_i).astype(BF16)

    def fold8(x, op):
        return op(x.reshape(x.shape[0] // 8, 8, Q_BLOCK), axis=0)

    def variant(nc):
        chunks = [(c, c * KEY_CHUNK) for c in range(nc)]

        mn = mx = cgt = cge = None
        for c, k0 in chunks:
            sa = _dot_nt(ka_ref[k0:k0 + KEY_CHUNK, :], qst)
            sb = _dot_nt(kb_ref[k0:k0 + KEY_CHUNK, :], qst)
            acc = None
            for j in range(pairs):
                ta = jnp.maximum(sa[:, j * Q_BLOCK:(j + 1) * Q_BLOCK], 0.0) * w_rows[2 * j]
                tb = jnp.maximum(sb[:, j * Q_BLOCK:(j + 1) * Q_BLOCK], 0.0) * w_rows[2 * j + 1]
                acc = ta + tb if acc is None else acc + ta + tb
            if c == nc - 1:
                kpos = k0 + lax.broadcasted_iota(jnp.int32, (KEY_CHUNK, Q_BLOCK), 0)
                causal = kpos <= tq
                s = jnp.where(causal, acc, -jnp.inf)
                s_min = jnp.where(causal, acc, jnp.inf)
            else:
                s = s_min = acc
            st_ref[k0:k0 + KEY_CHUNK, :] = s
            parts = (fold8(s_min, jnp.min), fold8(s, jnp.max),
                     fold8(jnp.where(s > 0.0, 1.0, 0.0), jnp.sum),
                     fold8(jnp.where(s >= 0.0, 1.0, 0.0), jnp.sum))
            if mn is None:
                mn, mx, cgt, cge = parts
            else:
                mn, mx = jnp.minimum(mn, parts[0]), jnp.maximum(mx, parts[1])
                cgt, cge = cgt + parts[2], cge + parts[3]
        mn = jnp.min(mn, axis=0, keepdims=True)
        mx = jnp.max(mx, axis=0, keepdims=True)
        cgt0 = jnp.sum(cgt, axis=0, keepdims=True)
        cge0 = jnp.sum(cge, axis=0, keepdims=True)

        def count_gt(ref, thr):
            acc = None
            for _, k0 in chunks:
                part = fold8(jnp.where(ref[k0:k0 + KEY_CHUNK, :] > thr, 1.0, 0.0), jnp.sum)
                acc = part if acc is None else acc + part
            return jnp.sum(acc, axis=0, keepdims=True)

        pos = cgt0 >= kf
        zero = jnp.logical_and(jnp.logical_not(pos), cge0 >= kf)
        zero = jnp.logical_and(zero, jnp.logical_not(small))
        lo = jnp.where(pos, 0.0, jnp.where(zero, 0.0, mn * 1.0001))
        hi = jnp.where(pos, mx, 0.0)
        c_lo = jnp.where(pos, cgt0, jnp.where(zero, cgt0, n_causal))
        c_hi = jnp.where(pos, 0.0, cgt0)
        lo = jnp.where(small, -jnp.inf, lo)
        hi = jnp.where(small, -jnp.inf, hi)
        searching = jnp.logical_not(jnp.logical_or(small, zero))

        def unresolved(c_lo):
            bad = jnp.logical_and(c_lo != kf, searching)
            return (jnp.max(jnp.where(bad, 1.0, 0.0)) > 0.0).astype(jnp.int32)

        def bis_cond(carry):
            return jnp.logical_and(carry[0] < VALUE_BISECT_MAX, carry[1] > 0)

        def bisect(state):
            lo, hi, c_lo, c_hi = state
            mid = 0.5 * (lo + hi)
            c = count_gt(st_ref, mid)
            ge = c >= kf
            return (jnp.where(ge, mid, lo), jnp.where(ge, hi, mid),
                    jnp.where(ge, c, c_lo), jnp.where(ge, c_hi, c))

        def bis_body(carry):
            it, _, state = carry
            flag = unresolved(state[2])
            for _ in range(_passes_per_body(nc)):
                state = bisect(state)
            return it + _passes_per_body(nc), flag, state

        _, _, (lo, hi, c_lo, c_hi) = lax.while_loop(
            bis_cond, bis_body, (jnp.int32(0), jnp.int32(1), (lo, hi, c_lo, c_hi)))
        done = jnp.logical_or(c_lo == kf, small)
        thr = jnp.where(done, lo, hi)

        @pl.when(jnp.max(jnp.where(done, 0.0, 1.0)) > 0.0)
        def _():
            need = jnp.where(done, 0.0, kf - c_hi)
            tied_f = jnp.where(jnp.logical_or(done, zero), 0.0, 1.0)
            zero_f = jnp.where(zero, 1.0, 0.0)
            before = jnp.zeros((1, Q_BLOCK), F32)
            for _, k0 in chunks:
                s = st_ref[k0:k0 + KEY_CHUNK, :]
                band_f = (jnp.where(s > lo, 1.0, 0.0) * jnp.where(s <= hi, 1.0, 0.0) * tied_f
                          + jnp.where(s == 0.0, 1.0, 0.0) * zero_f)
                rank = _dot(lower_tri, band_f.astype(BF16)) + before
                take_f = band_f * jnp.where(rank <= need, 1.0, 0.0)
                st_ref[k0:k0 + KEY_CHUNK, :] = jnp.where(take_f > 0.5, jnp.inf, s)
                before = before + jnp.sum(fold8(band_f, jnp.sum), axis=0, keepdims=True)

        for _, k0 in chunks:
            sel_t = jnp.where(st_ref[k0:k0 + KEY_CHUNK, :] > thr, 1.0, 0.0).astype(BF16)
            sel = _dot_nt(eye, sel_t)
            bias_ref[:, k0:k0 + KEY_CHUNK] = ((sel - 1.0) * (-MASK_BIAS)).astype(BF16)
        if nc < max_chunks:
            bias_ref[:, nc * KEY_CHUNK:] = jnp.full((Q_BLOCK, (max_chunks - nc) * KEY_CHUNK), MASK_BIAS, BF16)

    for nc in range(1, max_chunks + 1):
        pl.when(n_kc == nc)(functools.partial(variant, nc))


def _attend_kernel(qabs_ref, ckv_ref, bias_ref, sg_ref, wuv_ref, og_ref, s_ref, p_ref, m_ref, acc_ref,
                   l_ref, *, heads, v_dim, max_chunks):
    n_kc = (pl.program_id(1) * Q_BLOCK) // KEY_CHUNK + 1
    sub = KEY_CHUNK // LANES
    gsz = heads // HEAD_GROUPS
    grows = gsz * Q_BLOCK

    def variant(nc):
        nk = nc * KEY_CHUNK
        q_all = jnp.concatenate([qabs_ref[h] for h in range(heads)], axis=0)
        for c in range(nc):
            k0 = c * KEY_CHUNK
            s = _dot_nt(q_all, ckv_ref[k0:k0 + KEY_CHUNK, :])
            bias = bias_ref[:, k0:k0 + KEY_CHUNK].astype(F32)
            for h in range(heads):
                rows = slice(h * Q_BLOCK, (h + 1) * Q_BLOCK)
                sh = s[rows, :] + bias
                s_ref[rows, k0:k0 + KEY_CHUNK] = sh
                mh = sh[:, :LANES]
                for t in range(1, sub):
                    mh = jnp.maximum(mh, sh[:, t * LANES:(t + 1) * LANES])
                m_ref[rows, :] = mh if c == 0 else jnp.maximum(m_ref[rows, :], mh)
        for g in range(HEAD_GROUPS):
            for i in range(gsz):
                rows = slice((g * gsz + i) * Q_BLOCK, (g * gsz + i + 1) * Q_BLOCK)
                m = jnp.broadcast_to(jnp.max(m_ref[rows, :], axis=-1, keepdims=True), (Q_BLOCK, LANES))
                lsum = None
                for t in range(nc * sub):
                    pt = jnp.exp2(s_ref[rows, t * LANES:(t + 1) * LANES] - m)
                    p_ref[rows, t * LANES:(t + 1) * LANES] = pt.astype(BF16)
                    lsum = pt if lsum is None else lsum + pt
                l_ref[rows, :] = lsum
            acc_ref[g * grows:(g + 1) * grows, :] = _dot(p_ref[g * grows:(g + 1) * grows, :nk], ckv_ref[:nk, :])

    for nc in range(1, max_chunks + 1):
        pl.when(n_kc == nc)(functools.partial(variant, nc))

    inv_l = 1.0 / jnp.sum(l_ref[...], axis=-1, keepdims=True)
    for h in range(heads):
        oh = (acc_ref[h * Q_BLOCK:(h + 1) * Q_BLOCK, :] * inv_l[h * Q_BLOCK:(h + 1) * Q_BLOCK]).astype(BF16)
        ov = _dot(oh, wuv_ref[h])
        og_ref[:, h * v_dim:(h + 1) * v_dim] = (
            ov * sg_ref[:, h * v_dim:(h + 1) * v_dim].astype(F32)).astype(BF16)


def _ple(h, p, gple, wpg, wple):
    gate = jax.nn.sigmoid(_dot(_rms(h, gple).astype(BF16), wpg))
    return h + _dot(p.astype(BF16), wple) * gate


def _rotate(x, cos, sin_next, sin_prev):
    nxt = pltpu.roll(x, LANES - 1, 1)
    prv = pltpu.roll(x, 1, 1)
    return x * cos + nxt * sin_next + prv * sin_prev


def _mid_kernel(og_ref, x_ref, p_ref, wout_ref, gple_ref, wpg_ref, wple_ref, gpre_ref,
                wq_ref, wk_ref, wv_ref, wg_ref, cos_ref, sn_ref, sp_ref,
                h_ref, q_ref, k_ref, v_ref, sg_ref, *, k_scale, qk_dim):
    h = x_ref[...] + _dot(og_ref[...], wout_ref[...])
    h = _ple(h, p_ref[...], gple_ref[...], wpg_ref[...], wple_ref[...])
    h_ref[...] = h
    hb = _rms(h, gpre_ref[...]).astype(BF16)
    q = _dot(hb, wq_ref[...])
    k = _dot(hb, wk_ref[...]) * k_scale
    for j in range(q.shape[1] // LANES):
        t = (j * LANES) % qk_dim
        cs = cos_ref[:, t:t + LANES]
        sn = sn_ref[:, t:t + LANES]
        sp = sp_ref[:, t:t + LANES]
        q_ref[:, j * LANES:(j + 1) * LANES] = _rotate(q[:, j * LANES:(j + 1) * LANES], cs, sn, sp).astype(BF16)
        k_ref[:, j * LANES:(j + 1) * LANES] = _rotate(k[:, j * LANES:(j + 1) * LANES], cs, sn, sp).astype(BF16)
    v_ref[...] = _dot(hb, wv_ref[...]).astype(BF16)
    g = _dot(hb, wg_ref[...])
    sg_ref[...] = (g * jax.nn.sigmoid(g)).astype(BF16)


def _ret_kernel(q_ref, k_ref, v_ref, sg_ref, dmask_ref, xi_ref, zeta_ref, decay_ref, og_ref,
                state_ref, *, heads, qk_dim, v_dim):
    @pl.when(pl.program_id(1) == 0)
    def _():
        state_ref[...] = jnp.zeros(state_ref.shape, F32)

    for c in range(q_ref.shape[0] // R_CHUNK):
        r = slice(c * R_CHUNK, (c + 1) * R_CHUNK)
        for h in range(heads):
            qh = q_ref[r, h * qk_dim:(h + 1) * qk_dim]
            kh = k_ref[r, h * qk_dim:(h + 1) * qk_dim]
            vh = v_ref[r, h * v_dim:(h + 1) * v_dim]
            s = _dot_nt(qh, kh) * dmask_ref[h]
            o = _dot(s.astype(BF16), vh) + _dot(qh, state_ref[h].astype(BF16)) * xi_ref[h]
            kz = (kh.astype(F32) * zeta_ref[h]).astype(BF16)
            state_ref[h] = state_ref[h] * decay_ref[h] + _dot_tn(kz, vh)
            mu = jnp.mean(o, axis=-1, keepdims=True)
            d = o - mu
            var = jnp.mean(d * d, axis=-1, keepdims=True)
            on = d * lax.rsqrt(var + NORM_EPS)
            og_ref[r, h * v_dim:(h + 1) * v_dim] = (
                on * sg_ref[r, h * v_dim:(h + 1) * v_dim].astype(F32)).astype(BF16)


def _out_kernel(og_ref, h_ref, p_ref, wout_ref, gple_ref, wpg_ref, wple_ref, gfin_ref, o_ref):
    h = h_ref[...] + _dot(og_ref[...], wout_ref[...])
    h = _ple(h, p_ref[...], gple_ref[...], wpg_ref[...], wple_ref[...])
    o_ref[...] = _rms(h, gfin_ref[...])


def _params(*sem):
    return pltpu.CompilerParams(dimension_semantics=sem, vmem_limit_bytes=VMEM_LIMIT)


def kernel(x, p, g_pre, a_w_in, a_g_q, a_g_kv, a_w_q_up, a_w_idx_q, a_g_ik, a_b_ik, a_w_uk, a_w_uv,
           a_w_out, r_w_in, r_w_out, w_ple_gate, g_ple, w_ple, g_final):
    B, L, D = x.shape
    N = B * L
    q_rank = a_g_q.shape[-1]
    kv_rank = a_g_kv.shape[-1]
    idx_dim = a_g_ik.shape[-1]
    heads, _, qk_dim = a_w_uk.shape[1:]
    v_dim = a_w_uv.shape[-1]
    idx_heads = a_w_idx_q.shape[-1] // idx_dim
    a_width = heads * v_dim
    ple_dim = p.shape[-1]
    assert idx_heads == heads and 2 * idx_dim == LANES and L % KEY_CHUNK == 0
    assert KEY_CHUNK % Q_BLOCK == 0 and L % Q_BLOCK == 0 and heads % HEAD_GROUPS == 0
    assert q_rank % LANES == 0 and kv_rank % LANES == 0 and a_w_in.shape[0] == 1 and r_w_in.shape[0] == 1
    topk = min(TOPK_MAX, L // 4)
    x2 = x.reshape(N, D)
    p_layers = p.reshape(p.shape[0], N, ple_dim)
    row = lambda a: a.reshape(1, -1).astype(F32)

    w_in = a_w_in[0]
    o_ki = q_rank + kv_rank
    o_g = o_ki + idx_dim + idx_heads
    w_ki = jnp.pad(w_in[:, o_ki:o_g], ((0, 0), (0, LANES - (o_g - o_ki))))
    w0 = jnp.concatenate([w_in[:, :o_ki], w_ki, w_in[:, o_g:]], axis=1).astype(BF16)
    pad_k = lambda a: jnp.pad(a.reshape(1, -1).astype(F32), ((0, 0), (0, LANES - idx_dim)))
    wukt = jnp.swapaxes(a_w_uk[0], 1, 2).astype(BF16)
    tm0 = 1024
    n0 = w0.shape[1]
    outs0 = pl.pallas_call(
        functools.partial(_proj0_kernel, q_rank=q_rank, kv_rank=kv_rank, idx_dim=idx_dim, heads=heads,
                          qk_dim=qk_dim, qk_scale=qk_dim ** -0.5 * math.log2(math.e),
                          widx_scale=idx_heads ** -0.5 * idx_dim ** -0.5),
        grid=(N // tm0,),
        in_specs=[pl.BlockSpec((tm0, D), lambda i: (i, 0)), _const_spec((1, D)), _const_spec((D, n0)),
                  _const_spec((1, q_rank)), _const_spec((1, kv_rank)), _const_spec((1, LANES)),
                  _const_spec((1, LANES)), _const_spec((q_rank, heads * qk_dim)),
                  _const_spec((heads, qk_dim, kv_rank)), _const_spec((q_rank, idx_heads * idx_dim))],
        out_specs=[pl.BlockSpec((heads, tm0, kv_rank), lambda i: (0, i, 0)),
                   pl.BlockSpec((tm0, idx_heads * idx_dim), lambda i: (i, 0)),
                   pl.BlockSpec((tm0, LANES), lambda i: (i, 0)),
                   pl.BlockSpec((tm0, LANES), lambda i: (i, 0)),
                   pl.BlockSpec((tm0, LANES), lambda i: (i, 0)),
                   pl.BlockSpec((tm0, kv_rank), lambda i: (i, 0)),
                   pl.BlockSpec((tm0, a_width), lambda i: (i, 0))],
        out_shape=[jax.ShapeDtypeStruct((heads, N, kv_rank), BF16),
                   jax.ShapeDtypeStruct((N, idx_heads * idx_dim), BF16),
                   jax.ShapeDtypeStruct((N, LANES), BF16),
                   jax.ShapeDtypeStruct((N, LANES), BF16),
                   jax.ShapeDtypeStruct((N, LANES), F32),
                   jax.ShapeDtypeStruct((N, kv_rank), BF16),
                   jax.ShapeDtypeStruct((N, a_width), BF16)],
        compiler_params=_params("parallel"),
        name="dsa_proj",
    )(x2, row(g_pre[0]), w0, row(a_g_q[0]), row(a_g_kv[0]), pad_k(a_g_ik[0]), pad_k(a_b_ik[0]),
      a_w_q_up[0].astype(BF16), wukt, a_w_idx_q[0].astype(BF16))
    qabs, qidx, ka, kb, wi, ckv, sg0 = outs0

    nb = L // Q_BLOCK
    nkc = L // KEY_CHUNK
    b3 = lambda a: a.reshape(B, L, a.shape[-1])
    qblk = lambda w: pl.BlockSpec((None, Q_BLOCK, w), lambda b, i: (b, i, 0))
    seq = lambda w: pl.BlockSpec((None, L, w), lambda b, i: (b, 0, 0))
    bias = pl.pallas_call(
        functools.partial(_select_kernel, topk=topk, heads=heads, idx_dim=idx_dim, max_chunks=nkc),
        grid=(B, nb),
        in_specs=[qblk(idx_heads * idx_dim), seq(LANES), seq(LANES), qblk(LANES)],
        out_specs=qblk(L),
        out_shape=jax.ShapeDtypeStruct((B, L, L), BF16),
        scratch_shapes=[pltpu.VMEM((L, Q_BLOCK), F32)],
        compiler_params=_params("parallel", "arbitrary"),
        name="dsa_select",
    )(b3(qidx), b3(ka), b3(kb), b3(wi))
    og0 = pl.pallas_call(
        functools.partial(_attend_kernel, heads=heads, v_dim=v_dim, max_chunks=nkc),
        grid=(B, nb),
        in_specs=[pl.BlockSpec((heads, None, Q_BLOCK, kv_rank), lambda b, i: (0, b, i, 0)),
                  seq(kv_rank), qblk(L), qblk(a_width), _const_spec((heads, kv_rank, v_dim))],
        out_specs=qblk(a_width),
        out_shape=jax.ShapeDtypeStruct((B, L, a_width), BF16),
        scratch_shapes=[pltpu.VMEM((heads * Q_BLOCK, L), F32),
                        pltpu.VMEM((heads * Q_BLOCK, L), BF16),
                        pltpu.VMEM((heads * Q_BLOCK, LANES), F32),
                        pltpu.VMEM((heads * Q_BLOCK, kv_rank), F32),
                        pltpu.VMEM((heads * Q_BLOCK, LANES), F32)],
        compiler_params=_params("parallel", "arbitrary"),
        name="dsa_attend",
    )(qabs.reshape(heads, B, L, kv_rank), b3(ckv), bias, b3(sg0), a_w_uv[0].astype(BF16))

    r_heads = 4
    r_qk = D // r_heads
    r_v = 2 * D // r_heads
    r_width = r_heads * r_v
    assert r_w_in.shape[-1] == 2 * D + 2 * r_width and r_qk % LANES == 0 and L % R_CHUNK == 0
    wr = r_w_in[0].astype(BF16)
    wq, wk, wv, wg = wr[:, :D], wr[:, D:2 * D], wr[:, 2 * D:2 * D + r_width], wr[:, 2 * D + r_width:]
    angle = 1.0 / (10000.0 ** np.linspace(0.0, 1.0, r_qk // 2))
    theta = np.arange(L, dtype=np.float64)[:, None] * np.repeat(angle, 2)[None, :]
    even = (np.arange(r_qk) % 2 == 0)[None, :]
    cos_t = jnp.asarray(np.cos(theta), F32)
    sin_next = jnp.asarray(np.where(even, -np.sin(theta), 0.0), F32)
    sin_prev = jnp.asarray(np.where(even, 0.0, np.sin(theta)), F32)
    tm1 = 512
    tok = lambda w: pl.BlockSpec((tm1, w), lambda b, i: (b * (L // tm1) + i, 0))
    tab = pl.BlockSpec((tm1, r_qk), lambda b, i: (i, 0))
    h1, rq, rk, rv, sg1 = pl.pallas_call(
        functools.partial(_mid_kernel, k_scale=r_qk ** -0.5, qk_dim=r_qk),
        grid=(B, L // tm1),
        in_specs=[tok(a_width), tok(D),
                  pl.BlockSpec((None, tm1, ple_dim), lambda b, i: (0, b * (L // tm1) + i, 0)),
                  _const_spec((a_width, D)), _const_spec((1, D)),
                  _const_spec((D, D)), _const_spec((ple_dim, D)), _const_spec((1, D)),
                  _const_spec((D, D)), _const_spec((D, D)), _const_spec((D, r_width)),
                  _const_spec((D, r_width)), tab, tab, tab],
        out_specs=[tok(D), tok(D), tok(D), tok(r_width), tok(r_width)],
        out_shape=[jax.ShapeDtypeStruct((N, D), F32), jax.ShapeDtypeStruct((N, D), BF16),
                   jax.ShapeDtypeStruct((N, D), BF16), jax.ShapeDtypeStruct((N, r_width), BF16),
                   jax.ShapeDtypeStruct((N, r_width), BF16)],
        compiler_params=_params("parallel", "arbitrary"),
        name="mid_proj",
    )(og0.reshape(N, a_width), x2, p_layers, a_w_out[0].astype(BF16), row(g_ple[0]),
      w_ple_gate[0].astype(BF16), w_ple[0].astype(BF16), row(g_pre[1]), wq, wk, wv, wg,
      cos_t, sin_next, sin_prev)

    log_gamma = np.log(1.0 - 2.0 ** (-5.0 - np.arange(r_heads, dtype=np.float64)))
    ci = np.arange(R_CHUNK, dtype=np.float64)
    diff = ci[:, None] - ci[None, :]
    dmask = jnp.asarray(np.where(diff[None] >= 0, np.exp(diff[None] * log_gamma[:, None, None]), 0.0), F32)
    xi = np.exp((ci[None, :] + 1.0) * log_gamma[:, None])
    zeta = np.exp((R_CHUNK - 1.0 - ci[None, :]) * log_gamma[:, None])
    decay = np.exp(R_CHUNK * log_gamma)
    xi_b = jnp.asarray(np.broadcast_to(xi[:, :, None], (r_heads, R_CHUNK, r_v)), F32)
    zeta_b = jnp.asarray(np.broadcast_to(zeta[:, :, None], (r_heads, R_CHUNK, r_qk)), F32)
    decay_b = jnp.asarray(np.broadcast_to(decay[:, None, None], (r_heads, 1, r_v)), F32)
    rows_step = R_CHUNK * math.gcd(R_STEP_CHUNKS, L // R_CHUNK)
    cblk = lambda w: pl.BlockSpec((None, rows_step, w), lambda b, c: (b, c, 0))
    og1 = pl.pallas_call(
        functools.partial(_ret_kernel, heads=r_heads, qk_dim=r_qk, v_dim=r_v),
        grid=(B, L // rows_step),
        in_specs=[cblk(D), cblk(D), cblk(r_width), cblk(r_width),
                  _const_spec((r_heads, R_CHUNK, R_CHUNK)), _const_spec((r_heads, R_CHUNK, r_v)),
                  _const_spec((r_heads, R_CHUNK, r_qk)), _const_spec((r_heads, 1, r_v))],
        out_specs=cblk(r_width),
        out_shape=jax.ShapeDtypeStruct((B, L, r_width), BF16),
        scratch_shapes=[pltpu.VMEM((r_heads, r_qk, r_v), F32)],
        compiler_params=_params("parallel", "arbitrary"),
        name="retention",
    )(rq.reshape(B, L, D), rk.reshape(B, L, D), rv.reshape(B, L, r_width), sg1.reshape(B, L, r_width),
      dmask, xi_b, zeta_b, decay_b)

    tm2 = 1024
    tk2 = lambda w: pl.BlockSpec((tm2, w), lambda i: (i, 0))
    out = pl.pallas_call(
        _out_kernel,
        grid=(N // tm2,),
        in_specs=[tk2(r_width), tk2(D), pl.BlockSpec((None, tm2, ple_dim), lambda i: (1, i, 0)),
                  _const_spec((r_width, D)), _const_spec((1, D)),
                  _const_spec((D, D)), _const_spec((ple_dim, D)), _const_spec((1, D))],
        out_specs=tk2(D),
        out_shape=jax.ShapeDtypeStruct((N, D), F32),
        compiler_params=_params("parallel"),
        name="out_proj",
    )(og1.reshape(N, r_width), h1, p_layers, r_w_out[0].astype(BF16), row(g_ple[1]),
      w_ple_gate[1].astype(BF16), w_ple[1].astype(BF16), row(g_final))
    return out.reshape(B, L, D)
```

```python
import functools
import math

import jax
import jax.numpy as jnp
import numpy as np
from jax import lax
from jax.experimental import pallas as pl
from jax.experimental.pallas import tpu as pltpu

F32 = jnp.float32
BF16 = jnp.bfloat16

NORM_EPS = 1e-6
TOPK_MAX = 256
Q_BLOCK = 256
KEY_CHUNK = 256
R_CHUNK = 128
LANES = 128
MASK_BIAS = -1e30
VALUE_BISECT_MAX = 34
VMEM_LIMIT = 56 * 1024 * 1024
HEAD_GROUPS = 4
RET_HEADS = 4
R_STEP_CHUNKS = 8
ROWS_DSA_PROJ = 1024
ROWS_MID_PROJ = 512
ROWS_OUT_PROJ = 1024


def _passes_per_body(n_chunks):
    return max(1, min(4, round(6 / n_chunks)))


def _rms(x, g):
    return x * lax.rsqrt(jnp.mean(x * x, axis=-1, keepdims=True) + NORM_EPS) * g


def _dot(a, b):
    return jnp.dot(a, b, preferred_element_type=F32)


def _dot_nt(a, b):
    return lax.dot_general(a, b, (((1,), (1,)), ((), ())), preferred_element_type=F32)


def _dot_tn(a, b):
    return lax.dot_general(a, b, (((0,), (0,)), ((), ())), preferred_element_type=F32)


def _const_spec(shape):
    nd = len(shape)
    return pl.BlockSpec(shape, lambda *_: (0,) * nd, pipeline_mode=pl.Buffered(1))


def _proj0_kernel(x_ref, gpre_ref, w0_ref, gq_ref, gkv_ref, gik_ref, bik_ref, wqup_ref, wukt_ref,
                  widxq_ref, qabs_ref, qidx_ref, ka_ref, kb_ref, wi_ref, ckv_ref, sg_ref,
                  *, q_rank, kv_rank, idx_dim, heads, qk_dim, qk_scale, widx_scale):
    hn = _rms(x_ref[...], gpre_ref[...])
    z = _dot(hn.astype(BF16), w0_ref[...])
    o_kv = q_rank
    o_ki = q_rank + kv_rank
    o_g = o_ki + LANES
    cq = _rms(z[:, :o_kv], gq_ref[...]).astype(BF16)
    ckv_ref[...] = _rms(z[:, o_kv:o_ki], gkv_ref[...]).astype(BF16)
    ki = z[:, o_ki:o_g]
    lane = lax.broadcasted_iota(jnp.int32, (1, LANES), 1)
    in_k = lane < idx_dim
    mu = jnp.sum(jnp.where(in_k, ki, 0.0), axis=-1, keepdims=True) * (1.0 / idx_dim)
    d = jnp.where(in_k, ki - mu, 0.0)
    var = jnp.sum(d * d, axis=-1, keepdims=True) * (1.0 / idx_dim)
    kn = d * lax.rsqrt(var + NORM_EPS) * gik_ref[...] + bik_ref[...]
    ka_ref[...] = kn.astype(BF16)
    kb_ref[...] = pltpu.roll(kn, idx_dim, 1).astype(BF16)
    wi_ref[...] = ki * widx_scale
    g = z[:, o_g:]
    sg_ref[...] = (g * jax.nn.sigmoid(g)).astype(BF16)
    q = _dot(cq, wqup_ref[...])
    for h in range(heads):
        qh = q[:, h * qk_dim:(h + 1) * qk_dim].astype(BF16)
        qabs_ref[h] = (_dot(qh, wukt_ref[h]) * qk_scale).astype(BF16)
    qidx_ref[...] = _dot(cq, widxq_ref[...]).astype(BF16)


def _select_kernel(qidx_ref, ka_ref, kb_ref, wi_ref, bias_ref, st_ref, *, topk, heads, idx_dim, max_chunks):
    qb = pl.program_id(1)
    n_kc = (qb * Q_BLOCK) // KEY_CHUNK + 1
    kf = float(topk)
    pairs = heads // 2

    qi = qidx_ref[...]
    qst = jnp.concatenate([qi[:, j * LANES:(j + 1) * LANES] for j in range(pairs)], axis=0)
    wit = wi_ref[...].T
    w_rows = [wit[idx_dim + h:idx_dim + h + 1, :] for h in range(heads)]
    tq = qb * Q_BLOCK + lax.broadcasted_iota(jnp.int32, (1, Q_BLOCK), 1)
    n_causal = (tq + 1).astype(F32)
    small = n_causal <= kf
    row_i = lax.broadcasted_iota(jnp.int32, (KEY_CHUNK, KEY_CHUNK), 0)
    col_i = lax.broadcasted_iota(jnp.int32, (KEY_CHUNK, KEY_CHUNK), 1)
    eye = (row_i == col_i).astype(BF16)[:Q_BLOCK, :Q_BLOCK]
    lower_tri = (col_i <= row_i).astype(BF16)

    def fold8(x, op):
        return op(x.reshape(x.shape[0] // 8, 8, Q_BLOCK), axis=0)

    def variant(nc):
        chunks = [(c, c * KEY_CHUNK) for c in range(nc)]

        mn = mx = cgt = cge = None
        for c, k0 in chunks:
            sa = _dot_nt(ka_ref[k0:k0 + KEY_CHUNK, :], qst)
            sb = _dot_nt(kb_ref[k0:k0 + KEY_CHUNK, :], qst)
            acc = None
            for j in range(pairs):
                ta = jnp.maximum(sa[:, j * Q_BLOCK:(j + 1) * Q_BLOCK], 0.0) * w_rows[2 * j]
                tb = jnp.maximum(sb[:, j * Q_BLOCK:(j + 1) * Q_BLOCK], 0.0) * w_rows[2 * j + 1]
                acc = ta + tb if acc is None else acc + ta + tb
            if c == nc - 1:
                kpos = k0 + lax.broadcasted_iota(jnp.int32, (KEY_CHUNK, Q_BLOCK), 0)
                causal = kpos <= tq
                s = jnp.where(causal, acc, -jnp.inf)
                s_min = jnp.where(causal, acc, jnp.inf)
            else:
                s = s_min = acc
            st_ref[k0:k0 + KEY_CHUNK, :] = s
            parts = (fold8(s_min, jnp.min), fold8(s, jnp.max),
                     fold8(jnp.where(s > 0.0, 1.0, 0.0), jnp.sum),
                     fold8(jnp.where(s >= 0.0, 1.0, 0.0), jnp.sum))
            if mn is None:
                mn, mx, cgt, cge = parts
            else:
                mn, mx = jnp.minimum(mn, parts[0]), jnp.maximum(mx, parts[1])
                cgt, cge = cgt + parts[2], cge + parts[3]
        mn = jnp.min(mn, axis=0, keepdims=True)
        mx = jnp.max(mx, axis=0, keepdims=True)
        cgt0 = jnp.sum(cgt, axis=0, keepdims=True)
        cge0 = jnp.sum(cge, axis=0, keepdims=True)

        def count_gt(ref, thr):
            acc = None
            for _, k0 in chunks:
                part = fold8(jnp.where(ref[k0:k0 + KEY_CHUNK, :] > thr, 1.0, 0.0), jnp.sum)
                acc = part if acc is None else acc + part
            return jnp.sum(acc, axis=0, keepdims=True)

        pos = cgt0 >= kf
        zero = jnp.logical_and(jnp.logical_not(pos), cge0 >= kf)
        zero = jnp.logical_and(zero, jnp.logical_not(small))
        lo = jnp.where(pos, 0.0, jnp.where(zero, 0.0, mn * 1.0001))
        hi = jnp.where(pos, mx, 0.0)
        c_lo = jnp.where(pos, cgt0, jnp.where(zero, cgt0, n_causal))
        c_hi = jnp.where(pos, 0.0, cgt0)
        lo = jnp.where(small, -jnp.inf, lo)
        hi = jnp.where(small, -jnp.inf, hi)
        searching = jnp.logical_not(jnp.logical_or(small, zero))

        def unresolved(c_lo):
            bad = jnp.logical_and(c_lo != kf, searching)
            return (jnp.max(jnp.where(bad, 1.0, 0.0)) > 0.0).astype(jnp.int32)

        def bis_cond(carry):
            return jnp.logical_and(carry[0] < VALUE_BISECT_MAX, carry[1] > 0)

        def bisect(state):
            lo, hi, c_lo, c_hi = state
            mid = 0.5 * (lo + hi)
            c = count_gt(st_ref, mid)
            ge = c >= kf
            return (jnp.where(ge, mid, lo), jnp.where(ge, hi, mid),
                    jnp.where(ge, c, c_lo), jnp.where(ge, c_hi, c))

        def bis_body(carry):
            it, _, state = carry
            flag = unresolved(state[2])
            for _ in range(_passes_per_body(nc)):
                state = bisect(state)
            return it + _passes_per_body(nc), flag, state

        _, _, (lo, hi, c_lo, c_hi) = lax.while_loop(
            bis_cond, bis_body, (jnp.int32(0), jnp.int32(1), (lo, hi, c_lo, c_hi)))
        done = jnp.logical_or(c_lo == kf, small)
        thr = jnp.where(done, lo, hi)

        @pl.when(jnp.max(jnp.where(done, 0.0, 1.0)) > 0.0)
        def _():
            need = jnp.where(done, 0.0, kf - c_hi)
            tied_f = jnp.where(jnp.logical_or(done, zero), 0.0, 1.0)
            zero_f = jnp.where(zero, 1.0, 0.0)
            before = jnp.zeros((1, Q_BLOCK), F32)
            for _, k0 in chunks:
                s = st_ref[k0:k0 + KEY_CHUNK, :]
                band_f = (jnp.where(s > lo, 1.0, 0.0) * jnp.where(s <= hi, 1.0, 0.0) * tied_f
                          + jnp.where(s == 0.0, 1.0, 0.0) * zero_f)
                rank = _dot(lower_tri, band_f.astype(BF16)) + before
                take_f = band_f * jnp.where(rank <= need, 1.0, 0.0)
                st_ref[k0:k0 + KEY_CHUNK, :] = jnp.where(take_f > 0.5, jnp.inf, s)
                before = before + jnp.sum(fold8(band_f, jnp.sum), axis=0, keepdims=True)

        for _, k0 in chunks:
            sel_t = jnp.where(st_ref[k0:k0 + KEY_CHUNK, :] > thr, 1.0, 0.0).astype(BF16)
            sel = _dot_nt(eye, sel_t)
            bias_ref[:, k0:k0 + KEY_CHUNK] = ((sel - 1.0) * (-MASK_BIAS)).astype(BF16)
        if nc < max_chunks:
            bias_ref[:, nc * KEY_CHUNK:] = jnp.full((Q_BLOCK, (max_chunks - nc) * KEY_CHUNK), MASK_BIAS, BF16)

    for nc in range(1, max_chunks + 1):
        pl.when(n_kc == nc)(functools.partial(variant, nc))


def _attend_kernel(qabs_ref, ckv_ref, bias_ref, sg_ref, wuv_ref, og_ref, s_ref, p_ref, m_ref, acc_ref,
                   l_ref, *, heads, v_dim, max_chunks):
    n_kc = (pl.program_id(1) * Q_BLOCK) // KEY_CHUNK + 1
    sub = KEY_CHUNK // LANES
    gsz = heads // HEAD_GROUPS
    grows = gsz * Q_BLOCK

    def variant(nc):
        nk = nc * KEY_CHUNK
        q_all = jnp.concatenate([qabs_ref[h] for h in range(heads)], axis=0)
        for c in range(nc):
            k0 = c * KEY_CHUNK
            s = _dot_nt(q_all, ckv_ref[k0:k0 + KEY_CHUNK, :])
            bias = bias_ref[:, k0:k0 + KEY_CHUNK].astype(F32)
            for h in range(heads):
                rows = slice(h * Q_BLOCK, (h + 1) * Q_BLOCK)
                sh = s[rows, :] + bias
                s_ref[rows, k0:k0 + KEY_CHUNK] = sh
                mh = sh[:, :LANES]
                for t in range(1, sub):
                    mh = jnp.maximum(mh, sh[:, t * LANES:(t + 1) * LANES])
                m_ref[rows, :] = mh if c == 0 else jnp.maximum(m_ref[rows, :], mh)
        for g in range(HEAD_GROUPS):
            for i in range(gsz):
                rows = slice((g * gsz + i) * Q_BLOCK, (g * gsz + i + 1) * Q_BLOCK)
                m = jnp.broadcast_to(jnp.max(m_ref[rows, :], axis=-1, keepdims=True), (Q_BLOCK, LANES))
                lsum = None
                for t in range(nc * sub):
                    pt = jnp.exp2(s_ref[rows, t * LANES:(t + 1) * LANES] - m)
                    p_ref[rows, t * LANES:(t + 1) * LANES] = pt.astype(BF16)
                    lsum = pt if lsum is None else lsum + pt
                l_ref[rows, :] = lsum
            acc_ref[g * grows:(g + 1) * grows, :] = _dot(p_ref[g * grows:(g + 1) * grows, :nk], ckv_ref[:nk, :])

    for nc in range(1, max_chunks + 1):
        pl.when(n_kc == nc)(functools.partial(variant, nc))

    inv_l = 1.0 / jnp.sum(l_ref[...], axis=-1, keepdims=True)
    for h in range(heads):
        oh = (acc_ref[h * Q_BLOCK:(h + 1) * Q_BLOCK, :] * inv_l[h * Q_BLOCK:(h + 1) * Q_BLOCK]).astype(BF16)
        ov = _dot(oh, wuv_ref[h])
        og_ref[:, h * v_dim:(h + 1) * v_dim] = (
            ov * sg_ref[:, h * v_dim:(h + 1) * v_dim].astype(F32)).astype(BF16)


def _ple(h, p, gple, wpg, wple):
    gate = jax.nn.sigmoid(_dot(_rms(h, gple).astype(BF16), wpg))
    return h + _dot(p.astype(BF16), wple) * gate


def _rotate(x, cos, sin_next, sin_prev):
    nxt = pltpu.roll(x, LANES - 1, 1)
    prv = pltpu.roll(x, 1, 1)
    return x * cos + nxt * sin_next + prv * sin_prev


def _mid_kernel(og_ref, x_ref, p_ref, wout_ref, gple_ref, wpg_ref, wple_ref, gpre_ref,
                wq_ref, wk_ref, wv_ref, wg_ref, cos_ref, sn_ref, sp_ref,
                h_ref, q_ref, k_ref, v_ref, sg_ref, *, k_scale, qk_dim):
    h = x_ref[...] + _dot(og_ref[...], wout_ref[...])
    h = _ple(h, p_ref[...], gple_ref[...], wpg_ref[...], wple_ref[...])
    h_ref[...] = h
    hb = _rms(h, gpre_ref[...]).astype(BF16)
    q = _dot(hb, wq_ref[...])
    k = _dot(hb, wk_ref[...]) * k_scale
    for j in range(q.shape[1] // LANES):
        t = (j * LANES) % qk_dim
        cs = cos_ref[:, t:t + LANES]
        sn = sn_ref[:, t:t + LANES]
        sp = sp_ref[:, t:t + LANES]
        q_ref[:, j * LANES:(j + 1) * LANES] = _rotate(q[:, j * LANES:(j + 1) * LANES], cs, sn, sp).astype(BF16)
        k_ref[:, j * LANES:(j + 1) * LANES] = _rotate(k[:, j * LANES:(j + 1) * LANES], cs, sn, sp).astype(BF16)
    v_ref[...] = _dot(hb, wv_ref[...]).astype(BF16)
    g = _dot(hb, wg_ref[...])
    sg_ref[...] = (g * jax.nn.sigmoid(g)).astype(BF16)


def _ret_kernel(q_ref, k_ref, v_ref, sg_ref, dmask_ref, xi_ref, zeta_ref, decay_ref, og_ref,
                state_ref, *, heads, qk_dim, v_dim):
    @pl.when(pl.program_id(1) == 0)
    def _():
        state_ref[...] = jnp.zeros(state_ref.shape, F32)

    for c in range(q_ref.shape[0] // R_CHUNK):
        r = slice(c * R_CHUNK, (c + 1) * R_CHUNK)
        for h in range(heads):
            qh = q_ref[r, h * qk_dim:(h + 1) * qk_dim]
            kh = k_ref[r, h * qk_dim:(h + 1) * qk_dim]
            vh = v_ref[r, h * v_dim:(h + 1) * v_dim]
            s = _dot_nt(qh, kh) * dmask_ref[h]
            o = _dot(s.astype(BF16), vh) + _dot(qh, state_ref[h].astype(BF16)) * xi_ref[h]
            kz = (kh.astype(F32) * zeta_ref[h]).astype(BF16)
            state_ref[h] = state_ref[h] * decay_ref[h] + _dot_tn(kz, vh)
            mu = jnp.mean(o, axis=-1, keepdims=True)
            d = o - mu
            var = jnp.mean(d * d, axis=-1, keepdims=True)
            on = d * lax.rsqrt(var + NORM_EPS)
            og_ref[r, h * v_dim:(h + 1) * v_dim] = (
                on * sg_ref[r, h * v_dim:(h + 1) * v_dim].astype(F32)).astype(BF16)


def _out_kernel(og_ref, h_ref, p_ref, wout_ref, gple_ref, wpg_ref, wple_ref, gfin_ref, o_ref):
    h = h_ref[...] + _dot(og_ref[...], wout_ref[...])
    h = _ple(h, p_ref[...], gple_ref[...], wpg_ref[...], wple_ref[...])
    o_ref[...] = _rms(h, gfin_ref[...])


def _params(*sem):
    return pltpu.CompilerParams(dimension_semantics=sem, vmem_limit_bytes=VMEM_LIMIT)


def kernel(x, p, g_pre, a_w_in, a_g_q, a_g_kv, a_w_q_up, a_w_idx_q, a_g_ik, a_b_ik, a_w_uk, a_w_uv,
           a_w_out, r_w_in, r_w_out, w_ple_gate, g_ple, w_ple, g_final):
    B, L, D = x.shape
    N = B * L
    q_rank = a_g_q.shape[-1]
    kv_rank = a_g_kv.shape[-1]
    idx_dim = a_g_ik.shape[-1]
    heads, _, qk_dim = a_w_uk.shape[1:]
    v_dim = a_w_uv.shape[-1]
    idx_heads = a_w_idx_q.shape[-1] // idx_dim
    a_width = heads * v_dim
    ple_dim = p.shape[-1]
    assert idx_heads == heads and 2 * idx_dim == LANES and L % KEY_CHUNK == 0
    assert KEY_CHUNK % Q_BLOCK == 0 and L % Q_BLOCK == 0 and heads % HEAD_GROUPS == 0
    assert q_rank % LANES == 0 and kv_rank % LANES == 0 and a_w_in.shape[0] == 1 and r_w_in.shape[0] == 1
    topk = min(TOPK_MAX, L // 4)
    x2 = x.reshape(N, D)
    p_layers = p.reshape(p.shape[0], N, ple_dim)
    row = lambda a: a.reshape(1, -1).astype(F32)

    w_in = a_w_in[0]
    o_ki = q_rank + kv_rank
    o_g = o_ki + idx_dim + idx_heads
    w_ki = jnp.pad(w_in[:, o_ki:o_g], ((0, 0), (0, LANES - (o_g - o_ki))))
    w0 = jnp.concatenate([w_in[:, :o_ki], w_ki, w_in[:, o_g:]], axis=1).astype(BF16)
    pad_k = lambda a: jnp.pad(a.reshape(1, -1).astype(F32), ((0, 0), (0, LANES - idx_dim)))
    wukt = jnp.swapaxes(a_w_uk[0], 1, 2).astype(BF16)
    tm0 = math.gcd(ROWS_DSA_PROJ, N)
    n0 = w0.shape[1]
    outs0 = pl.pallas_call(
        functools.partial(_proj0_kernel, q_rank=q_rank, kv_rank=kv_rank, idx_dim=idx_dim, heads=heads,
                          qk_dim=qk_dim, qk_scale=qk_dim ** -0.5 * math.log2(math.e),
                          widx_scale=idx_heads ** -0.5 * idx_dim ** -0.5),
        grid=(N // tm0,),
        in_specs=[pl.BlockSpec((tm0, D), lambda i: (i, 0)), _const_spec((1, D)), _const_spec((D, n0)),
                  _const_spec((1, q_rank)), _const_spec((1, kv_rank)), _const_spec((1, LANES)),
                  _const_spec((1, LANES)), _const_spec((q_rank, heads * qk_dim)),
                  _const_spec((heads, qk_dim, kv_rank)), _const_spec((q_rank, idx_heads * idx_dim))],
        out_specs=[pl.BlockSpec((heads, tm0, kv_rank), lambda i: (0, i, 0)),
                   pl.BlockSpec((tm0, idx_heads * idx_dim), lambda i: (i, 0)),
                   pl.BlockSpec((tm0, LANES), lambda i: (i, 0)),
                   pl.BlockSpec((tm0, LANES), lambda i: (i, 0)),
                   pl.BlockSpec((tm0, LANES), lambda i: (i, 0)),
                   pl.BlockSpec((tm0, kv_rank), lambda i: (i, 0)),
                   pl.BlockSpec((tm0, a_width), lambda i: (i, 0))],
        out_shape=[jax.ShapeDtypeStruct((heads, N, kv_rank), BF16),
                   jax.ShapeDtypeStruct((N, idx_heads * idx_dim), BF16),
                   jax.ShapeDtypeStruct((N, LANES), BF16),
                   jax.ShapeDtypeStruct((N, LANES), BF16),
                   jax.ShapeDtypeStruct((N, LANES), F32),
                   jax.ShapeDtypeStruct((N, kv_rank), BF16),
                   jax.ShapeDtypeStruct((N, a_width), BF16)],
        compiler_params=_params("parallel"),
        name="dsa_proj",
    )(x2, row(g_pre[0]), w0, row(a_g_q[0]), row(a_g_kv[0]), pad_k(a_g_ik[0]), pad_k(a_b_ik[0]),
      a_w_q_up[0].astype(BF16), wukt, a_w_idx_q[0].astype(BF16))
    qabs, qidx, ka, kb, wi, ckv, sg0 = outs0

    nb = L // Q_BLOCK
    nkc = L // KEY_CHUNK
    b3 = lambda a: a.reshape(B, L, a.shape[-1])
    qblk = lambda w: pl.BlockSpec((None, Q_BLOCK, w), lambda b, i: (b, i, 0))
    seq = lambda w: pl.BlockSpec((None, L, w), lambda b, i: (b, 0, 0))
    bias = pl.pallas_call(
        functools.partial(_select_kernel, topk=topk, heads=heads, idx_dim=idx_dim, max_chunks=nkc),
        grid=(B, nb),
        in_specs=[qblk(idx_heads * idx_dim), seq(LANES), seq(LANES), qblk(LANES)],
        out_specs=qblk(L),
        out_shape=jax.ShapeDtypeStruct((B, L, L), BF16),
        scratch_shapes=[pltpu.VMEM((L, Q_BLOCK), F32)],
        compiler_params=_params("parallel", "arbitrary"),
        name="dsa_select",
    )(b3(qidx), b3(ka), b3(kb), b3(wi))
    og0 = pl.pallas_call(
        functools.partial(_attend_kernel, heads=heads, v_dim=v_dim, max_chunks=nkc),
        grid=(B, nb),
        in_specs=[pl.BlockSpec((heads, None, Q_BLOCK, kv_rank), lambda b, i: (0, b, i, 0)),
                  seq(kv_rank), qblk(L), qblk(a_width), _const_spec((heads, kv_rank, v_dim))],
        out_specs=qblk(a_width),
        out_shape=jax.ShapeDtypeStruct((B, L, a_width), BF16),
        scratch_shapes=[pltpu.VMEM((heads * Q_BLOCK, L), F32),
                        pltpu.VMEM((heads * Q_BLOCK, L), BF16),
                        pltpu.VMEM((heads * Q_BLOCK, LANES), F32),
                        pltpu.VMEM((heads * Q_BLOCK, kv_rank), F32),
                        pltpu.VMEM((heads * Q_BLOCK, LANES), F32)],
        compiler_params=_params("parallel", "arbitrary"),
        name="dsa_attend",
    )(qabs.reshape(heads, B, L, kv_rank), b3(ckv), bias, b3(sg0), a_w_uv[0].astype(BF16))

    r_heads = RET_HEADS
    r_qk = D // r_heads
    r_v = 2 * D // r_heads
    r_width = r_heads * r_v
    assert r_w_in.shape[-1] == 2 * D + 2 * r_width and r_qk % LANES == 0 and L % R_CHUNK == 0
    wr = r_w_in[0].astype(BF16)
    wq, wk, wv, wg = wr[:, :D], wr[:, D:2 * D], wr[:, 2 * D:2 * D + r_width], wr[:, 2 * D + r_width:]
    angle = 1.0 / (10000.0 ** np.linspace(0.0, 1.0, r_qk // 2))
    theta = np.arange(L, dtype=np.float64)[:, None] * np.repeat(angle, 2)[None, :]
    even = (np.arange(r_qk) % 2 == 0)[None, :]
    cos_t = jnp.asarray(np.cos(theta), F32)
    sin_next = jnp.asarray(np.where(even, -np.sin(theta), 0.0), F32)
    sin_prev = jnp.asarray(np.where(even, 0.0, np.sin(theta)), F32)
    tm1 = math.gcd(ROWS_MID_PROJ, L)
    tok = lambda w: pl.BlockSpec((tm1, w), lambda b, i: (b * (L // tm1) + i, 0))
    tab = pl.BlockSpec((tm1, r_qk), lambda b, i: (i, 0))
    h1, rq, rk, rv, sg1 = pl.pallas_call(
        functools.partial(_mid_kernel, k_scale=r_qk ** -0.5, qk_dim=r_qk),
        grid=(B, L // tm1),
        in_specs=[tok(a_width), tok(D),
                  pl.BlockSpec((None, tm1, ple_dim), lambda b, i: (0, b * (L // tm1) + i, 0)),
                  _const_spec((a_width, D)), _const_spec((1, D)),
                  _const_spec((D, D)), _const_spec((ple_dim, D)), _const_spec((1, D)),
                  _const_spec((D, D)), _const_spec((D, D)), _const_spec((D, r_width)),
                  _const_spec((D, r_width)), tab, tab, tab],
        out_specs=[tok(D), tok(D), tok(D), tok(r_width), tok(r_width)],
        out_shape=[jax.ShapeDtypeStruct((N, D), F32), jax.ShapeDtypeStruct((N, D), BF16),
                   jax.ShapeDtypeStruct((N, D), BF16), jax.ShapeDtypeStruct((N, r_width), BF16),
                   jax.ShapeDtypeStruct((N, r_width), BF16)],
        compiler_params=_params("parallel", "arbitrary"),
        name="mid_proj",
    )(og0.reshape(N, a_width), x2, p_layers, a_w_out[0].astype(BF16), row(g_ple[0]),
      w_ple_gate[0].astype(BF16), w_ple[0].astype(BF16), row(g_pre[1]), wq, wk, wv, wg,
      cos_t, sin_next, sin_prev)

    log_gamma = np.log(1.0 - 2.0 ** (-5.0 - np.arange(r_heads, dtype=np.float64)))
    ci = np.arange(R_CHUNK, dtype=np.float64)
    diff = ci[:, None] - ci[None, :]
    dmask = jnp.asarray(np.where(diff[None] >= 0, np.exp(diff[None] * log_gamma[:, None, None]), 0.0), F32)
    xi = np.exp((ci[None, :] + 1.0) * log_gamma[:, None])
    zeta = np.exp((R_CHUNK - 1.0 - ci[None, :]) * log_gamma[:, None])
    decay = np.exp(R_CHUNK * log_gamma)
    xi_b = jnp.asarray(np.broadcast_to(xi[:, :, None], (r_heads, R_CHUNK, r_v)), F32)
    zeta_b = jnp.asarray(np.broadcast_to(zeta[:, :, None], (r_heads, R_CHUNK, r_qk)), F32)
    decay_b = jnp.asarray(np.broadcast_to(decay[:, None, None], (r_heads, 1, r_v)), F32)
    rows_step = R_CHUNK * math.gcd(R_STEP_CHUNKS, L // R_CHUNK)
    cblk = lambda w: pl.BlockSpec((None, rows_step, w), lambda b, c: (b, c, 0))
    og1 = pl.pallas_call(
        functools.partial(_ret_kernel, heads=r_heads, qk_dim=r_qk, v_dim=r_v),
        grid=(B, L // rows_step),
        in_specs=[cblk(D), cblk(D), cblk(r_width), cblk(r_width),
                  _const_spec((r_heads, R_CHUNK, R_CHUNK)), _const_spec((r_heads, R_CHUNK, r_v)),
                  _const_spec((r_heads, R_CHUNK, r_qk)), _const_spec((r_heads, 1, r_v))],
        out_specs=cblk(r_width),
        out_shape=jax.ShapeDtypeStruct((B, L, r_width), BF16),
        scratch_shapes=[pltpu.VMEM((r_heads, r_qk, r_v), F32)],
        compiler_params=_params("parallel", "arbitrary"),
        name="retention",
    )(rq.reshape(B, L, D), rk.reshape(B, L, D), rv.reshape(B, L, r_width), sg1.reshape(B, L, r_width),
      dmask, xi_b, zeta_b, decay_b)

    tm2 = math.gcd(ROWS_OUT_PROJ, N)
    tk2 = lambda w: pl.BlockSpec((tm2, w), lambda i: (i, 0))
    out = pl.pallas_call(
        _out_kernel,
        grid=(N // tm2,),
        in_specs=[tk2(r_width), tk2(D), pl.BlockSpec((None, tm2, ple_dim), lambda i: (1, i, 0)),
                  _const_spec((r_width, D)), _const_spec((1, D)),
                  _const_spec((D, D)), _const_spec((ple_dim, D)), _const_spec((1, D))],
        out_specs=tk2(D),
        out_shape=jax.ShapeDtypeStruct((N, D), F32),
        compiler_params=_params("parallel"),
        name="out_proj",
    )(og1.reshape(N, r_width), h1, p_layers, r_w_out[0].astype(BF16), row(g_ple[1]),
      w_ple_gate[1].astype(BF16), w_ple[1].astype(BF16), row(g_final))
    return out.reshape(B, L, D)
```

```python
import functools
import math

import jax
import jax.numpy as jnp
import numpy as np
from jax import lax
from jax.experimental import pallas as pl
from jax.experimental.pallas import tpu as pltpu

F32 = jnp.float32
BF16 = jnp.bfloat16

NORM_EPS = 1e-6
TOPK_MAX = 256
Q_BLOCK = 256
KEY_CHUNK = 256
R_CHUNK = 128
LANES = 128
MASK_BIAS = -1e30
VALUE_BISECT_MAX = 34
VMEM_LIMIT = 56 * 1024 * 1024
HEAD_GROUPS = 4
RET_HEADS = 4
R_STEP_CHUNKS = 8
ROWS_DSA_PROJ = 1024
ROWS_MID_PROJ = 512
ROWS_OUT_PROJ = 1024
MID_SPLIT = 2
OUT_SPLIT = 4


def _passes_per_body(n_chunks):
    return max(1, min(4, round(6 / n_chunks)))


def _rms(x, g):
    return x * lax.rsqrt(jnp.mean(x * x, axis=-1, keepdims=True) + NORM_EPS) * g


def _dot(a, b):
    return jnp.dot(a, b, preferred_element_type=F32)


def _dot_nt(a, b):
    return lax.dot_general(a, b, (((1,), (1,)), ((), ())), preferred_element_type=F32)


def _dot_tn(a, b):
    return lax.dot_general(a, b, (((0,), (0,)), ((), ())), preferred_element_type=F32)


def _const_spec(shape):
    nd = len(shape)
    return pl.BlockSpec(shape, lambda *_: (0,) * nd, pipeline_mode=pl.Buffered(1))


def _proj0_kernel(x_ref, gpre_ref, w0_ref, gq_ref, gkv_ref, gik_ref, bik_ref, wqup_ref, wukt_ref,
                  widxq_ref, qabs_ref, qidx_ref, ka_ref, kb_ref, wi_ref, ckv_ref, sg_ref,
                  *, q_rank, kv_rank, idx_dim, heads, qk_dim, qk_scale, widx_scale):
    hn = _rms(x_ref[...], gpre_ref[...])
    z = _dot(hn.astype(BF16), w0_ref[...])
    o_kv = q_rank
    o_ki = q_rank + kv_rank
    o_g = o_ki + LANES
    cq = _rms(z[:, :o_kv], gq_ref[...]).astype(BF16)
    ckv_ref[...] = _rms(z[:, o_kv:o_ki], gkv_ref[...]).astype(BF16)
    ki = z[:, o_ki:o_g]
    lane = lax.broadcasted_iota(jnp.int32, (1, LANES), 1)
    in_k = lane < idx_dim
    mu = jnp.sum(jnp.where(in_k, ki, 0.0), axis=-1, keepdims=True) * (1.0 / idx_dim)
    d = jnp.where(in_k, ki - mu, 0.0)
    var = jnp.sum(d * d, axis=-1, keepdims=True) * (1.0 / idx_dim)
    kn = d * lax.rsqrt(var + NORM_EPS) * gik_ref[...] + bik_ref[...]
    ka_ref[...] = kn.astype(BF16)
    kb_ref[...] = pltpu.roll(kn, idx_dim, 1).astype(BF16)
    wi_ref[...] = ki * widx_scale
    g = z[:, o_g:]
    sg_ref[...] = (g * jax.nn.sigmoid(g)).astype(BF16)
    q = _dot(cq, wqup_ref[...])
    for h in range(heads):
        qh = q[:, h * qk_dim:(h + 1) * qk_dim].astype(BF16)
        qabs_ref[h] = (_dot(qh, wukt_ref[h]) * qk_scale).astype(BF16)
    qidx_ref[...] = _dot(cq, widxq_ref[...]).astype(BF16)


def _select_kernel(qidx_ref, ka_ref, kb_ref, wi_ref, bias_ref, st_ref, *, topk, heads, idx_dim, max_chunks):
    qb = pl.program_id(1)
    n_kc = (qb * Q_BLOCK) // KEY_CHUNK + 1
    kf = float(topk)
    pairs = heads // 2

    qi = qidx_ref[...]
    qst = jnp.concatenate([qi[:, j * LANES:(j + 1) * LANES] for j in range(pairs)], axis=0)
    wit = wi_ref[...].T
    w_rows = [wit[idx_dim + h:idx_dim + h + 1, :] for h in range(heads)]
    tq = qb * Q_BLOCK + lax.broadcasted_iota(jnp.int32, (1, Q_BLOCK), 1)
    n_causal = (tq + 1).astype(F32)
    small = n_causal <= kf
    row_i = lax.broadcasted_iota(jnp.int32, (KEY_CHUNK, KEY_CHUNK), 0)
    col_i = lax.broadcasted_iota(jnp.int32, (KEY_CHUNK, KEY_CHUNK), 1)
    eye = (row_i == col_i).astype(BF16)[:Q_BLOCK, :Q_BLOCK]
    lower_tri = (col_i <= row_i).astype(BF16)

    def fold8(x, op):
        return op(x.reshape(x.shape[0] // 8, 8, Q_BLOCK), axis=0)

    def variant(nc):
        chunks = [(c, c * KEY_CHUNK) for c in range(nc)]

        mn = mx = cgt = cge = None
        for c, k0 in chunks:
            sa = _dot_nt(ka_ref[k0:k0 + KEY_CHUNK, :], qst)
            sb = _dot_nt(kb_ref[k0:k0 + KEY_CHUNK, :], qst)
            acc = None
            for j in range(pairs):
                ta = jnp.maximum(sa[:, j * Q_BLOCK:(j + 1) * Q_BLOCK], 0.0) * w_rows[2 * j]
                tb = jnp.maximum(sb[:, j * Q_BLOCK:(j + 1) * Q_BLOCK], 0.0) * w_rows[2 * j + 1]
                acc = ta + tb if acc is None else acc + ta + tb
            if c == nc - 1:
                kpos = k0 + lax.broadcasted_iota(jnp.int32, (KEY_CHUNK, Q_BLOCK), 0)
                causal = kpos <= tq
                s = jnp.where(causal, acc, -jnp.inf)
                s_min = jnp.where(causal, acc, jnp.inf)
            else:
                s = s_min = acc
            st_ref[k0:k0 + KEY_CHUNK, :] = s
            parts = (fold8(s_min, jnp.min), fold8(s, jnp.max),
                     fold8(jnp.where(s > 0.0, 1.0, 0.0), jnp.sum),
                     fold8(jnp.where(s >= 0.0, 1.0, 0.0), jnp.sum))
            if mn is None:
                mn, mx, cgt, cge = parts
            else:
                mn, mx = jnp.minimum(mn, parts[0]), jnp.maximum(mx, parts[1])
                cgt, cge = cgt + parts[2], cge + parts[3]
        mn = jnp.min(mn, axis=0, keepdims=True)
        mx = jnp.max(mx, axis=0, keepdims=True)
        cgt0 = jnp.sum(cgt, axis=0, keepdims=True)
        cge0 = jnp.sum(cge, axis=0, keepdims=True)

        def count_gt(ref, thr):
            acc = None
            for _, k0 in chunks:
                part = fold8(jnp.where(ref[k0:k0 + KEY_CHUNK, :] > thr, 1.0, 0.0), jnp.sum)
                acc = part if acc is None else acc + part
            return jnp.sum(acc, axis=0, keepdims=True)

        pos = cgt0 >= kf
        zero = jnp.logical_and(jnp.logical_not(pos), cge0 >= kf)
        zero = jnp.logical_and(zero, jnp.logical_not(small))
        lo = jnp.where(pos, 0.0, jnp.where(zero, 0.0, mn * 1.0001))
        hi = jnp.where(pos, mx, 0.0)
        c_lo = jnp.where(pos, cgt0, jnp.where(zero, cgt0, n_causal))
        c_hi = jnp.where(pos, 0.0, cgt0)
        lo = jnp.where(small, -jnp.inf, lo)
        hi = jnp.where(small, -jnp.inf, hi)
        searching = jnp.logical_not(jnp.logical_or(small, zero))

        def unresolved(c_lo):
            bad = jnp.logical_and(c_lo != kf, searching)
            return (jnp.max(jnp.where(bad, 1.0, 0.0)) > 0.0).astype(jnp.int32)

        def bis_cond(carry):
            return jnp.logical_and(carry[0] < VALUE_BISECT_MAX, carry[1] > 0)

        def bisect(state):
            lo, hi, c_lo, c_hi = state
            mid = 0.5 * (lo + hi)
            c = count_gt(st_ref, mid)
            ge = c >= kf
            return (jnp.where(ge, mid, lo), jnp.where(ge, hi, mid),
                    jnp.where(ge, c, c_lo), jnp.where(ge, c_hi, c))

        def bis_body(carry):
            it, _, state = carry
            flag = unresolved(state[2])
            for _ in range(_passes_per_body(nc)):
                state = bisect(state)
            return it + _passes_per_body(nc), flag, state

        _, _, (lo, hi, c_lo, c_hi) = lax.while_loop(
            bis_cond, bis_body, (jnp.int32(0), jnp.int32(1), (lo, hi, c_lo, c_hi)))
        done = jnp.logical_or(c_lo == kf, small)
        thr = jnp.where(done, lo, hi)

        @pl.when(jnp.max(jnp.where(done, 0.0, 1.0)) > 0.0)
        def _():
            need = jnp.where(done, 0.0, kf - c_hi)
            tied_f = jnp.where(jnp.logical_or(done, zero), 0.0, 1.0)
            zero_f = jnp.where(zero, 1.0, 0.0)
            before = jnp.zeros((1, Q_BLOCK), F32)
            for _, k0 in chunks:
                s = st_ref[k0:k0 + KEY_CHUNK, :]
                band_f = (jnp.where(s > lo, 1.0, 0.0) * jnp.where(s <= hi, 1.0, 0.0) * tied_f
                          + jnp.where(s == 0.0, 1.0, 0.0) * zero_f)
                rank = _dot(lower_tri, band_f.astype(BF16)) + before
                take_f = band_f * jnp.where(rank <= need, 1.0, 0.0)
                st_ref[k0:k0 + KEY_CHUNK, :] = jnp.where(take_f > 0.5, jnp.inf, s)
                before = before + jnp.sum(fold8(band_f, jnp.sum), axis=0, keepdims=True)

        for _, k0 in chunks:
            sel_t = jnp.where(st_ref[k0:k0 + KEY_CHUNK, :] > thr, 1.0, 0.0).astype(BF16)
            sel = _dot_nt(eye, sel_t)
            bias_ref[:, k0:k0 + KEY_CHUNK] = ((sel - 1.0) * (-MASK_BIAS)).astype(BF16)
        if nc < max_chunks:
            bias_ref[:, nc * KEY_CHUNK:] = jnp.full((Q_BLOCK, (max_chunks - nc) * KEY_CHUNK), MASK_BIAS, BF16)

    for nc in range(1, max_chunks + 1):
        pl.when(n_kc == nc)(functools.partial(variant, nc))


def _attend_kernel(qabs_ref, ckv_ref, bias_ref, sg_ref, wuv_ref, og_ref, s_ref, p_ref, m_ref, acc_ref,
                   l_ref, *, heads, v_dim, max_chunks):
    n_kc = (pl.program_id(1) * Q_BLOCK) // KEY_CHUNK + 1
    sub = KEY_CHUNK // LANES
    gsz = heads // HEAD_GROUPS
    grows = gsz * Q_BLOCK

    def variant(nc):
        nk = nc * KEY_CHUNK
        q_all = jnp.concatenate([qabs_ref[h] for h in range(heads)], axis=0)
        for c in range(nc):
            k0 = c * KEY_CHUNK
            s = _dot_nt(q_all, ckv_ref[k0:k0 + KEY_CHUNK, :])
            bias = bias_ref[:, k0:k0 + KEY_CHUNK].astype(F32)
            for h in range(heads):
                rows = slice(h * Q_BLOCK, (h + 1) * Q_BLOCK)
                sh = s[rows, :] + bias
                s_ref[rows, k0:k0 + KEY_CHUNK] = sh
                mh = sh[:, :LANES]
                for t in range(1, sub):
                    mh = jnp.maximum(mh, sh[:, t * LANES:(t + 1) * LANES])
                m_ref[rows, :] = mh if c == 0 else jnp.maximum(m_ref[rows, :], mh)
        for g in range(HEAD_GROUPS):
            for i in range(gsz):
                rows = slice((g * gsz + i) * Q_BLOCK, (g * gsz + i + 1) * Q_BLOCK)
                m = jnp.broadcast_to(jnp.max(m_ref[rows, :], axis=-1, keepdims=True), (Q_BLOCK, LANES))
                lsum = None
                for t in range(nc * sub):
                    pt = jnp.exp2(s_ref[rows, t * LANES:(t + 1) * LANES] - m)
                    p_ref[rows, t * LANES:(t + 1) * LANES] = pt.astype(BF16)
                    lsum = pt if lsum is None else lsum + pt
                l_ref[rows, :] = lsum
            acc_ref[g * grows:(g + 1) * grows, :] = _dot(p_ref[g * grows:(g + 1) * grows, :nk], ckv_ref[:nk, :])

    for nc in range(1, max_chunks + 1):
        pl.when(n_kc == nc)(functools.partial(variant, nc))

    inv_l = 1.0 / jnp.sum(l_ref[...], axis=-1, keepdims=True)
    for h in range(heads):
        oh = (acc_ref[h * Q_BLOCK:(h + 1) * Q_BLOCK, :] * inv_l[h * Q_BLOCK:(h + 1) * Q_BLOCK]).astype(BF16)
        ov = _dot(oh, wuv_ref[h])
        og_ref[:, h * v_dim:(h + 1) * v_dim] = (
            ov * sg_ref[:, h * v_dim:(h + 1) * v_dim].astype(F32)).astype(BF16)


def _rotate(x, cos, sin_next, sin_prev):
    nxt = pltpu.roll(x, LANES - 1, 1)
    prv = pltpu.roll(x, 1, 1)
    return x * cos + nxt * sin_next + prv * sin_prev


def _mid_kernel(og_ref, x_ref, p_ref, wout_ref, gple_ref, wpg_ref, wple_ref, gpre_ref,
                wq_ref, wk_ref, wv_ref, wg_ref, cos_ref, sn_ref, sp_ref,
                h_ref, q_ref, k_ref, v_ref, sg_ref, *, k_scale, qk_dim):
    n_rows = x_ref.shape[0]
    halves = [slice(i * (n_rows // MID_SPLIT), (i + 1) * (n_rows // MID_SPLIT)) for i in range(MID_SPLIT)]
    hs = [x_ref[r, :] + _dot(og_ref[r, :], wout_ref[...]) for r in halves]
    gates = [jax.nn.sigmoid(_dot(_rms(h, gple_ref[...]).astype(BF16), wpg_ref[...])) for h in hs]
    hs = [h + _dot(p_ref[r, :].astype(BF16), wple_ref[...]) * gt for h, r, gt in zip(hs, halves, gates)]
    for h, r in zip(hs, halves):
        h_ref[r, :] = h
    hbs = [_rms(h, gpre_ref[...]).astype(BF16) for h in hs]
    for hb, r in zip(hbs, halves):
        q = _dot(hb, wq_ref[...])
        k = _dot(hb, wk_ref[...]) * k_scale
        for j in range(q.shape[1] // LANES):
            t = (j * LANES) % qk_dim
            cs = cos_ref[r, t:t + LANES]
            sn = sn_ref[r, t:t + LANES]
            sp = sp_ref[r, t:t + LANES]
            q_ref[r, j * LANES:(j + 1) * LANES] = _rotate(q[:, j * LANES:(j + 1) * LANES], cs, sn, sp).astype(BF16)
            k_ref[r, j * LANES:(j + 1) * LANES] = _rotate(k[:, j * LANES:(j + 1) * LANES], cs, sn, sp).astype(BF16)
        v_ref[r, :] = _dot(hb, wv_ref[...]).astype(BF16)
        g = _dot(hb, wg_ref[...])
        sg_ref[r, :] = (g * jax.nn.sigmoid(g)).astype(BF16)


def _ret_kernel(q_ref, k_ref, v_ref, sg_ref, dmask_ref, xi_ref, zeta_ref, decay_ref, og_ref,
                state_ref, *, heads, qk_dim, v_dim):
    @pl.when(pl.program_id(1) == 0)
    def _():
        state_ref[...] = jnp.zeros(state_ref.shape, F32)

    for c in range(q_ref.shape[0] // R_CHUNK):
        r = slice(c * R_CHUNK, (c + 1) * R_CHUNK)
        for h in range(heads):
            qh = q_ref[r, h * qk_dim:(h + 1) * qk_dim]
            kh = k_ref[r, h * qk_dim:(h + 1) * qk_dim]
            vh = v_ref[r, h * v_dim:(h + 1) * v_dim]
            s = _dot_nt(qh, kh) * dmask_ref[h]
            o = _dot(s.astype(BF16), vh) + _dot(qh, state_ref[h].astype(BF16)) * xi_ref[h]
            kz = (kh.astype(F32) * zeta_ref[h]).astype(BF16)
            state_ref[h] = state_ref[h] * decay_ref[h] + _dot_tn(kz, vh)
            mu = jnp.mean(o, axis=-1, keepdims=True)
            d = o - mu
            var = jnp.mean(d * d, axis=-1, keepdims=True)
            on = d * lax.rsqrt(var + NORM_EPS)
            og_ref[r, h * v_dim:(h + 1) * v_dim] = (
                on * sg_ref[r, h * v_dim:(h + 1) * v_dim].astype(F32)).astype(BF16)


def _out_kernel(og_ref, h_ref, p_ref, wout_ref, gple_ref, wpg_ref, wple_ref, gfin_ref, o_ref):
    n_rows = h_ref.shape[0]
    parts = [slice(i * (n_rows // OUT_SPLIT), (i + 1) * (n_rows // OUT_SPLIT)) for i in range(OUT_SPLIT)]
    hs = [h_ref[r, :] + _dot(og_ref[r, :], wout_ref[...]) for r in parts]
    gates = [jax.nn.sigmoid(_dot(_rms(h, gple_ref[...]).astype(BF16), wpg_ref[...])) for h in hs]
    for h, r, gt in zip(hs, parts, gates):
        o_ref[r, :] = _rms(h + _dot(p_ref[r, :].astype(BF16), wple_ref[...]) * gt, gfin_ref[...])


def _params(*sem):
    return pltpu.CompilerParams(dimension_semantics=sem, vmem_limit_bytes=VMEM_LIMIT)


def kernel(x, p, g_pre, a_w_in, a_g_q, a_g_kv, a_w_q_up, a_w_idx_q, a_g_ik, a_b_ik, a_w_uk, a_w_uv,
           a_w_out, r_w_in, r_w_out, w_ple_gate, g_ple, w_ple, g_final):
    B, L, D = x.shape
    N = B * L
    q_rank = a_g_q.shape[-1]
    kv_rank = a_g_kv.shape[-1]
    idx_dim = a_g_ik.shape[-1]
    heads, _, qk_dim = a_w_uk.shape[1:]
    v_dim = a_w_uv.shape[-1]
    idx_heads = a_w_idx_q.shape[-1] // idx_dim
    a_width = heads * v_dim
    ple_dim = p.shape[-1]
    assert idx_heads == heads and 2 * idx_dim == LANES and L % KEY_CHUNK == 0
    assert KEY_CHUNK % Q_BLOCK == 0 and L % Q_BLOCK == 0 and heads % HEAD_GROUPS == 0
    assert q_rank % LANES == 0 and kv_rank % LANES == 0 and a_w_in.shape[0] == 1 and r_w_in.shape[0] == 1
    topk = min(TOPK_MAX, L // 4)
    x2 = x.reshape(N, D)
    p_layers = p.reshape(p.shape[0], N, ple_dim)
    row = lambda a: a.reshape(1, -1).astype(F32)

    w_in = a_w_in[0]
    o_ki = q_rank + kv_rank
    o_g = o_ki + idx_dim + idx_heads
    w_ki = jnp.pad(w_in[:, o_ki:o_g], ((0, 0), (0, LANES - (o_g - o_ki))))
    w0 = jnp.concatenate([w_in[:, :o_ki], w_ki, w_in[:, o_g:]], axis=1).astype(BF16)
    pad_k = lambda a: jnp.pad(a.reshape(1, -1).astype(F32), ((0, 0), (0, LANES - idx_dim)))
    wukt = jnp.swapaxes(a_w_uk[0], 1, 2).astype(BF16)
    tm0 = math.gcd(ROWS_DSA_PROJ, N)
    n0 = w0.shape[1]
    outs0 = pl.pallas_call(
        functools.partial(_proj0_kernel, q_rank=q_rank, kv_rank=kv_rank, idx_dim=idx_dim, heads=heads,
                          qk_dim=qk_dim, qk_scale=qk_dim ** -0.5 * math.log2(math.e),
                          widx_scale=idx_heads ** -0.5 * idx_dim ** -0.5),
        grid=(N // tm0,),
        in_specs=[pl.BlockSpec((tm0, D), lambda i: (i, 0)), _const_spec((1, D)), _const_spec((D, n0)),
                  _const_spec((1, q_rank)), _const_spec((1, kv_rank)), _const_spec((1, LANES)),
                  _const_spec((1, LANES)), _const_spec((q_rank, heads * qk_dim)),
                  _const_spec((heads, qk_dim, kv_rank)), _const_spec((q_rank, idx_heads * idx_dim))],
        out_specs=[pl.BlockSpec((heads, tm0, kv_rank), lambda i: (0, i, 0)),
                   pl.BlockSpec((tm0, idx_heads * idx_dim), lambda i: (i, 0)),
                   pl.BlockSpec((tm0, LANES), lambda i: (i, 0)),
                   pl.BlockSpec((tm0, LANES), lambda i: (i, 0)),
                   pl.BlockSpec((tm0, LANES), lambda i: (i, 0)),
                   pl.BlockSpec((tm0, kv_rank), lambda i: (i, 0)),
                   pl.BlockSpec((tm0, a_width), lambda i: (i, 0))],
        out_shape=[jax.ShapeDtypeStruct((heads, N, kv_rank), BF16),
                   jax.ShapeDtypeStruct((N, idx_heads * idx_dim), BF16),
                   jax.ShapeDtypeStruct((N, LANES), BF16),
                   jax.ShapeDtypeStruct((N, LANES), BF16),
                   jax.ShapeDtypeStruct((N, LANES), F32),
                   jax.ShapeDtypeStruct((N, kv_rank), BF16),
                   jax.ShapeDtypeStruct((N, a_width), BF16)],
        compiler_params=_params("parallel"),
        name="dsa_proj",
    )(x2, row(g_pre[0]), w0, row(a_g_q[0]), row(a_g_kv[0]), pad_k(a_g_ik[0]), pad_k(a_b_ik[0]),
      a_w_q_up[0].astype(BF16), wukt, a_w_idx_q[0].astype(BF16))
    qabs, qidx, ka, kb, wi, ckv, sg0 = outs0

    nb = L // Q_BLOCK
    nkc = L // KEY_CHUNK
    b3 = lambda a: a.reshape(B, L, a.shape[-1])
    qblk = lambda w: pl.BlockSpec((None, Q_BLOCK, w), lambda b, i: (b, i, 0))
    seq = lambda w: pl.BlockSpec((None, L, w), lambda b, i: (b, 0, 0))
    bias = pl.pallas_call(
        functools.partial(_select_kernel, topk=topk, heads=heads, idx_dim=idx_dim, max_chunks=nkc),
        grid=(B, nb),
        in_specs=[qblk(idx_heads * idx_dim), seq(LANES), seq(LANES), qblk(LANES)],
        out_specs=qblk(L),
        out_shape=jax.ShapeDtypeStruct((B, L, L), BF16),
        scratch_shapes=[pltpu.VMEM((L, Q_BLOCK), F32)],
        compiler_params=_params("parallel", "arbitrary"),
        name="dsa_select",
    )(b3(qidx), b3(ka), b3(kb), b3(wi))
    og0 = pl.pallas_call(
        functools.partial(_attend_kernel, heads=heads, v_dim=v_dim, max_chunks=nkc),
        grid=(B, nb),
        in_specs=[pl.BlockSpec((heads, None, Q_BLOCK, kv_rank), lambda b, i: (0, b, i, 0)),
                  seq(kv_rank), qblk(L), qblk(a_width), _const_spec((heads, kv_rank, v_dim))],
        out_specs=qblk(a_width),
        out_shape=jax.ShapeDtypeStruct((B, L, a_width), BF16),
        scratch_shapes=[pltpu.VMEM((heads * Q_BLOCK, L), F32),
                        pltpu.VMEM((heads * Q_BLOCK, L), BF16),
                        pltpu.VMEM((heads * Q_BLOCK, LANES), F32),
                        pltpu.VMEM((heads * Q_BLOCK, kv_rank), F32),
                        pltpu.VMEM((heads * Q_BLOCK, LANES), F32)],
        compiler_params=_params("parallel", "arbitrary"),
        name="dsa_attend",
    )(qabs.reshape(heads, B, L, kv_rank), b3(ckv), bias, b3(sg0), a_w_uv[0].astype(BF16))

    r_heads = RET_HEADS
    r_qk = D // r_heads
    r_v = 2 * D // r_heads
    r_width = r_heads * r_v
    assert r_w_in.shape[-1] == 2 * D + 2 * r_width and r_qk % LANES == 0 and L % R_CHUNK == 0
    wr = r_w_in[0].astype(BF16)
    wq, wk, wv, wg = wr[:, :D], wr[:, D:2 * D], wr[:, 2 * D:2 * D + r_width], wr[:, 2 * D + r_width:]
    angle = 1.0 / (10000.0 ** np.linspace(0.0, 1.0, r_qk // 2))
    theta = np.arange(L, dtype=np.float64)[:, None] * np.repeat(angle, 2)[None, :]
    even = (np.arange(r_qk) % 2 == 0)[None, :]
    cos_t = jnp.asarray(np.cos(theta), F32)
    sin_next = jnp.asarray(np.where(even, -np.sin(theta), 0.0), F32)
    sin_prev = jnp.asarray(np.where(even, 0.0, np.sin(theta)), F32)
    tm1 = math.gcd(ROWS_MID_PROJ, L)
    tok = lambda w: pl.BlockSpec((tm1, w), lambda b, i: (b * (L // tm1) + i, 0))
    tab = pl.BlockSpec((tm1, r_qk), lambda b, i: (i, 0))
    h1, rq, rk, rv, sg1 = pl.pallas_call(
        functools.partial(_mid_kernel, k_scale=r_qk ** -0.5, qk_dim=r_qk),
        grid=(B, L // tm1),
        in_specs=[tok(a_width), tok(D),
                  pl.BlockSpec((None, tm1, ple_dim), lambda b, i: (0, b * (L // tm1) + i, 0)),
                  _const_spec((a_width, D)), _const_spec((1, D)),
                  _const_spec((D, D)), _const_spec((ple_dim, D)), _const_spec((1, D)),
                  _const_spec((D, D)), _const_spec((D, D)), _const_spec((D, r_width)),
                  _const_spec((D, r_width)), tab, tab, tab],
        out_specs=[tok(D), tok(D), tok(D), tok(r_width), tok(r_width)],
        out_shape=[jax.ShapeDtypeStruct((N, D), F32), jax.ShapeDtypeStruct((N, D), BF16),
                   jax.ShapeDtypeStruct((N, D), BF16), jax.ShapeDtypeStruct((N, r_width), BF16),
                   jax.ShapeDtypeStruct((N, r_width), BF16)],
        compiler_params=_params("parallel", "arbitrary"),
        name="mid_proj",
    )(og0.reshape(N, a_width), x2, p_layers, a_w_out[0].astype(BF16), row(g_ple[0]),
      w_ple_gate[0].astype(BF16), w_ple[0].astype(BF16), row(g_pre[1]), wq, wk, wv, wg,
      cos_t, sin_next, sin_prev)

    log_gamma = np.log(1.0 - 2.0 ** (-5.0 - np.arange(r_heads, dtype=np.float64)))
    ci = np.arange(R_CHUNK, dtype=np.float64)
    diff = ci[:, None] - ci[None, :]
    dmask = jnp.asarray(np.where(diff[None] >= 0, np.exp(diff[None] * log_gamma[:, None, None]), 0.0), F32)
    xi = np.exp((ci[None, :] + 1.0) * log_gamma[:, None])
    zeta = np.exp((R_CHUNK - 1.0 - ci[None, :]) * log_gamma[:, None])
    decay = np.exp(R_CHUNK * log_gamma)
    xi_b = jnp.asarray(np.broadcast_to(xi[:, :, None], (r_heads, R_CHUNK, r_v)), F32)
    zeta_b = jnp.asarray(np.broadcast_to(zeta[:, :, None], (r_heads, R_CHUNK, r_qk)), F32)
    decay_b = jnp.asarray(np.broadcast_to(decay[:, None, None], (r_heads, 1, r_v)), F32)
    rows_step = R_CHUNK * math.gcd(R_STEP_CHUNKS, L // R_CHUNK)
    cblk = lambda w: pl.BlockSpec((None, rows_step, w), lambda b, c: (b, c, 0))
    og1 = pl.pallas_call(
        functools.partial(_ret_kernel, heads=r_heads, qk_dim=r_qk, v_dim=r_v),
        grid=(B, L // rows_step),
        in_specs=[cblk(D), cblk(D), cblk(r_width), cblk(r_width),
                  _const_spec((r_heads, R_CHUNK, R_CHUNK)), _const_spec((r_heads, R_CHUNK, r_v)),
                  _const_spec((r_heads, R_CHUNK, r_qk)), _const_spec((r_heads, 1, r_v))],
        out_specs=cblk(r_width),
        out_shape=jax.ShapeDtypeStruct((B, L, r_width), BF16),
        scratch_shapes=[pltpu.VMEM((r_heads, r_qk, r_v), F32)],
        compiler_params=_params("parallel", "arbitrary"),
        name="retention",
    )(rq.reshape(B, L, D), rk.reshape(B, L, D), rv.reshape(B, L, r_width), sg1.reshape(B, L, r_width),
      dmask, xi_b, zeta_b, decay_b)

    tm2 = math.gcd(ROWS_OUT_PROJ, N)
    tk2 = lambda w: pl.BlockSpec((tm2, w), lambda i: (i, 0))
    out = pl.pallas_call(
        _out_kernel,
        grid=(N // tm2,),
        in_specs=[tk2(r_width), tk2(D), pl.BlockSpec((None, tm2, ple_dim), lambda i: (1, i, 0)),
                  _const_spec((r_width, D)), _const_spec((1, D)),
                  _const_spec((D, D)), _const_spec((ple_dim, D)), _const_spec((1, D))],
        out_specs=tk2(D),
        out_shape=jax.ShapeDtypeStruct((N, D), F32),
        compiler_params=_params("parallel"),
        name="out_proj",
    )(og1.reshape(N, r_width), h1, p_layers, r_w_out[0].astype(BF16), row(g_ple[1]),
      w_ple_gate[1].astype(BF16), w_ple[1].astype(BF16), row(g_final))
    return out.reshape(B, L, D)
```

```python
import functools
import math

import jax
import jax.numpy as jnp
import numpy as np
from jax import lax
from jax.experimental import pallas as pl
from jax.experimental.pallas import tpu as pltpu

F32 = jnp.float32
BF16 = jnp.bfloat16

NORM_EPS = 1e-6
TOPK_MAX = 256
Q_BLOCK = 256
KEY_CHUNK = 256
R_CHUNK = 128
LANES = 128
MASK_BIAS = -1e30
VALUE_BISECT_MAX = 34
VMEM_LIMIT = 56 * 1024 * 1024
HEAD_GROUPS = 4
RET_HEADS = 4
R_STEP_CHUNKS = 4
ROWS_DSA_PROJ = 1024
ROWS_MID_PROJ = 512
MID_SPLIT = 2
OUT_SPLIT = 2


def _passes_per_body(n_chunks):
    return max(1, min(4, round(6 / n_chunks)))


def _rms(x, g):
    return x * lax.rsqrt(jnp.mean(x * x, axis=-1, keepdims=True) + NORM_EPS) * g


def _dot(a, b):
    return jnp.dot(a, b, preferred_element_type=F32)


def _dot_nt(a, b):
    return lax.dot_general(a, b, (((1,), (1,)), ((), ())), preferred_element_type=F32)


def _dot_tn(a, b):
    return lax.dot_general(a, b, (((0,), (0,)), ((), ())), preferred_element_type=F32)


def _const_spec(shape):
    nd = len(shape)
    return pl.BlockSpec(shape, lambda *_: (0,) * nd, pipeline_mode=pl.Buffered(1))


def _proj0_kernel(x_ref, gpre_ref, w0_ref, gq_ref, gkv_ref, gik_ref, bik_ref, wqup_ref, wukt_ref,
                  widxq_ref, qabs_ref, qidx_ref, ka_ref, kb_ref, wi_ref, ckv_ref, sg_ref,
                  *, q_rank, kv_rank, idx_dim, heads, qk_dim, qk_scale, widx_scale):
    hn = _rms(x_ref[...], gpre_ref[...])
    z = _dot(hn.astype(BF16), w0_ref[...])
    o_kv = q_rank
    o_ki = q_rank + kv_rank
    o_g = o_ki + LANES
    cq = _rms(z[:, :o_kv], gq_ref[...]).astype(BF16)
    ckv_ref[...] = _rms(z[:, o_kv:o_ki], gkv_ref[...]).astype(BF16)
    ki = z[:, o_ki:o_g]
    lane = lax.broadcasted_iota(jnp.int32, (1, LANES), 1)
    in_k = lane < idx_dim
    mu = jnp.sum(jnp.where(in_k, ki, 0.0), axis=-1, keepdims=True) * (1.0 / idx_dim)
    d = jnp.where(in_k, ki - mu, 0.0)
    var = jnp.sum(d * d, axis=-1, keepdims=True) * (1.0 / idx_dim)
    kn = d * lax.rsqrt(var + NORM_EPS) * gik_ref[...] + bik_ref[...]
    ka_ref[...] = kn.astype(BF16)
    kb_ref[...] = pltpu.roll(kn, idx_dim, 1).astype(BF16)
    wi_ref[...] = ki * widx_scale
    g = z[:, o_g:]
    sg_ref[...] = (g * jax.nn.sigmoid(g)).astype(BF16)
    q = _dot(cq, wqup_ref[...])
    for h in range(heads):
        qh = q[:, h * qk_dim:(h + 1) * qk_dim].astype(BF16)
        qabs_ref[h] = (_dot(qh, wukt_ref[h]) * qk_scale).astype(BF16)
    qidx_ref[...] = _dot(cq, widxq_ref[...]).astype(BF16)


def _select_kernel(qidx_ref, ka_ref, kb_ref, wi_ref, bias_ref, st_ref, *, topk, heads, idx_dim, max_chunks):
    qb = pl.program_id(1)
    n_kc = (qb * Q_BLOCK) // KEY_CHUNK + 1
    kf = float(topk)
    pairs = heads // 2

    qi = qidx_ref[...]
    qst = jnp.concatenate([qi[:, j * LANES:(j + 1) * LANES] for j in range(pairs)], axis=0)
    wit = wi_ref[...].T
    w_rows = [wit[idx_dim + h:idx_dim + h + 1, :] for h in range(heads)]
    tq = qb * Q_BLOCK + lax.broadcasted_iota(jnp.int32, (1, Q_BLOCK), 1)
    n_causal = (tq + 1).astype(F32)
    small = n_causal <= kf
    row_i = lax.broadcasted_iota(jnp.int32, (KEY_CHUNK, KEY_CHUNK), 0)
    col_i = lax.broadcasted_iota(jnp.int32, (KEY_CHUNK, KEY_CHUNK), 1)
    eye = (row_i == col_i).astype(BF16)[:Q_BLOCK, :Q_BLOCK]
    lower_tri = (col_i <= row_i).astype(BF16)

    def fold8(x, op):
        return op(x.reshape(x.shape[0] // 8, 8, Q_BLOCK), axis=0)

    def variant(nc):
        chunks = [(c, c * KEY_CHUNK) for c in range(nc)]

        mn = mx = cgt = cge = None
        for c, k0 in chunks:
            sa = _dot_nt(ka_ref[k0:k0 + KEY_CHUNK, :], qst)
            sb = _dot_nt(kb_ref[k0:k0 + KEY_CHUNK, :], qst)
            acc = None
            for j in range(pairs):
                ta = jnp.maximum(sa[:, j * Q_BLOCK:(j + 1) * Q_BLOCK], 0.0) * w_rows[2 * j]
                tb = jnp.maximum(sb[:, j * Q_BLOCK:(j + 1) * Q_BLOCK], 0.0) * w_rows[2 * j + 1]
                acc = ta + tb if acc is None else acc + ta + tb
            if c == nc - 1:
                kpos = k0 + lax.broadcasted_iota(jnp.int32, (KEY_CHUNK, Q_BLOCK), 0)
                causal = kpos <= tq
                s = jnp.where(causal, acc, -jnp.inf)
                s_min = jnp.where(causal, acc, jnp.inf)
            else:
                s = s_min = acc
            st_ref[k0:k0 + KEY_CHUNK, :] = s
            parts = (fold8(s_min, jnp.min), fold8(s, jnp.max),
                     fold8(jnp.where(s > 0.0, 1.0, 0.0), jnp.sum),
                     fold8(jnp.where(s >= 0.0, 1.0, 0.0), jnp.sum))
            if mn is None:
                mn, mx, cgt, cge = parts
            else:
                mn, mx = jnp.minimum(mn, parts[0]), jnp.maximum(mx, parts[1])
                cgt, cge = cgt + parts[2], cge + parts[3]
        mn = jnp.min(mn, axis=0, keepdims=True)
        mx = jnp.max(mx, axis=0, keepdims=True)
        cgt0 = jnp.sum(cgt, axis=0, keepdims=True)
        cge0 = jnp.sum(cge, axis=0, keepdims=True)

        def count_gt(ref, thr):
            acc = None
            for _, k0 in chunks:
                part = fold8(jnp.where(ref[k0:k0 + KEY_CHUNK, :] > thr, 1.0, 0.0), jnp.sum)
                acc = part if acc is None else acc + part
            return jnp.sum(acc, axis=0, keepdims=True)

        pos = cgt0 >= kf
        zero = jnp.logical_and(jnp.logical_not(pos), cge0 >= kf)
        zero = jnp.logical_and(zero, jnp.logical_not(small))
        lo = jnp.where(pos, 0.0, jnp.where(zero, 0.0, mn * 1.0001))
        hi = jnp.where(pos, mx, 0.0)
        c_lo = jnp.where(pos, cgt0, jnp.where(zero, cgt0, n_causal))
        c_hi = jnp.where(pos, 0.0, cgt0)
        lo = jnp.where(small, -jnp.inf, lo)
        hi = jnp.where(small, -jnp.inf, hi)
        searching = jnp.logical_not(jnp.logical_or(small, zero))

        def unresolved(c_lo):
            bad = jnp.logical_and(c_lo != kf, searching)
            return (jnp.max(jnp.where(bad, 1.0, 0.0)) > 0.0).astype(jnp.int32)

        def bis_cond(carry):
            return jnp.logical_and(carry[0] < VALUE_BISECT_MAX, carry[1] > 0)

        def bisect(state):
            lo, hi, c_lo, c_hi = state
            mid = 0.5 * (lo + hi)
            c = count_gt(st_ref, mid)
            ge = c >= kf
            return (jnp.where(ge, mid, lo), jnp.where(ge, hi, mid),
                    jnp.where(ge, c, c_lo), jnp.where(ge, c_hi, c))

        def bis_body(carry):
            it, _, state = carry
            flag = unresolved(state[2])
            for _ in range(_passes_per_body(nc)):
                state = bisect(state)
            return it + _passes_per_body(nc), flag, state

        _, _, (lo, hi, c_lo, c_hi) = lax.while_loop(
            bis_cond, bis_body, (jnp.int32(0), jnp.int32(1), (lo, hi, c_lo, c_hi)))
        done = jnp.logical_or(c_lo == kf, small)
        thr = jnp.where(done, lo, hi)

        @pl.when(jnp.max(jnp.where(done, 0.0, 1.0)) > 0.0)
        def _():
            need = jnp.where(done, 0.0, kf - c_hi)
            tied_f = jnp.where(jnp.logical_or(done, zero), 0.0, 1.0)
            zero_f = jnp.where(zero, 1.0, 0.0)
            before = jnp.zeros((1, Q_BLOCK), F32)
            for _, k0 in chunks:
                s = st_ref[k0:k0 + KEY_CHUNK, :]
                band_f = (jnp.where(s > lo, 1.0, 0.0) * jnp.where(s <= hi, 1.0, 0.0) * tied_f
                          + jnp.where(s == 0.0, 1.0, 0.0) * zero_f)
                rank = _dot(lower_tri, band_f.astype(BF16)) + before
                take_f = band_f * jnp.where(rank <= need, 1.0, 0.0)
                st_ref[k0:k0 + KEY_CHUNK, :] = jnp.where(take_f > 0.5, jnp.inf, s)
                before = before + jnp.sum(fold8(band_f, jnp.sum), axis=0, keepdims=True)

        for _, k0 in chunks:
            sel_t = jnp.where(st_ref[k0:k0 + KEY_CHUNK, :] > thr, 1.0, 0.0).astype(BF16)
            sel = _dot_nt(eye, sel_t)
            bias_ref[:, k0:k0 + KEY_CHUNK] = ((sel - 1.0) * (-MASK_BIAS)).astype(BF16)
        if nc < max_chunks:
            bias_ref[:, nc * KEY_CHUNK:] = jnp.full((Q_BLOCK, (max_chunks - nc) * KEY_CHUNK), MASK_BIAS, BF16)

    for nc in range(1, max_chunks + 1):
        pl.when(n_kc == nc)(functools.partial(variant, nc))


def _attend_kernel(qabs_ref, ckv_ref, bias_ref, sg_ref, wuv_ref, og_ref, s_ref, p_ref, m_ref, acc_ref,
                   l_ref, *, heads, v_dim, max_chunks):
    n_kc = (pl.program_id(1) * Q_BLOCK) // KEY_CHUNK + 1
    sub = KEY_CHUNK // LANES
    gsz = heads // HEAD_GROUPS
    grows = gsz * Q_BLOCK

    def variant(nc):
        nk = nc * KEY_CHUNK
        q_all = jnp.concatenate([qabs_ref[h] for h in range(heads)], axis=0)
        for c in range(nc):
            k0 = c * KEY_CHUNK
            s = _dot_nt(q_all, ckv_ref[k0:k0 + KEY_CHUNK, :])
            bias = bias_ref[:, k0:k0 + KEY_CHUNK].astype(F32)
            for h in range(heads):
                rows = slice(h * Q_BLOCK, (h + 1) * Q_BLOCK)
                sh = s[rows, :] + bias
                s_ref[rows, k0:k0 + KEY_CHUNK] = sh
                mh = sh[:, :LANES]
                for t in range(1, sub):
                    mh = jnp.maximum(mh, sh[:, t * LANES:(t + 1) * LANES])
                m_ref[rows, :] = mh if c == 0 else jnp.maximum(m_ref[rows, :], mh)
        for g in range(HEAD_GROUPS):
            for i in range(gsz):
                rows = slice((g * gsz + i) * Q_BLOCK, (g * gsz + i + 1) * Q_BLOCK)
                m = jnp.broadcast_to(jnp.max(m_ref[rows, :], axis=-1, keepdims=True), (Q_BLOCK, LANES))
                lsum = None
                for t in range(nc * sub):
                    pt = jnp.exp2(s_ref[rows, t * LANES:(t + 1) * LANES] - m)
                    p_ref[rows, t * LANES:(t + 1) * LANES] = pt.astype(BF16)
                    lsum = pt if lsum is None else lsum + pt
                l_ref[rows, :] = lsum
            acc_ref[g * grows:(g + 1) * grows, :] = _dot(p_ref[g * grows:(g + 1) * grows, :nk], ckv_ref[:nk, :])

    for nc in range(1, max_chunks + 1):
        pl.when(n_kc == nc)(functools.partial(variant, nc))

    inv_l = 1.0 / jnp.sum(l_ref[...], axis=-1, keepdims=True)
    for h in range(heads):
        oh = (acc_ref[h * Q_BLOCK:(h + 1) * Q_BLOCK, :] * inv_l[h * Q_BLOCK:(h + 1) * Q_BLOCK]).astype(BF16)
        ov = _dot(oh, wuv_ref[h])
        og_ref[:, h * v_dim:(h + 1) * v_dim] = (
            ov * sg_ref[:, h * v_dim:(h + 1) * v_dim].astype(F32)).astype(BF16)


def _rotate(x, cos, sin_next, sin_prev):
    nxt = pltpu.roll(x, LANES - 1, 1)
    prv = pltpu.roll(x, 1, 1)
    return x * cos + nxt * sin_next + prv * sin_prev


def _mid_kernel(og_ref, x_ref, p_ref, wout_ref, gple_ref, wpg_ref, wple_ref, gpre_ref,
                wq_ref, wk_ref, wv_ref, wg_ref, cos_ref, sn_ref, sp_ref,
                h_ref, q_ref, k_ref, v_ref, sg_ref, *, k_scale, qk_dim):
    n_rows = x_ref.shape[0]
    halves = [slice(i * (n_rows // MID_SPLIT), (i + 1) * (n_rows // MID_SPLIT)) for i in range(MID_SPLIT)]
    hs = [x_ref[r, :] + _dot(og_ref[r, :], wout_ref[...]) for r in halves]
    gates = [jax.nn.sigmoid(_dot(_rms(h, gple_ref[...]).astype(BF16), wpg_ref[...])) for h in hs]
    hs = [h + _dot(p_ref[r, :].astype(BF16), wple_ref[...]) * gt for h, r, gt in zip(hs, halves, gates)]
    for h, r in zip(hs, halves):
        h_ref[r, :] = h
    hbs = [_rms(h, gpre_ref[...]).astype(BF16) for h in hs]
    for hb, r in zip(hbs, halves):
        q = _dot(hb, wq_ref[...])
        k = _dot(hb, wk_ref[...]) * k_scale
        for j in range(q.shape[1] // LANES):
            t = (j * LANES) % qk_dim
            cs = cos_ref[r, t:t + LANES]
            sn = sn_ref[r, t:t + LANES]
            sp = sp_ref[r, t:t + LANES]
            q_ref[r, j * LANES:(j + 1) * LANES] = _rotate(q[:, j * LANES:(j + 1) * LANES], cs, sn, sp).astype(BF16)
            k_ref[r, j * LANES:(j + 1) * LANES] = _rotate(k[:, j * LANES:(j + 1) * LANES], cs, sn, sp).astype(BF16)
        v_ref[r, :] = _dot(hb, wv_ref[...]).astype(BF16)
        g = _dot(hb, wg_ref[...])
        sg_ref[r, :] = (g * jax.nn.sigmoid(g)).astype(BF16)


def _ret_kernel(q_ref, k_ref, v_ref, sg_ref, dmask_ref, xi_ref, zeta_ref, decay_ref,
                h_ref, p_ref, wout_ref, gple_ref, wpg_ref, wple_ref, gfin_ref, o_ref,
                state_ref, og_ref, *, heads, qk_dim, v_dim):
    @pl.when(pl.program_id(1) == 0)
    def _():
        state_ref[...] = jnp.zeros(state_ref.shape, F32)

    for c in range(q_ref.shape[0] // R_CHUNK):
        r = slice(c * R_CHUNK, (c + 1) * R_CHUNK)
        for h in range(heads):
            qh = q_ref[r, h * qk_dim:(h + 1) * qk_dim]
            kh = k_ref[r, h * qk_dim:(h + 1) * qk_dim]
            vh = v_ref[r, h * v_dim:(h + 1) * v_dim]
            s = _dot_nt(qh, kh) * dmask_ref[h]
            o = _dot(s.astype(BF16), vh) + _dot(qh, state_ref[h].astype(BF16)) * xi_ref[h]
            kz = (kh.astype(F32) * zeta_ref[h]).astype(BF16)
            state_ref[h] = state_ref[h] * decay_ref[h] + _dot_tn(kz, vh)
            mu = jnp.mean(o, axis=-1, keepdims=True)
            d = o - mu
            var = jnp.mean(d * d, axis=-1, keepdims=True)
            on = d * lax.rsqrt(var + NORM_EPS)
            og_ref[r, h * v_dim:(h + 1) * v_dim] = (
                on * sg_ref[r, h * v_dim:(h + 1) * v_dim].astype(F32)).astype(BF16)

    n_rows = h_ref.shape[0]
    parts = [slice(i * (n_rows // OUT_SPLIT), (i + 1) * (n_rows // OUT_SPLIT)) for i in range(OUT_SPLIT)]
    hs = [h_ref[r, :] + _dot(og_ref[r, :], wout_ref[...]) for r in parts]
    gates = [jax.nn.sigmoid(_dot(_rms(h, gple_ref[...]).astype(BF16), wpg_ref[...])) for h in hs]
    for h, r, gt in zip(hs, parts, gates):
        o_ref[r, :] = _rms(h + _dot(p_ref[r, :].astype(BF16), wple_ref[...]) * gt, gfin_ref[...])


def _params(*sem):
    return pltpu.CompilerParams(dimension_semantics=sem, vmem_limit_bytes=VMEM_LIMIT)


def kernel(x, p, g_pre, a_w_in, a_g_q, a_g_kv, a_w_q_up, a_w_idx_q, a_g_ik, a_b_ik, a_w_uk, a_w_uv,
           a_w_out, r_w_in, r_w_out, w_ple_gate, g_ple, w_ple, g_final):
    B, L, D = x.shape
    N = B * L
    q_rank = a_g_q.shape[-1]
    kv_rank = a_g_kv.shape[-1]
    idx_dim = a_g_ik.shape[-1]
    heads, _, qk_dim = a_w_uk.shape[1:]
    v_dim = a_w_uv.shape[-1]
    idx_heads = a_w_idx_q.shape[-1] // idx_dim
    a_width = heads * v_dim
    ple_dim = p.shape[-1]
    assert idx_heads == heads and 2 * idx_dim == LANES and L % KEY_CHUNK == 0
    assert KEY_CHUNK % Q_BLOCK == 0 and L % Q_BLOCK == 0 and heads % HEAD_GROUPS == 0
    assert q_rank % LANES == 0 and kv_rank % LANES == 0 and a_w_in.shape[0] == 1 and r_w_in.shape[0] == 1
    topk = min(TOPK_MAX, L // 4)
    x2 = x.reshape(N, D)
    p_layers = p.reshape(p.shape[0], N, ple_dim)
    row = lambda a: a.reshape(1, -1).astype(F32)

    w_in = a_w_in[0]
    o_ki = q_rank + kv_rank
    o_g = o_ki + idx_dim + idx_heads
    w_ki = jnp.pad(w_in[:, o_ki:o_g], ((0, 0), (0, LANES - (o_g - o_ki))))
    w0 = jnp.concatenate([w_in[:, :o_ki], w_ki, w_in[:, o_g:]], axis=1).astype(BF16)
    pad_k = lambda a: jnp.pad(a.reshape(1, -1).astype(F32), ((0, 0), (0, LANES - idx_dim)))
    wukt = jnp.swapaxes(a_w_uk[0], 1, 2).astype(BF16)
    tm0 = math.gcd(ROWS_DSA_PROJ, N)
    n0 = w0.shape[1]
    outs0 = pl.pallas_call(
        functools.partial(_proj0_kernel, q_rank=q_rank, kv_rank=kv_rank, idx_dim=idx_dim, heads=heads,
                          qk_dim=qk_dim, qk_scale=qk_dim ** -0.5 * math.log2(math.e),
                          widx_scale=idx_heads ** -0.5 * idx_dim ** -0.5),
        grid=(N // tm0,),
        in_specs=[pl.BlockSpec((tm0, D), lambda i: (i, 0)), _const_spec((1, D)), _const_spec((D, n0)),
                  _const_spec((1, q_rank)), _const_spec((1, kv_rank)), _const_spec((1, LANES)),
                  _const_spec((1, LANES)), _const_spec((q_rank, heads * qk_dim)),
                  _const_spec((heads, qk_dim, kv_rank)), _const_spec((q_rank, idx_heads * idx_dim))],
        out_specs=[pl.BlockSpec((heads, tm0, kv_rank), lambda i: (0, i, 0)),
                   pl.BlockSpec((tm0, idx_heads * idx_dim), lambda i: (i, 0)),
                   pl.BlockSpec((tm0, LANES), lambda i: (i, 0)),
                   pl.BlockSpec((tm0, LANES), lambda i: (i, 0)),
                   pl.BlockSpec((tm0, LANES), lambda i: (i, 0)),
                   pl.BlockSpec((tm0, kv_rank), lambda i: (i, 0)),
                   pl.BlockSpec((tm0, a_width), lambda i: (i, 0))],
        out_shape=[jax.ShapeDtypeStruct((heads, N, kv_rank), BF16),
                   jax.ShapeDtypeStruct((N, idx_heads * idx_dim), BF16),
                   jax.ShapeDtypeStruct((N, LANES), BF16),
                   jax.ShapeDtypeStruct((N, LANES), BF16),
                   jax.ShapeDtypeStruct((N, LANES), F32),
                   jax.ShapeDtypeStruct((N, kv_rank), BF16),
                   jax.ShapeDtypeStruct((N, a_width), BF16)],
        compiler_params=_params("parallel"),
        name="dsa_proj",
    )(x2, row(g_pre[0]), w0, row(a_g_q[0]), row(a_g_kv[0]), pad_k(a_g_ik[0]), pad_k(a_b_ik[0]),
      a_w_q_up[0].astype(BF16), wukt, a_w_idx_q[0].astype(BF16))
    qabs, qidx, ka, kb, wi, ckv, sg0 = outs0

    nb = L // Q_BLOCK
    nkc = L // KEY_CHUNK
    b3 = lambda a: a.reshape(B, L, a.shape[-1])
    qblk = lambda w: pl.BlockSpec((None, Q_BLOCK, w), lambda b, i: (b, i, 0))
    seq = lambda w: pl.BlockSpec((None, L, w), lambda b, i: (b, 0, 0))
    bias = pl.pallas_call(
        functools.partial(_select_kernel, topk=topk, heads=heads, idx_dim=idx_dim, max_chunks=nkc),
        grid=(B, nb),
        in_specs=[qblk(idx_heads * idx_dim), seq(LANES), seq(LANES), qblk(LANES)],
        out_specs=qblk(L),
        out_shape=jax.ShapeDtypeStruct((B, L, L), BF16),
        scratch_shapes=[pltpu.VMEM((L, Q_BLOCK), F32)],
        compiler_params=_params("parallel", "arbitrary"),
        name="dsa_select",
    )(b3(qidx), b3(ka), b3(kb), b3(wi))
    og0 = pl.pallas_call(
        functools.partial(_attend_kernel, heads=heads, v_dim=v_dim, max_chunks=nkc),
        grid=(B, nb),
        in_specs=[pl.BlockSpec((heads, None, Q_BLOCK, kv_rank), lambda b, i: (0, b, i, 0)),
                  seq(kv_rank), qblk(L), qblk(a_width), _const_spec((heads, kv_rank, v_dim))],
        out_specs=qblk(a_width),
        out_shape=jax.ShapeDtypeStruct((B, L, a_width), BF16),
        scratch_shapes=[pltpu.VMEM((heads * Q_BLOCK, L), F32),
                        pltpu.VMEM((heads * Q_BLOCK, L), BF16),
                        pltpu.VMEM((heads * Q_BLOCK, LANES), F32),
                        pltpu.VMEM((heads * Q_BLOCK, kv_rank), F32),
                        pltpu.VMEM((heads * Q_BLOCK, LANES), F32)],
        compiler_params=_params("parallel", "arbitrary"),
        name="dsa_attend",
    )(qabs.reshape(heads, B, L, kv_rank), b3(ckv), bias, b3(sg0), a_w_uv[0].astype(BF16))

    r_heads = RET_HEADS
    r_qk = D // r_heads
    r_v = 2 * D // r_heads
    r_width = r_heads * r_v
    assert r_w_in.shape[-1] == 2 * D + 2 * r_width and r_qk % LANES == 0 and L % R_CHUNK == 0
    wr = r_w_in[0].astype(BF16)
    wq, wk, wv, wg = wr[:, :D], wr[:, D:2 * D], wr[:, 2 * D:2 * D + r_width], wr[:, 2 * D + r_width:]
    angle = 1.0 / (10000.0 ** np.linspace(0.0, 1.0, r_qk // 2))
    theta = np.arange(L, dtype=np.float64)[:, None] * np.repeat(angle, 2)[None, :]
    even = (np.arange(r_qk) % 2 == 0)[None, :]
    cos_t = jnp.asarray(np.cos(theta), F32)
    sin_next = jnp.asarray(np.where(even, -np.sin(theta), 0.0), F32)
    sin_prev = jnp.asarray(np.where(even, 0.0, np.sin(theta)), F32)
    tm1 = math.gcd(ROWS_MID_PROJ, L)
    tok = lambda w: pl.BlockSpec((tm1, w), lambda b, i: (b * (L // tm1) + i, 0))
    tab = pl.BlockSpec((tm1, r_qk), lambda b, i: (i, 0))
    h1, rq, rk, rv, sg1 = pl.pallas_call(
        functools.partial(_mid_kernel, k_scale=r_qk ** -0.5, qk_dim=r_qk),
        grid=(B, L // tm1),
        in_specs=[tok(a_width), tok(D),
                  pl.BlockSpec((None, tm1, ple_dim), lambda b, i: (0, b * (L // tm1) + i, 0)),
                  _const_spec((a_width, D)), _const_spec((1, D)),
                  _const_spec((D, D)), _const_spec((ple_dim, D)), _const_spec((1, D)),
                  _const_spec((D, D)), _const_spec((D, D)), _const_spec((D, r_width)),
                  _const_spec((D, r_width)), tab, tab, tab],
        out_specs=[tok(D), tok(D), tok(D), tok(r_width), tok(r_width)],
        out_shape=[jax.ShapeDtypeStruct((N, D), F32), jax.ShapeDtypeStruct((N, D), BF16),
                   jax.ShapeDtypeStruct((N, D), BF16), jax.ShapeDtypeStruct((N, r_width), BF16),
                   jax.ShapeDtypeStruct((N, r_width), BF16)],
        compiler_params=_params("parallel", "arbitrary"),
        name="mid_proj",
    )(og0.reshape(N, a_width), x2, p_layers, a_w_out[0].astype(BF16), row(g_ple[0]),
      w_ple_gate[0].astype(BF16), w_ple[0].astype(BF16), row(g_pre[1]), wq, wk, wv, wg,
      cos_t, sin_next, sin_prev)

    log_gamma = np.log(1.0 - 2.0 ** (-5.0 - np.arange(r_heads, dtype=np.float64)))
    ci = np.arange(R_CHUNK, dtype=np.float64)
    diff = ci[:, None] - ci[None, :]
    dmask = jnp.asarray(np.where(diff[None] >= 0, np.exp(diff[None] * log_gamma[:, None, None]), 0.0), F32)
    xi = np.exp((ci[None, :] + 1.0) * log_gamma[:, None])
    zeta = np.exp((R_CHUNK - 1.0 - ci[None, :]) * log_gamma[:, None])
    decay = np.exp(R_CHUNK * log_gamma)
    xi_b = jnp.asarray(np.broadcast_to(xi[:, :, None], (r_heads, R_CHUNK, r_v)), F32)
    zeta_b = jnp.asarray(np.broadcast_to(zeta[:, :, None], (r_heads, R_CHUNK, r_qk)), F32)
    decay_b = jnp.asarray(np.broadcast_to(decay[:, None, None], (r_heads, 1, r_v)), F32)
    rows_step = R_CHUNK * math.gcd(R_STEP_CHUNKS, L // R_CHUNK)
    cblk = lambda w: pl.BlockSpec((None, rows_step, w), lambda b, c: (b, c, 0))
    return pl.pallas_call(
        functools.partial(_ret_kernel, heads=r_heads, qk_dim=r_qk, v_dim=r_v),
        grid=(B, L // rows_step),
        in_specs=[cblk(D), cblk(D), cblk(r_width), cblk(r_width),
                  _const_spec((r_heads, R_CHUNK, R_CHUNK)), _const_spec((r_heads, R_CHUNK, r_v)),
                  _const_spec((r_heads, R_CHUNK, r_qk)), _const_spec((r_heads, 1, r_v)),
                  cblk(D),
                  pl.BlockSpec((None, None, rows_step, ple_dim), lambda b, c: (1, b, c, 0)),
                  _const_spec((r_width, D)), _const_spec((1, D)),
                  _const_spec((D, D)), _const_spec((ple_dim, D)), _const_spec((1, D))],
        out_specs=cblk(D),
        out_shape=jax.ShapeDtypeStruct((B, L, D), F32),
        scratch_shapes=[pltpu.VMEM((r_heads, r_qk, r_v), F32),
                        pltpu.VMEM((rows_step, r_width), BF16)],
        compiler_params=_params("parallel", "arbitrary"),
        name="retention_out",
    )(rq.reshape(B, L, D), rk.reshape(B, L, D), rv.reshape(B, L, r_width), sg1.reshape(B, L, r_width),
      dmask, xi_b, zeta_b, decay_b, h1.reshape(B, L, D), p_layers.reshape(p.shape[0], B, L, ple_dim),
      r_w_out[0].astype(BF16), row(g_ple[1]), w_ple_gate[1].astype(BF16), w_ple[1].astype(BF16),
      row(g_final))
```

```python
import functools
import math

import jax
import jax.numpy as jnp
import numpy as np
from jax import lax
from jax.experimental import pallas as pl
from jax.experimental.pallas import tpu as pltpu

F32 = jnp.float32
BF16 = jnp.bfloat16

NORM_EPS = 1e-6
TOPK_MAX = 256
Q_BLOCK = 256
KEY_CHUNK = 256
R_CHUNK = 128
LANES = 128
MASK_BIAS = -1e30
VALUE_BISECT_MAX = 34
VMEM_LIMIT = 56 * 1024 * 1024
HEAD_GROUPS = 2
RET_HEADS = 4
R_STEP_CHUNKS = 8
ROWS_DSA_PROJ = 1024
ROWS_MID_PROJ = 512
ROWS_OUT_PROJ = 1024
MID_SPLIT = 2
OUT_SPLIT = 4


def _passes_per_body(n_chunks):
    return max(1, min(4, round(6 / n_chunks)))


def _rms(x, g):
    return x * lax.rsqrt(jnp.mean(x * x, axis=-1, keepdims=True) + NORM_EPS) * g


def _dot(a, b):
    return jnp.dot(a, b, preferred_element_type=F32)


def _dot_nt(a, b):
    return lax.dot_general(a, b, (((1,), (1,)), ((), ())), preferred_element_type=F32)


def _dot_tn(a, b):
    return lax.dot_general(a, b, (((0,), (0,)), ((), ())), preferred_element_type=F32)


def _const_spec(shape):
    nd = len(shape)
    return pl.BlockSpec(shape, lambda *_: (0,) * nd, pipeline_mode=pl.Buffered(1))


def _proj0_kernel(x_ref, gpre_ref, w0_ref, gq_ref, gkv_ref, gik_ref, bik_ref, wqup_ref, wukt_ref,
                  widxq_ref, qabs_ref, qidx_ref, ka_ref, kb_ref, wi_ref, ckv_ref, sg_ref,
                  *, q_rank, kv_rank, idx_dim, heads, qk_dim, qk_scale, widx_scale):
    hn = _rms(x_ref[...], gpre_ref[...])
    z = _dot(hn.astype(BF16), w0_ref[...])
    o_kv = q_rank
    o_ki = q_rank + kv_rank
    o_g = o_ki + LANES
    cq = _rms(z[:, :o_kv], gq_ref[...]).astype(BF16)
    ckv_ref[...] = _rms(z[:, o_kv:o_ki], gkv_ref[...]).astype(BF16)
    ki = z[:, o_ki:o_g]
    lane = lax.broadcasted_iota(jnp.int32, (1, LANES), 1)
    in_k = lane < idx_dim
    mu = jnp.sum(jnp.where(in_k, ki, 0.0), axis=-1, keepdims=True) * (1.0 / idx_dim)
    d = jnp.where(in_k, ki - mu, 0.0)
    var = jnp.sum(d * d, axis=-1, keepdims=True) * (1.0 / idx_dim)
    kn = d * lax.rsqrt(var + NORM_EPS) * gik_ref[...] + bik_ref[...]
    ka_ref[...] = kn.astype(BF16)
    kb_ref[...] = pltpu.roll(kn, idx_dim, 1).astype(BF16)
    wi_ref[...] = ki * widx_scale
    g = z[:, o_g:]
    sg_ref[...] = (g * jax.nn.sigmoid(g)).astype(BF16)
    q = _dot(cq, wqup_ref[...])
    for h in range(heads):
        qh = q[:, h * qk_dim:(h + 1) * qk_dim].astype(BF16)
        qabs_ref[h] = (_dot(qh, wukt_ref[h]) * qk_scale).astype(BF16)
    qidx_ref[...] = _dot(cq, widxq_ref[...]).astype(BF16)


def _select_kernel(qidx_ref, ka_ref, kb_ref, wi_ref, bias_ref, st_ref, *, topk, heads, idx_dim, max_chunks):
    qb = pl.program_id(1)
    n_kc = (qb * Q_BLOCK) // KEY_CHUNK + 1
    kf = float(topk)
    pairs = heads // 2

    qi = qidx_ref[...]
    qst = jnp.concatenate([qi[:, j * LANES:(j + 1) * LANES] for j in range(pairs)], axis=0)
    wit = wi_ref[...].T
    w_rows = [wit[idx_dim + h:idx_dim + h + 1, :] for h in range(heads)]
    tq = qb * Q_BLOCK + lax.broadcasted_iota(jnp.int32, (1, Q_BLOCK), 1)
    n_causal = (tq + 1).astype(F32)
    small = n_causal <= kf
    row_i = lax.broadcasted_iota(jnp.int32, (KEY_CHUNK, KEY_CHUNK), 0)
    col_i = lax.broadcasted_iota(jnp.int32, (KEY_CHUNK, KEY_CHUNK), 1)
    eye = (row_i == col_i).astype(BF16)[:Q_BLOCK, :Q_BLOCK]
    lower_tri = (col_i <= row_i).astype(BF16)

    def fold8(x, op):
        return op(x.reshape(x.shape[0] // 8, 8, Q_BLOCK), axis=0)

    def variant(nc):
        chunks = [(c, c * KEY_CHUNK) for c in range(nc)]

        mn = mx = cgt = cge = None
        for c, k0 in chunks:
            sa = _dot_nt(ka_ref[k0:k0 + KEY_CHUNK, :], qst)
            sb = _dot_nt(kb_ref[k0:k0 + KEY_CHUNK, :], qst)
            acc = None
            for j in range(pairs):
                ta = jnp.maximum(sa[:, j * Q_BLOCK:(j + 1) * Q_BLOCK], 0.0) * w_rows[2 * j]
                tb = jnp.maximum(sb[:, j * Q_BLOCK:(j + 1) * Q_BLOCK], 0.0) * w_rows[2 * j + 1]
                acc = ta + tb if acc is None else acc + ta + tb
            if c == nc - 1:
                kpos = k0 + lax.broadcasted_iota(jnp.int32, (KEY_CHUNK, Q_BLOCK), 0)
                causal = kpos <= tq
                s = jnp.where(causal, acc, -jnp.inf)
                s_min = jnp.where(causal, acc, jnp.inf)
            else:
                s = s_min = acc
            st_ref[k0:k0 + KEY_CHUNK, :] = s
            parts = (fold8(s_min, jnp.min), fold8(s, jnp.max),
                     fold8(jnp.where(s > 0.0, 1.0, 0.0), jnp.sum),
                     fold8(jnp.where(s >= 0.0, 1.0, 0.0), jnp.sum))
            if mn is None:
                mn, mx, cgt, cge = parts
            else:
                mn, mx = jnp.minimum(mn, parts[0]), jnp.maximum(mx, parts[1])
                cgt, cge = cgt + parts[2], cge + parts[3]
        mn = jnp.min(mn, axis=0, keepdims=True)
        mx = jnp.max(mx, axis=0, keepdims=True)
        cgt0 = jnp.sum(cgt, axis=0, keepdims=True)
        cge0 = jnp.sum(cge, axis=0, keepdims=True)

        def count_gt(ref, thr):
            acc = None
            for _, k0 in chunks:
                part = fold8(jnp.where(ref[k0:k0 + KEY_CHUNK, :] > thr, 1.0, 0.0), jnp.sum)
                acc = part if acc is None else acc + part
            return jnp.sum(acc, axis=0, keepdims=True)

        pos = cgt0 >= kf
        zero = jnp.logical_and(jnp.logical_not(pos), cge0 >= kf)
        zero = jnp.logical_and(zero, jnp.logical_not(small))
        lo = jnp.where(pos, 0.0, jnp.where(zero, 0.0, mn * 1.0001))
        hi = jnp.where(pos, mx, 0.0)
        c_lo = jnp.where(pos, cgt0, jnp.where(zero, cgt0, n_causal))
        c_hi = jnp.where(pos, 0.0, cgt0)
        lo = jnp.where(small, -jnp.inf, lo)
        hi = jnp.where(small, -jnp.inf, hi)
        searching = jnp.logical_not(jnp.logical_or(small, zero))

        def unresolved(c_lo):
            bad = jnp.logical_and(c_lo != kf, searching)
            return (jnp.max(jnp.where(bad, 1.0, 0.0)) > 0.0).astype(jnp.int32)

        def bis_cond(carry):
            return jnp.logical_and(carry[0] < VALUE_BISECT_MAX, carry[1] > 0)

        def bisect(state):
            lo, hi, c_lo, c_hi = state
            mid = 0.5 * (lo + hi)
            c = count_gt(st_ref, mid)
            ge = c >= kf
            return (jnp.where(ge, mid, lo), jnp.where(ge, hi, mid),
                    jnp.where(ge, c, c_lo), jnp.where(ge, c_hi, c))

        def bis_body(carry):
            it, _, state = carry
            flag = unresolved(state[2])
            for _ in range(_passes_per_body(nc)):
                state = bisect(state)
            return it + _passes_per_body(nc), flag, state

        _, _, (lo, hi, c_lo, c_hi) = lax.while_loop(
            bis_cond, bis_body, (jnp.int32(0), jnp.int32(1), (lo, hi, c_lo, c_hi)))
        done = jnp.logical_or(c_lo == kf, small)
        thr = jnp.where(done, lo, hi)

        @pl.when(jnp.max(jnp.where(done, 0.0, 1.0)) > 0.0)
        def _():
            need = jnp.where(done, 0.0, kf - c_hi)
            tied_f = jnp.where(jnp.logical_or(done, zero), 0.0, 1.0)
            zero_f = jnp.where(zero, 1.0, 0.0)
            before = jnp.zeros((1, Q_BLOCK), F32)
            for _, k0 in chunks:
                s = st_ref[k0:k0 + KEY_CHUNK, :]
                band_f = (jnp.where(s > lo, 1.0, 0.0) * jnp.where(s <= hi, 1.0, 0.0) * tied_f
                          + jnp.where(s == 0.0, 1.0, 0.0) * zero_f)
                rank = _dot(lower_tri, band_f.astype(BF16)) + before
                take_f = band_f * jnp.where(rank <= need, 1.0, 0.0)
                st_ref[k0:k0 + KEY_CHUNK, :] = jnp.where(take_f > 0.5, jnp.inf, s)
                before = before + jnp.sum(fold8(band_f, jnp.sum), axis=0, keepdims=True)

        for _, k0 in chunks:
            sel_t = jnp.where(st_ref[k0:k0 + KEY_CHUNK, :] > thr, 1.0, 0.0).astype(BF16)
            sel = _dot_nt(eye, sel_t)
            bias_ref[:, k0:k0 + KEY_CHUNK] = ((sel - 1.0) * (-MASK_BIAS)).astype(BF16)
        if nc < max_chunks:
            bias_ref[:, nc * KEY_CHUNK:] = jnp.full((Q_BLOCK, (max_chunks - nc) * KEY_CHUNK), MASK_BIAS, BF16)

    for nc in range(1, max_chunks + 1):
        pl.when(n_kc == nc)(functools.partial(variant, nc))


def _attend_kernel(qabs_ref, ckv_ref, bias_ref, sg_ref, wuv_ref, og_ref, s_ref, p_ref, m_ref, acc_ref,
                   l_ref, *, heads, v_dim, max_chunks):
    n_kc = (pl.program_id(1) * Q_BLOCK) // KEY_CHUNK + 1
    sub = KEY_CHUNK // LANES
    gsz = heads // HEAD_GROUPS
    grows = gsz * Q_BLOCK

    def variant(nc):
        nk = nc * KEY_CHUNK
        q_all = jnp.concatenate([qabs_ref[h] for h in range(heads)], axis=0)
        for c in range(nc):
            k0 = c * KEY_CHUNK
            s = _dot_nt(q_all, ckv_ref[k0:k0 + KEY_CHUNK, :])
            bias = bias_ref[:, k0:k0 + KEY_CHUNK].astype(F32)
            for h in range(heads):
                rows = slice(h * Q_BLOCK, (h + 1) * Q_BLOCK)
                sh = s[rows, :] + bias
                s_ref[rows, k0:k0 + KEY_CHUNK] = sh
                mh = sh[:, :LANES]
                for t in range(1, sub):
                    mh = jnp.maximum(mh, sh[:, t * LANES:(t + 1) * LANES])
                m_ref[rows, :] = mh if c == 0 else jnp.maximum(m_ref[rows, :], mh)
        for g in range(HEAD_GROUPS):
            for i in range(gsz):
                rows = slice((g * gsz + i) * Q_BLOCK, (g * gsz + i + 1) * Q_BLOCK)
                m = jnp.broadcast_to(jnp.max(m_ref[rows, :], axis=-1, keepdims=True), (Q_BLOCK, LANES))
                lsum = None
                for t in range(nc * sub):
                    pt = jnp.exp2(s_ref[rows, t * LANES:(t + 1) * LANES] - m)
                    p_ref[rows, t * LANES:(t + 1) * LANES] = pt.astype(BF16)
                    lsum = pt if lsum is None else lsum + pt
                l_ref[rows, :] = lsum
            acc_ref[g * grows:(g + 1) * grows, :] = _dot(p_ref[g * grows:(g + 1) * grows, :nk], ckv_ref[:nk, :])

    for nc in range(1, max_chunks + 1):
        pl.when(n_kc == nc)(functools.partial(variant, nc))

    inv_l = 1.0 / jnp.sum(l_ref[...], axis=-1, keepdims=True)
    for h in range(heads):
        oh = (acc_ref[h * Q_BLOCK:(h + 1) * Q_BLOCK, :] * inv_l[h * Q_BLOCK:(h + 1) * Q_BLOCK]).astype(BF16)
        ov = _dot(oh, wuv_ref[h])
        og_ref[:, h * v_dim:(h + 1) * v_dim] = (
            ov * sg_ref[:, h * v_dim:(h + 1) * v_dim].astype(F32)).astype(BF16)


def _rotate(x, cos, sin_next, sin_prev):
    nxt = pltpu.roll(x, LANES - 1, 1)
    prv = pltpu.roll(x, 1, 1)
    return x * cos + nxt * sin_next + prv * sin_prev


def _mid_kernel(og_ref, x_ref, p_ref, wout_ref, gple_ref, wpg_ref, wple_ref, gpre_ref,
                wq_ref, wk_ref, wv_ref, wg_ref, cos_ref, sn_ref, sp_ref,
                h_ref, q_ref, k_ref, v_ref, sg_ref, *, k_scale, qk_dim):
    n_rows = x_ref.shape[0]
    halves = [slice(i * (n_rows // MID_SPLIT), (i + 1) * (n_rows // MID_SPLIT)) for i in range(MID_SPLIT)]
    hs = [x_ref[r, :] + _dot(og_ref[r, :], wout_ref[...]) for r in halves]
    gates = [jax.nn.sigmoid(_dot(_rms(h, gple_ref[...]).astype(BF16), wpg_ref[...])) for h in hs]
    hs = [h + _dot(p_ref[r, :].astype(BF16), wple_ref[...]) * gt for h, r, gt in zip(hs, halves, gates)]
    for h, r in zip(hs, halves):
        h_ref[r, :] = h
    hbs = [_rms(h, gpre_ref[...]).astype(BF16) for h in hs]
    for hb, r in zip(hbs, halves):
        q = _dot(hb, wq_ref[...])
        k = _dot(hb, wk_ref[...]) * k_scale
        for j in range(q.shape[1] // LANES):
            t = (j * LANES) % qk_dim
            cs = cos_ref[r, t:t + LANES]
            sn = sn_ref[r, t:t + LANES]
            sp = sp_ref[r, t:t + LANES]
            q_ref[r, j * LANES:(j + 1) * LANES] = _rotate(q[:, j * LANES:(j + 1) * LANES], cs, sn, sp).astype(BF16)
            k_ref[r, j * LANES:(j + 1) * LANES] = _rotate(k[:, j * LANES:(j + 1) * LANES], cs, sn, sp).astype(BF16)
        v_ref[r, :] = _dot(hb, wv_ref[...]).astype(BF16)
        g = _dot(hb, wg_ref[...])
        sg_ref[r, :] = (g * jax.nn.sigmoid(g)).astype(BF16)


def _ret_kernel(q_ref, k_ref, v_ref, sg_ref, dmask_ref, xi_ref, zeta_ref, decay_ref, og_ref,
                state_ref, *, heads, qk_dim, v_dim):
    @pl.when(pl.program_id(1) == 0)
    def _():
        state_ref[...] = jnp.zeros(state_ref.shape, F32)

    for c in range(q_ref.shape[0] // R_CHUNK):
        r = slice(c * R_CHUNK, (c + 1) * R_CHUNK)
        for h in range(heads):
            qh = q_ref[r, h * qk_dim:(h + 1) * qk_dim]
            kh = k_ref[r, h * qk_dim:(h + 1) * qk_dim]
            vh = v_ref[r, h * v_dim:(h + 1) * v_dim]
            s = _dot_nt(qh, kh) * dmask_ref[h]
            o = _dot(s.astype(BF16), vh) + _dot(qh, state_ref[h].astype(BF16)) * xi_ref[h]
            kz = (kh.astype(F32) * zeta_ref[h]).astype(BF16)
            state_ref[h] = state_ref[h] * decay_ref[h] + _dot_tn(kz, vh)
            mu = jnp.mean(o, axis=-1, keepdims=True)
            d = o - mu
            var = jnp.mean(d * d, axis=-1, keepdims=True)
            on = d * lax.rsqrt(var + NORM_EPS)
            og_ref[r, h * v_dim:(h + 1) * v_dim] = (
                on * sg_ref[r, h * v_dim:(h + 1) * v_dim].astype(F32)).astype(BF16)


def _out_kernel(og_ref, h_ref, p_ref, wout_ref, gple_ref, wpg_ref, wple_ref, gfin_ref, o_ref):
    n_rows = h_ref.shape[0]
    parts = [slice(i * (n_rows // OUT_SPLIT), (i + 1) * (n_rows // OUT_SPLIT)) for i in range(OUT_SPLIT)]
    hs = [h_ref[r, :] + _dot(og_ref[r, :], wout_ref[...]) for r in parts]
    gates = [jax.nn.sigmoid(_dot(_rms(h, gple_ref[...]).astype(BF16), wpg_ref[...])) for h in hs]
    for h, r, gt in zip(hs, parts, gates):
        o_ref[r, :] = _rms(h + _dot(p_ref[r, :].astype(BF16), wple_ref[...]) * gt, gfin_ref[...])


def _params(*sem):
    return pltpu.CompilerParams(dimension_semantics=sem, vmem_limit_bytes=VMEM_LIMIT)


def kernel(x, p, g_pre, a_w_in, a_g_q, a_g_kv, a_w_q_up, a_w_idx_q, a_g_ik, a_b_ik, a_w_uk, a_w_uv,
           a_w_out, r_w_in, r_w_out, w_ple_gate, g_ple, w_ple, g_final):
    B, L, D = x.shape
    N = B * L
    q_rank = a_g_q.shape[-1]
    kv_rank = a_g_kv.shape[-1]
    idx_dim = a_g_ik.shape[-1]
    heads, _, qk_dim = a_w_uk.shape[1:]
    v_dim = a_w_uv.shape[-1]
    idx_heads = a_w_idx_q.shape[-1] // idx_dim
    a_width = heads * v_dim
    ple_dim = p.shape[-1]
    assert idx_heads == heads and 2 * idx_dim == LANES and L % KEY_CHUNK == 0
    assert KEY_CHUNK % Q_BLOCK == 0 and L % Q_BLOCK == 0 and heads % HEAD_GROUPS == 0
    assert q_rank % LANES == 0 and kv_rank % LANES == 0 and a_w_in.shape[0] == 1 and r_w_in.shape[0] == 1
    topk = min(TOPK_MAX, L // 4)
    x2 = x.reshape(N, D)
    p_layers = p.reshape(p.shape[0], N, ple_dim)
    row = lambda a: a.reshape(1, -1).astype(F32)

    w_in = a_w_in[0]
    o_ki = q_rank + kv_rank
    o_g = o_ki + idx_dim + idx_heads
    w_ki = jnp.pad(w_in[:, o_ki:o_g], ((0, 0), (0, LANES - (o_g - o_ki))))
    w0 = jnp.concatenate([w_in[:, :o_ki], w_ki, w_in[:, o_g:]], axis=1).astype(BF16)
    pad_k = lambda a: jnp.pad(a.reshape(1, -1).astype(F32), ((0, 0), (0, LANES - idx_dim)))
    wukt = jnp.swapaxes(a_w_uk[0], 1, 2).astype(BF16)
    tm0 = math.gcd(ROWS_DSA_PROJ, N)
    n0 = w0.shape[1]
    outs0 = pl.pallas_call(
        functools.partial(_proj0_kernel, q_rank=q_rank, kv_rank=kv_rank, idx_dim=idx_dim, heads=heads,
                          qk_dim=qk_dim, qk_scale=qk_dim ** -0.5 * math.log2(math.e),
                          widx_scale=idx_heads ** -0.5 * idx_dim ** -0.5),
        grid=(N // tm0,),
        in_specs=[pl.BlockSpec((tm0, D), lambda i: (i, 0)), _const_spec((1, D)), _const_spec((D, n0)),
                  _const_spec((1, q_rank)), _const_spec((1, kv_rank)), _const_spec((1, LANES)),
                  _const_spec((1, LANES)), _const_spec((q_rank, heads * qk_dim)),
                  _const_spec((heads, qk_dim, kv_rank)), _const_spec((q_rank, idx_heads * idx_dim))],
        out_specs=[pl.BlockSpec((heads, tm0, kv_rank), lambda i: (0, i, 0)),
                   pl.BlockSpec((tm0, idx_heads * idx_dim), lambda i: (i, 0)),
                   pl.BlockSpec((tm0, LANES), lambda i: (i, 0)),
                   pl.BlockSpec((tm0, LANES), lambda i: (i, 0)),
                   pl.BlockSpec((tm0, LANES), lambda i: (i, 0)),
                   pl.BlockSpec((tm0, kv_rank), lambda i: (i, 0)),
                   pl.BlockSpec((tm0, a_width), lambda i: (i, 0))],
        out_shape=[jax.ShapeDtypeStruct((heads, N, kv_rank), BF16),
                   jax.ShapeDtypeStruct((N, idx_heads * idx_dim), BF16),
                   jax.ShapeDtypeStruct((N, LANES), BF16),
                   jax.ShapeDtypeStruct((N, LANES), BF16),
                   jax.ShapeDtypeStruct((N, LANES), F32),
                   jax.ShapeDtypeStruct((N, kv_rank), BF16),
                   jax.ShapeDtypeStruct((N, a_width), BF16)],
        compiler_params=_params("parallel"),
        name="dsa_proj",
    )(x2, row(g_pre[0]), w0, row(a_g_q[0]), row(a_g_kv[0]), pad_k(a_g_ik[0]), pad_k(a_b_ik[0]),
      a_w_q_up[0].astype(BF16), wukt, a_w_idx_q[0].astype(BF16))
    qabs, qidx, ka, kb, wi, ckv, sg0 = outs0

    nb = L // Q_BLOCK
    nkc = L // KEY_CHUNK
    b3 = lambda a: a.reshape(B, L, a.shape[-1])
    qblk = lambda w: pl.BlockSpec((None, Q_BLOCK, w), lambda b, i: (b, i, 0))
    seq = lambda w: pl.BlockSpec((None, L, w), lambda b, i: (b, 0, 0))
    bias = pl.pallas_call(
        functools.partial(_select_kernel, topk=topk, heads=heads, idx_dim=idx_dim, max_chunks=nkc),
        grid=(B, nb),
        in_specs=[qblk(idx_heads * idx_dim), seq(LANES), seq(LANES), qblk(LANES)],
        out_specs=qblk(L),
        out_shape=jax.ShapeDtypeStruct((B, L, L), BF16),
        scratch_shapes=[pltpu.VMEM((L, Q_BLOCK), F32)],
        compiler_params=_params("parallel", "arbitrary"),
        name="dsa_select",
    )(b3(qidx), b3(ka), b3(kb), b3(wi))
    og0 = pl.pallas_call(
        functools.partial(_attend_kernel, heads=heads, v_dim=v_dim, max_chunks=nkc),
        grid=(B, nb),
        in_specs=[pl.BlockSpec((heads, None, Q_BLOCK, kv_rank), lambda b, i: (0, b, i, 0)),
                  seq(kv_rank), qblk(L), qblk(a_width), _const_spec((heads, kv_rank, v_dim))],
        out_specs=qblk(a_width),
        out_shape=jax.ShapeDtypeStruct((B, L, a_width), BF16),
        scratch_shapes=[pltpu.VMEM((heads * Q_BLOCK, L), F32),
                        pltpu.VMEM((heads * Q_BLOCK, L), BF16),
                        pltpu.VMEM((heads * Q_BLOCK, LANES), F32),
                        pltpu.VMEM((heads * Q_BLOCK, kv_rank), F32),
                        pltpu.VMEM((heads * Q_BLOCK, LANES), F32)],
        compiler_params=_params("parallel", "arbitrary"),
        name="dsa_attend",
    )(qabs.reshape(heads, B, L, kv_rank), b3(ckv), bias, b3(sg0), a_w_uv[0].astype(BF16))

    r_heads = RET_HEADS
    r_qk = D // r_heads
    r_v = 2 * D // r_heads
    r_width = r_heads * r_v
    assert r_w_in.shape[-1] == 2 * D + 2 * r_width and r_qk % LANES == 0 and L % R_CHUNK == 0
    wr = r_w_in[0].astype(BF16)
    wq, wk, wv, wg = wr[:, :D], wr[:, D:2 * D], wr[:, 2 * D:2 * D + r_width], wr[:, 2 * D + r_width:]
    angle = 1.0 / (10000.0 ** np.linspace(0.0, 1.0, r_qk // 2))
    theta = np.arange(L, dtype=np.float64)[:, None] * np.repeat(angle, 2)[None, :]
    even = (np.arange(r_qk) % 2 == 0)[None, :]
    cos_t = jnp.asarray(np.cos(theta), F32)
    sin_next = jnp.asarray(np.where(even, -np.sin(theta), 0.0), F32)
    sin_prev = jnp.asarray(np.where(even, 0.0, np.sin(theta)), F32)
    tm1 = math.gcd(ROWS_MID_PROJ, L)
    tok = lambda w: pl.BlockSpec((tm1, w), lambda b, i: (b * (L // tm1) + i, 0))
    tab = pl.BlockSpec((tm1, r_qk), lambda b, i: (i, 0))
    h1, rq, rk, rv, sg1 = pl.pallas_call(
        functools.partial(_mid_kernel, k_scale=r_qk ** -0.5, qk_dim=r_qk),
        grid=(B, L // tm1),
        in_specs=[tok(a_width), tok(D),
                  pl.BlockSpec((None, tm1, ple_dim), lambda b, i: (0, b * (L // tm1) + i, 0)),
                  _const_spec((a_width, D)), _const_spec((1, D)),
                  _const_spec((D, D)), _const_spec((ple_dim, D)), _const_spec((1, D)),
                  _const_spec((D, D)), _const_spec((D, D)), _const_spec((D, r_width)),
                  _const_spec((D, r_width)), tab, tab, tab],
        out_specs=[tok(D), tok(D), tok(D), tok(r_width), tok(r_width)],
        out_shape=[jax.ShapeDtypeStruct((N, D), F32), jax.ShapeDtypeStruct((N, D), BF16),
                   jax.ShapeDtypeStruct((N, D), BF16), jax.ShapeDtypeStruct((N, r_width), BF16),
                   jax.ShapeDtypeStruct((N, r_width), BF16)],
        compiler_params=_params("parallel", "arbitrary"),
        name="mid_proj",
    )(og0.reshape(N, a_width), x2, p_layers, a_w_out[0].astype(BF16), row(g_ple[0]),
      w_ple_gate[0].astype(BF16), w_ple[0].astype(BF16), row(g_pre[1]), wq, wk, wv, wg,
      cos_t, sin_next, sin_prev)

    log_gamma = np.log(1.0 - 2.0 ** (-5.0 - np.arange(r_heads, dtype=np.float64)))
    ci = np.arange(R_CHUNK, dtype=np.float64)
    diff = ci[:, None] - ci[None, :]
    dmask = jnp.asarray(np.where(diff[None] >= 0, np.exp(diff[None] * log_gamma[:, None, None]), 0.0), F32)
    xi = np.exp((ci[None, :] + 1.0) * log_gamma[:, None])
    zeta = np.exp((R_CHUNK - 1.0 - ci[None, :]) * log_gamma[:, None])
    decay = np.exp(R_CHUNK * log_gamma)
    xi_b = jnp.asarray(np.broadcast_to(xi[:, :, None], (r_heads, R_CHUNK, r_v)), F32)
    zeta_b = jnp.asarray(np.broadcast_to(zeta[:, :, None], (r_heads, R_CHUNK, r_qk)), F32)
    decay_b = jnp.asarray(np.broadcast_to(decay[:, None, None], (r_heads, 1, r_v)), F32)
    rows_step = R_CHUNK * math.gcd(R_STEP_CHUNKS, L // R_CHUNK)
    cblk = lambda w: pl.BlockSpec((None, rows_step, w), lambda b, c: (b, c, 0))
    og1 = pl.pallas_call(
        functools.partial(_ret_kernel, heads=r_heads, qk_dim=r_qk, v_dim=r_v),
        grid=(B, L // rows_step),
        in_specs=[cblk(D), cblk(D), cblk(r_width), cblk(r_width),
                  _const_spec((r_heads, R_CHUNK, R_CHUNK)), _const_spec((r_heads, R_CHUNK, r_v)),
                  _const_spec((r_heads, R_CHUNK, r_qk)), _const_spec((r_heads, 1, r_v))],
        out_specs=cblk(r_width),
        out_shape=jax.ShapeDtypeStruct((B, L, r_width), BF16),
        scratch_shapes=[pltpu.VMEM((r_heads, r_qk, r_v), F32)],
        compiler_params=_params("parallel", "arbitrary"),
        name="retention",
    )(rq.reshape(B, L, D), rk.reshape(B, L, D), rv.reshape(B, L, r_width), sg1.reshape(B, L, r_width),
      dmask, xi_b, zeta_b, decay_b)

    tm2 = math.gcd(ROWS_OUT_PROJ, N)
    tk2 = lambda w: pl.BlockSpec((tm2, w), lambda i: (i, 0))
    out = pl.pallas_call(
        _out_kernel,
        grid=(N // tm2,),
        in_specs=[tk2(r_width), tk2(D), pl.BlockSpec((None, tm2, ple_dim), lambda i: (1, i, 0)),
                  _const_spec((r_width, D)), _const_spec((1, D)),
                  _const_spec((D, D)), _const_spec((ple_dim, D)), _const_spec((1, D))],
        out_specs=tk2(D),
        out_shape=jax.ShapeDtypeStruct((N, D), F32),
        compiler_params=_params("parallel"),
        name="out_proj",
    )(og1.reshape(N, r_width), h1, p_layers, r_w_out[0].astype(BF16), row(g_ple[1]),
      w_ple_gate[1].astype(BF16), w_ple[1].astype(BF16), row(g_final))
    return out.reshape(B, L, D)
```

```python
import functools
import math

import jax
import jax.numpy as jnp
import numpy as np
from jax import lax
from jax.experimental import pallas as pl
from jax.experimental.pallas import tpu as pltpu

F32 = jnp.float32
BF16 = jnp.bfloat16

NORM_EPS = 1e-6
TOPK_MAX = 256
Q_BLOCK = 256
KEY_CHUNK = 256
R_CHUNK = 128
LANES = 128
MASK_BIAS = -1e30
VALUE_BISECT_MAX = 34
VMEM_LIMIT = 56 * 1024 * 1024
HEAD_GROUPS = 2
RET_HEADS = 4
R_STEP_CHUNKS = 8
ROWS_DSA_PROJ = 1024
ROWS_MID_PROJ = 512
ROWS_OUT_PROJ = 1024
MID_SPLIT = 2
OUT_SPLIT = 4


def _passes_per_body(n_chunks):
    return max(1, min(4, round(6 / n_chunks)))


def _rms(x, g):
    return x * lax.rsqrt(jnp.mean(x * x, axis=-1, keepdims=True) + NORM_EPS) * g


def _dot(a, b):
    return jnp.dot(a, b, preferred_element_type=F32)


def _dot_nt(a, b):
    return lax.dot_general(a, b, (((1,), (1,)), ((), ())), preferred_element_type=F32)


def _dot_tn(a, b):
    return lax.dot_general(a, b, (((0,), (0,)), ((), ())), preferred_element_type=F32)


def _const_spec(shape):
    nd = len(shape)
    return pl.BlockSpec(shape, lambda *_: (0,) * nd, pipeline_mode=pl.Buffered(1))


def _proj0_kernel(x_ref, gpre_ref, w0_ref, gq_ref, gkv_ref, gik_ref, bik_ref, wqup_ref, wukt_ref,
                  widxq_ref, qabs_ref, qidx_ref, ka_ref, kb_ref, wi_ref, ckv_ref, sg_ref,
                  *, q_rank, kv_rank, idx_dim, heads, qk_dim, qk_scale, widx_scale):
    hn = _rms(x_ref[...], gpre_ref[...])
    z = _dot(hn.astype(BF16), w0_ref[...])
    o_kv = q_rank
    o_ki = q_rank + kv_rank
    o_g = o_ki + LANES
    cq = _rms(z[:, :o_kv], gq_ref[...]).astype(BF16)
    ckv_ref[...] = _rms(z[:, o_kv:o_ki], gkv_ref[...]).astype(BF16)
    ki = z[:, o_ki:o_g]
    lane = lax.broadcasted_iota(jnp.int32, (1, LANES), 1)
    in_k = lane < idx_dim
    mu = jnp.sum(jnp.where(in_k, ki, 0.0), axis=-1, keepdims=True) * (1.0 / idx_dim)
    d = jnp.where(in_k, ki - mu, 0.0)
    var = jnp.sum(d * d, axis=-1, keepdims=True) * (1.0 / idx_dim)
    kn = d * lax.rsqrt(var + NORM_EPS) * gik_ref[...] + bik_ref[...]
    ka_ref[...] = kn.astype(BF16)
    kb_ref[...] = pltpu.roll(kn, idx_dim, 1).astype(BF16)
    wi_ref[...] = ki * widx_scale
    g = z[:, o_g:]
    sg_ref[...] = (g * jax.nn.sigmoid(g)).astype(BF16)
    q = _dot(cq, wqup_ref[...])
    for h in range(heads):
        qh = q[:, h * qk_dim:(h + 1) * qk_dim].astype(BF16)
        qabs_ref[h] = (_dot(qh, wukt_ref[h]) * qk_scale).astype(BF16)
    qidx_ref[...] = _dot(cq, widxq_ref[...]).astype(BF16)


def _select_kernel(qidx_ref, ka_ref, kb_ref, wi_ref, bias_ref, st_ref, *, topk, heads, idx_dim, max_chunks):
    qb = pl.program_id(1)
    n_kc = (qb * Q_BLOCK) // KEY_CHUNK + 1
    kf = float(topk)
    pairs = heads // 2

    qi = qidx_ref[...]
    qst = jnp.concatenate([qi[:, j * LANES:(j + 1) * LANES] for j in range(pairs)], axis=0)
    wit = wi_ref[...].T
    w_rows = [wit[idx_dim + h:idx_dim + h + 1, :] for h in range(heads)]
    tq = qb * Q_BLOCK + lax.broadcasted_iota(jnp.int32, (1, Q_BLOCK), 1)
    n_causal = (tq + 1).astype(F32)
    small = n_causal <= kf
    row_i = lax.broadcasted_iota(jnp.int32, (KEY_CHUNK, KEY_CHUNK), 0)
    col_i = lax.broadcasted_iota(jnp.int32, (KEY_CHUNK, KEY_CHUNK), 1)
    eye = (row_i == col_i).astype(BF16)[:Q_BLOCK, :Q_BLOCK]
    lower_tri = (col_i <= row_i).astype(BF16)

    def fold8(x, op):
        return op(x.reshape(x.shape[0] // 8, 8, Q_BLOCK), axis=0)

    def variant(nc):
        chunks = [(c, c * KEY_CHUNK) for c in range(nc)]

        mn = mx = cgt = cge = None
        for c, k0 in chunks:
            sa = _dot_nt(ka_ref[k0:k0 + KEY_CHUNK, :], qst)
            sb = _dot_nt(kb_ref[k0:k0 + KEY_CHUNK, :], qst)
            acc = None
            for j in range(pairs):
                ta = jnp.maximum(sa[:, j * Q_BLOCK:(j + 1) * Q_BLOCK], 0.0) * w_rows[2 * j]
                tb = jnp.maximum(sb[:, j * Q_BLOCK:(j + 1) * Q_BLOCK], 0.0) * w_rows[2 * j + 1]
                acc = ta + tb if acc is None else acc + ta + tb
            if c == nc - 1:
                kpos = k0 + lax.broadcasted_iota(jnp.int32, (KEY_CHUNK, Q_BLOCK), 0)
                causal = kpos <= tq
                s = jnp.where(causal, acc, -jnp.inf)
                s_min = jnp.where(causal, acc, jnp.inf)
            else:
                s = s_min = acc
            st_ref[k0:k0 + KEY_CHUNK, :] = s
            parts = (fold8(s_min, jnp.min), fold8(s, jnp.max),
                     fold8(jnp.where(s > 0.0, 1.0, 0.0), jnp.sum),
                     fold8(jnp.where(s >= 0.0, 1.0, 0.0), jnp.sum))
            if mn is None:
                mn, mx, cgt, cge = parts
            else:
                mn, mx = jnp.minimum(mn, parts[0]), jnp.maximum(mx, parts[1])
                cgt, cge = cgt + parts[2], cge + parts[3]
        mn = jnp.min(mn, axis=0, keepdims=True)
        mx = jnp.max(mx, axis=0, keepdims=True)
        cgt0 = jnp.sum(cgt, axis=0, keepdims=True)
        cge0 = jnp.sum(cge, axis=0, keepdims=True)

        def count_gt(ref, thr):
            acc = None
            for _, k0 in chunks:
                part = fold8(jnp.where(ref[k0:k0 + KEY_CHUNK, :] > thr, 1.0, 0.0), jnp.sum)
                acc = part if acc is None else acc + part
            return jnp.sum(acc, axis=0, keepdims=True)

        pos = cgt0 >= kf
        zero = jnp.logical_and(jnp.logical_not(pos), cge0 >= kf)
        zero = jnp.logical_and(zero, jnp.logical_not(small))
        lo = jnp.where(pos, 0.0, jnp.where(zero, 0.0, mn * 1.0001))
        hi = jnp.where(pos, mx, 0.0)
        c_lo = jnp.where(pos, cgt0, jnp.where(zero, cgt0, n_causal))
        c_hi = jnp.where(pos, 0.0, cgt0)
        lo = jnp.where(small, -jnp.inf, lo)
        hi = jnp.where(small, -jnp.inf, hi)
        searching = jnp.logical_not(jnp.logical_or(small, zero))

        def unresolved(c_lo):
            bad = jnp.logical_and(c_lo != kf, searching)
            return (jnp.max(jnp.where(bad, 1.0, 0.0)) > 0.0).astype(jnp.int32)

        def bis_cond(carry):
            return jnp.logical_and(carry[0] < VALUE_BISECT_MAX, carry[1] > 0)

        def bisect(state):
            lo, hi, c_lo, c_hi = state
            mid = 0.5 * (lo + hi)
            c = count_gt(st_ref, mid)
            ge = c >= kf
            return (jnp.where(ge, mid, lo), jnp.where(ge, hi, mid),
                    jnp.where(ge, c, c_lo), jnp.where(ge, c_hi, c))

        def bis_body(carry):
            it, _, state = carry
            flag = unresolved(state[2])
            for _ in range(_passes_per_body(nc)):
                state = bisect(state)
            return it + _passes_per_body(nc), flag, state

        _, _, (lo, hi, c_lo, c_hi) = lax.while_loop(
            bis_cond, bis_body, (jnp.int32(0), jnp.int32(1), (lo, hi, c_lo, c_hi)))
        done = jnp.logical_or(c_lo == kf, small)
        thr = jnp.where(done, lo, hi)

        @pl.when(jnp.max(jnp.where(done, 0.0, 1.0)) > 0.0)
        def _():
            need = jnp.where(done, 0.0, kf - c_hi)
            tied_f = jnp.where(jnp.logical_or(done, zero), 0.0, 1.0)
            zero_f = jnp.where(zero, 1.0, 0.0)
            before = jnp.zeros((1, Q_BLOCK), F32)
            for _, k0 in chunks:
                s = st_ref[k0:k0 + KEY_CHUNK, :]
                band_f = (jnp.where(s > lo, 1.0, 0.0) * jnp.where(s <= hi, 1.0, 0.0) * tied_f
                          + jnp.where(s == 0.0, 1.0, 0.0) * zero_f)
                rank = _dot(lower_tri, band_f.astype(BF16)) + before
                take_f = band_f * jnp.where(rank <= need, 1.0, 0.0)
                st_ref[k0:k0 + KEY_CHUNK, :] = jnp.where(take_f > 0.5, jnp.inf, s)
                before = before + jnp.sum(fold8(band_f, jnp.sum), axis=0, keepdims=True)

        for _, k0 in chunks:
            sel_t = jnp.where(st_ref[k0:k0 + KEY_CHUNK, :] > thr, 1.0, 0.0).astype(BF16)
            sel = _dot_nt(eye, sel_t)
            bias_ref[:, k0:k0 + KEY_CHUNK] = ((sel - 1.0) * (-MASK_BIAS)).astype(BF16)
        if nc < max_chunks:
            bias_ref[:, nc * KEY_CHUNK:] = jnp.full((Q_BLOCK, (max_chunks - nc) * KEY_CHUNK), MASK_BIAS, BF16)

    for nc in range(1, max_chunks + 1):
        pl.when(n_kc == nc)(functools.partial(variant, nc))


def _attend_kernel(qabs_ref, ckv_ref, bias_ref, sg_ref, wuv_ref, og_ref, s_ref, p_ref, m_ref, acc_ref,
                   l_ref, *, heads, v_dim, max_chunks):
    n_kc = (pl.program_id(1) * Q_BLOCK) // KEY_CHUNK + 1
    sub = KEY_CHUNK // LANES
    gsz = heads // HEAD_GROUPS
    grows = gsz * Q_BLOCK

    def variant(nc):
        nk = nc * KEY_CHUNK
        q_all = jnp.concatenate([qabs_ref[h] for h in range(heads)], axis=0)
        for c in range(nc):
            k0 = c * KEY_CHUNK
            s = _dot_nt(q_all, ckv_ref[k0:k0 + KEY_CHUNK, :])
            bias = bias_ref[:, k0:k0 + KEY_CHUNK].astype(F32)
            for h in range(heads):
                rows = slice(h * Q_BLOCK, (h + 1) * Q_BLOCK)
                sh = s[rows, :] + bias
                s_ref[rows, k0:k0 + KEY_CHUNK] = sh
                mh = sh[:, :LANES]
                for t in range(1, sub):
                    mh = jnp.maximum(mh, sh[:, t * LANES:(t + 1) * LANES])
                m_ref[rows, :] = mh if c == 0 else jnp.maximum(m_ref[rows, :], mh)
        for g in range(HEAD_GROUPS):
            for i in range(gsz):
                rows = slice((g * gsz + i) * Q_BLOCK, (g * gsz + i + 1) * Q_BLOCK)
                m = jnp.broadcast_to(jnp.max(m_ref[rows, :], axis=-1, keepdims=True), (Q_BLOCK, LANES))
                lsum = None
                for t in range(nc * sub):
                    pt = jnp.exp2(s_ref[rows, t * LANES:(t + 1) * LANES] - m)
                    p_ref[rows, t * LANES:(t + 1) * LANES] = pt.astype(BF16)
                    lsum = pt if lsum is None else lsum + pt
                l_ref[rows, :] = lsum
            acc_ref[g * grows:(g + 1) * grows, :] = _dot(p_ref[g * grows:(g + 1) * grows, :nk], ckv_ref[:nk, :])

    for nc in range(1, max_chunks + 1):
        pl.when(n_kc == nc)(functools.partial(variant, nc))

    inv_l = 1.0 / jnp.sum(l_ref[...], axis=-1, keepdims=True)
    for h in range(heads):
        oh = (acc_ref[h * Q_BLOCK:(h + 1) * Q_BLOCK, :] * inv_l[h * Q_BLOCK:(h + 1) * Q_BLOCK]).astype(BF16)
        ov = _dot(oh, wuv_ref[h])
        og_ref[:, h * v_dim:(h + 1) * v_dim] = (
            ov * sg_ref[:, h * v_dim:(h + 1) * v_dim].astype(F32)).astype(BF16)


def _rotate(x, cos, sin_next, sin_prev):
    nxt = pltpu.roll(x, LANES - 1, 1)
    prv = pltpu.roll(x, 1, 1)
    return x * cos + nxt * sin_next + prv * sin_prev


def _mid_kernel(og_ref, x_ref, p_ref, wout_ref, gple_ref, wpg_ref, wple_ref, gpre_ref,
                wq_ref, wk_ref, wv_ref, wg_ref, cos_ref, sn_ref, sp_ref,
                h_ref, q_ref, k_ref, v_ref, sg_ref, *, k_scale, qk_dim):
    n_rows = x_ref.shape[0]
    halves = [slice(i * (n_rows // MID_SPLIT), (i + 1) * (n_rows // MID_SPLIT)) for i in range(MID_SPLIT)]
    hs = [x_ref[r, :] + _dot(og_ref[r, :], wout_ref[...]) for r in halves]
    gates = [jax.nn.sigmoid(_dot(_rms(h, gple_ref[...]).astype(BF16), wpg_ref[...])) for h in hs]
    hs = [h + _dot(p_ref[r, :].astype(BF16), wple_ref[...]) * gt for h, r, gt in zip(hs, halves, gates)]
    for h, r in zip(hs, halves):
        h_ref[r, :] = h
    hbs = [_rms(h, gpre_ref[...]).astype(BF16) for h in hs]
    for hb, r in zip(hbs, halves):
        q = _dot(hb, wq_ref[...])
        k = _dot(hb, wk_ref[...]) * k_scale
        for j in range(q.shape[1] // LANES):
            t = (j * LANES) % qk_dim
            cs = cos_ref[r, t:t + LANES]
            sn = sn_ref[r, t:t + LANES]
            sp = sp_ref[r, t:t + LANES]
            q_ref[r, j * LANES:(j + 1) * LANES] = _rotate(q[:, j * LANES:(j + 1) * LANES], cs, sn, sp).astype(BF16)
            k_ref[r, j * LANES:(j + 1) * LANES] = _rotate(k[:, j * LANES:(j + 1) * LANES], cs, sn, sp).astype(BF16)
        v_ref[r, :] = _dot(hb, wv_ref[...]).astype(BF16)
        g = _dot(hb, wg_ref[...])
        sg_ref[r, :] = (g * jax.nn.sigmoid(g)).astype(BF16)


def _ret_kernel(q_ref, k_ref, v_ref, sg_ref, dmask_ref, xi_ref, zeta_ref, decay_ref, og_ref,
                state_ref, *, heads, qk_dim, v_dim):
    @pl.when(pl.program_id(1) == 0)
    def _():
        state_ref[...] = jnp.zeros(state_ref.shape, F32)

    hs = range(heads)
    for c in range(q_ref.shape[0] // R_CHUNK):
        r = slice(c * R_CHUNK, (c + 1) * R_CHUNK)
        qs = [q_ref[r, h * qk_dim:(h + 1) * qk_dim] for h in hs]
        ks = [k_ref[r, h * qk_dim:(h + 1) * qk_dim] for h in hs]
        vs = [v_ref[r, h * v_dim:(h + 1) * v_dim] for h in hs]
        ss = [(_dot_nt(qs[h], ks[h]) * dmask_ref[h]).astype(BF16) for h in hs]
        crosses = [_dot(qs[h], state_ref[h].astype(BF16)) * xi_ref[h] for h in hs]
        os_ = [_dot(ss[h], vs[h]) + crosses[h] for h in hs]
        kzs = [(ks[h].astype(F32) * zeta_ref[h]).astype(BF16) for h in hs]
        for h in hs:
            state_ref[h] = state_ref[h] * decay_ref[h] + _dot_tn(kzs[h], vs[h])
        for h in hs:
            o = os_[h]
            mu = jnp.mean(o, axis=-1, keepdims=True)
            d = o - mu
            var = jnp.mean(d * d, axis=-1, keepdims=True)
            on = d * lax.rsqrt(var + NORM_EPS)
            og_ref[r, h * v_dim:(h + 1) * v_dim] = (
                on * sg_ref[r, h * v_dim:(h + 1) * v_dim].astype(F32)).astype(BF16)


def _out_kernel(og_ref, h_ref, p_ref, wout_ref, gple_ref, wpg_ref, wple_ref, gfin_ref, o_ref):
    n_rows = h_ref.shape[0]
    parts = [slice(i * (n_rows // OUT_SPLIT), (i + 1) * (n_rows // OUT_SPLIT)) for i in range(OUT_SPLIT)]
    hs = [h_ref[r, :] + _dot(og_ref[r, :], wout_ref[...]) for r in parts]
    gates = [jax.nn.sigmoid(_dot(_rms(h, gple_ref[...]).astype(BF16), wpg_ref[...])) for h in hs]
    for h, r, gt in zip(hs, parts, gates):
        o_ref[r, :] = _rms(h + _dot(p_ref[r, :].astype(BF16), wple_ref[...]) * gt, gfin_ref[...])


def _params(*sem):
    return pltpu.CompilerParams(dimension_semantics=sem, vmem_limit_bytes=VMEM_LIMIT)


def kernel(x, p, g_pre, a_w_in, a_g_q, a_g_kv, a_w_q_up, a_w_idx_q, a_g_ik, a_b_ik, a_w_uk, a_w_uv,
           a_w_out, r_w_in, r_w_out, w_ple_gate, g_ple, w_ple, g_final):
    B, L, D = x.shape
    N = B * L
    q_rank = a_g_q.shape[-1]
    kv_rank = a_g_kv.shape[-1]
    idx_dim = a_g_ik.shape[-1]
    heads, _, qk_dim = a_w_uk.shape[1:]
    v_dim = a_w_uv.shape[-1]
    idx_heads = a_w_idx_q.shape[-1] // idx_dim
    a_width = heads * v_dim
    ple_dim = p.shape[-1]
    assert idx_heads == heads and 2 * idx_dim == LANES and L % KEY_CHUNK == 0
    assert KEY_CHUNK % Q_BLOCK == 0 and L % Q_BLOCK == 0 and heads % HEAD_GROUPS == 0
    assert q_rank % LANES == 0 and kv_rank % LANES == 0 and a_w_in.shape[0] == 1 and r_w_in.shape[0] == 1
    topk = min(TOPK_MAX, L // 4)
    x2 = x.reshape(N, D)
    p_layers = p.reshape(p.shape[0], N, ple_dim)
    row = lambda a: a.reshape(1, -1).astype(F32)

    w_in = a_w_in[0]
    o_ki = q_rank + kv_rank
    o_g = o_ki + idx_dim + idx_heads
    w_ki = jnp.pad(w_in[:, o_ki:o_g], ((0, 0), (0, LANES - (o_g - o_ki))))
    w0 = jnp.concatenate([w_in[:, :o_ki], w_ki, w_in[:, o_g:]], axis=1).astype(BF16)
    pad_k = lambda a: jnp.pad(a.reshape(1, -1).astype(F32), ((0, 0), (0, LANES - idx_dim)))
    wukt = jnp.swapaxes(a_w_uk[0], 1, 2).astype(BF16)
    tm0 = math.gcd(ROWS_DSA_PROJ, N)
    n0 = w0.shape[1]
    outs0 = pl.pallas_call(
        functools.partial(_proj0_kernel, q_rank=q_rank, kv_rank=kv_rank, idx_dim=idx_dim, heads=heads,
                          qk_dim=qk_dim, qk_scale=qk_dim ** -0.5 * math.log2(math.e),
                          widx_scale=idx_heads ** -0.5 * idx_dim ** -0.5),
        grid=(N // tm0,),
        in_specs=[pl.BlockSpec((tm0, D), lambda i: (i, 0)), _const_spec((1, D)), _const_spec((D, n0)),
                  _const_spec((1, q_rank)), _const_spec((1, kv_rank)), _const_spec((1, LANES)),
                  _const_spec((1, LANES)), _const_spec((q_rank, heads * qk_dim)),
                  _const_spec((heads, qk_dim, kv_rank)), _const_spec((q_rank, idx_heads * idx_dim))],
        out_specs=[pl.BlockSpec((heads, tm0, kv_rank), lambda i: (0, i, 0)),
                   pl.BlockSpec((tm0, idx_heads * idx_dim), lambda i: (i, 0)),
                   pl.BlockSpec((tm0, LANES), lambda i: (i, 0)),
                   pl.BlockSpec((tm0, LANES), lambda i: (i, 0)),
                   pl.BlockSpec((tm0, LANES), lambda i: (i, 0)),
                   pl.BlockSpec((tm0, kv_rank), lambda i: (i, 0)),
                   pl.BlockSpec((tm0, a_width), lambda i: (i, 0))],
        out_shape=[jax.ShapeDtypeStruct((heads, N, kv_rank), BF16),
                   jax.ShapeDtypeStruct((N, idx_heads * idx_dim), BF16),
                   jax.ShapeDtypeStruct((N, LANES), BF16),
                   jax.ShapeDtypeStruct((N, LANES), BF16),
                   jax.ShapeDtypeStruct((N, LANES), F32),
                   jax.ShapeDtypeStruct((N, kv_rank), BF16),
                   jax.ShapeDtypeStruct((N, a_width), BF16)],
        compiler_params=_params("parallel"),
        name="dsa_proj",
    )(x2, row(g_pre[0]), w0, row(a_g_q[0]), row(a_g_kv[0]), pad_k(a_g_ik[0]), pad_k(a_b_ik[0]),
      a_w_q_up[0].astype(BF16), wukt, a_w_idx_q[0].astype(BF16))
    qabs, qidx, ka, kb, wi, ckv, sg0 = outs0

    nb = L // Q_BLOCK
    nkc = L // KEY_CHUNK
    b3 = lambda a: a.reshape(B, L, a.shape[-1])
    qblk = lambda w: pl.BlockSpec((None, Q_BLOCK, w), lambda b, i: (b, i, 0))
    seq = lambda w: pl.BlockSpec((None, L, w), lambda b, i: (b, 0, 0))
    bias = pl.pallas_call(
        functools.partial(_select_kernel, topk=topk, heads=heads, idx_dim=idx_dim, max_chunks=nkc),
        grid=(B, nb),
        in_specs=[qblk(idx_heads * idx_dim), seq(LANES), seq(LANES), qblk(LANES)],
        out_specs=qblk(L),
        out_shape=jax.ShapeDtypeStruct((B, L, L), BF16),
        scratch_shapes=[pltpu.VMEM((L, Q_BLOCK), F32)],
        compiler_params=_params("parallel", "arbitrary"),
        name="dsa_select",
    )(b3(qidx), b3(ka), b3(kb), b3(wi))
    og0 = pl.pallas_call(
        functools.partial(_attend_kernel, heads=heads, v_dim=v_dim, max_chunks=nkc),
        grid=(B, nb),
        in_specs=[pl.BlockSpec((heads, None, Q_BLOCK, kv_rank), lambda b, i: (0, b, i, 0)),
                  seq(kv_rank), qblk(L), qblk(a_width), _const_spec((heads, kv_rank, v_dim))],
        out_specs=qblk(a_width),
        out_shape=jax.ShapeDtypeStruct((B, L, a_width), BF16),
        scratch_shapes=[pltpu.VMEM((heads * Q_BLOCK, L), F32),
                        pltpu.VMEM((heads * Q_BLOCK, L), BF16),
                        pltpu.VMEM((heads * Q_BLOCK, LANES), F32),
                        pltpu.VMEM((heads * Q_BLOCK, kv_rank), F32),
                        pltpu.VMEM((heads * Q_BLOCK, LANES), F32)],
        compiler_params=_params("parallel", "arbitrary"),
        name="dsa_attend",
    )(qabs.reshape(heads, B, L, kv_rank), b3(ckv), bias, b3(sg0), a_w_uv[0].astype(BF16))

    r_heads = RET_HEADS
    r_qk = D // r_heads
    r_v = 2 * D // r_heads
    r_width = r_heads * r_v
    assert r_w_in.shape[-1] == 2 * D + 2 * r_width and r_qk % LANES == 0 and L % R_CHUNK == 0
    wr = r_w_in[0].astype(BF16)
    wq, wk, wv, wg = wr[:, :D], wr[:, D:2 * D], wr[:, 2 * D:2 * D + r_width], wr[:, 2 * D + r_width:]
    angle = 1.0 / (10000.0 ** np.linspace(0.0, 1.0, r_qk // 2))
    theta = np.arange(L, dtype=np.float64)[:, None] * np.repeat(angle, 2)[None, :]
    even = (np.arange(r_qk) % 2 == 0)[None, :]
    cos_t = jnp.asarray(np.cos(theta), F32)
    sin_next = jnp.asarray(np.where(even, -np.sin(theta), 0.0), F32)
    sin_prev = jnp.asarray(np.where(even, 0.0, np.sin(theta)), F32)
    tm1 = math.gcd(ROWS_MID_PROJ, L)
    tok = lambda w: pl.BlockSpec((tm1, w), lambda b, i: (b * (L // tm1) + i, 0))
    tab = pl.BlockSpec((tm1, r_qk), lambda b, i: (i, 0))
    h1, rq, rk, rv, sg1 = pl.pallas_call(
        functools.partial(_mid_kernel, k_scale=r_qk ** -0.5, qk_dim=r_qk),
        grid=(B, L // tm1),
        in_specs=[tok(a_width), tok(D),
                  pl.BlockSpec((None, tm1, ple_dim), lambda b, i: (0, b * (L // tm1) + i, 0)),
                  _const_spec((a_width, D)), _const_spec((1, D)),
                  _const_spec((D, D)), _const_spec((ple_dim, D)), _const_spec((1, D)),
                  _const_spec((D, D)), _const_spec((D, D)), _const_spec((D, r_width)),
                  _const_spec((D, r_width)), tab, tab, tab],
        out_specs=[tok(D), tok(D), tok(D), tok(r_width), tok(r_width)],
        out_shape=[jax.ShapeDtypeStruct((N, D), F32), jax.ShapeDtypeStruct((N, D), BF16),
                   jax.ShapeDtypeStruct((N, D), BF16), jax.ShapeDtypeStruct((N, r_width), BF16),
                   jax.ShapeDtypeStruct((N, r_width), BF16)],
        compiler_params=_params("parallel", "arbitrary"),
        name="mid_proj",
    )(og0.reshape(N, a_width), x2, p_layers, a_w_out[0].astype(BF16), row(g_ple[0]),
      w_ple_gate[0].astype(BF16), w_ple[0].astype(BF16), row(g_pre[1]), wq, wk, wv, wg,
      cos_t, sin_next, sin_prev)

    log_gamma = np.log(1.0 - 2.0 ** (-5.0 - np.arange(r_heads, dtype=np.float64)))
    ci = np.arange(R_CHUNK, dtype=np.float64)
    diff = ci[:, None] - ci[None, :]
    dmask = jnp.asarray(np.where(diff[None] >= 0, np.exp(diff[None] * log_gamma[:, None, None]), 0.0), F32)
    xi = np.exp((ci[None, :] + 1.0) * log_gamma[:, None])
    zeta = np.exp((R_CHUNK - 1.0 - ci[None, :]) * log_gamma[:, None])
    decay = np.exp(R_CHUNK * log_gamma)
    xi_b = jnp.asarray(np.broadcast_to(xi[:, :, None], (r_heads, R_CHUNK, r_v)), F32)
    zeta_b = jnp.asarray(np.broadcast_to(zeta[:, :, None], (r_heads, R_CHUNK, r_qk)), F32)
    decay_b = jnp.asarray(np.broadcast_to(decay[:, None, None], (r_heads, 1, r_v)), F32)
    rows_step = R_CHUNK * math.gcd(R_STEP_CHUNKS, L // R_CHUNK)
    cblk = lambda w: pl.BlockSpec((None, rows_step, w), lambda b, c: (b, c, 0))
    og1 = pl.pallas_call(
        functools.partial(_ret_kernel, heads=r_heads, qk_dim=r_qk, v_dim=r_v),
        grid=(B, L // rows_step),
        in_specs=[cblk(D), cblk(D), cblk(r_width), cblk(r_width),
                  _const_spec((r_heads, R_CHUNK, R_CHUNK)), _const_spec((r_heads, R_CHUNK, r_v)),
                  _const_spec((r_heads, R_CHUNK, r_qk)), _const_spec((r_heads, 1, r_v))],
        out_specs=cblk(r_width),
        out_shape=jax.ShapeDtypeStruct((B, L, r_width), BF16),
        scratch_shapes=[pltpu.VMEM((r_heads, r_qk, r_v), F32)],
        compiler_params=_params("parallel", "arbitrary"),
        name="retention",
    )(rq.reshape(B, L, D), rk.reshape(B, L, D), rv.reshape(B, L, r_width), sg1.reshape(B, L, r_width),
      dmask, xi_b, zeta_b, decay_b)

    tm2 = math.gcd(ROWS_OUT_PROJ, N)
    tk2 = lambda w: pl.BlockSpec((tm2, w), lambda i: (i, 0))
    out = pl.pallas_call(
        _out_kernel,
        grid=(N // tm2,),
        in_specs=[tk2(r_width), tk2(D), pl.BlockSpec((None, tm2, ple_dim), lambda i: (1, i, 0)),
                  _const_spec((r_width, D)), _const_spec((1, D)),
                  _const_spec((D, D)), _const_spec((ple_dim, D)), _const_spec((1, D))],
        out_specs=tk2(D),
        out_shape=jax.ShapeDtypeStruct((N, D), F32),
        compiler_params=_params("parallel"),
        name="out_proj",
    )(og1.reshape(N, r_width), h1, p_layers, r_w_out[0].astype(BF16), row(g_ple[1]),
      w_ple_gate[1].astype(BF16), w_ple[1].astype(BF16), row(g_final))
    return out.reshape(B, L, D)
```
